```python
import numpy as np
import jax
import jax.numpy as jnp
from jax import lax

D_MODEL = 4096
BATCH = 4
SEQ = 2048
DEPTH = 2

HEAD_DIM = 128
Q_BLK = 128

NSA_HEADS = 12
NSA_KV_HEADS = 3
NSA_GROUP = NSA_HEADS // NSA_KV_HEADS
CMP_LEN = 32
CMP_STRIDE = 16
SLC_BLK = 64
N_SEL = 16
WIN = 512
SLC_Q_CHUNK = 32

FOX_HEADS = 8

DIL_PAIRS = ((128, 1), (512, 4), (2048, 16))
DIL_GROUPS = 3
DIL_HEADS_PER_GROUP = 4
DIL_HEADS = DIL_GROUPS * DIL_HEADS_PER_GROUP

ALIBI_HEADS = NSA_HEADS + DIL_HEADS

NSA_Q_W = NSA_HEADS * HEAD_DIM
NSA_KV_W = NSA_KV_HEADS * HEAD_DIM
FOX_W = FOX_HEADS * HEAD_DIM
DIL_W = DIL_HEADS * HEAD_DIM
OFF_NSA_Q = 0
OFF_NSA_KV = OFF_NSA_Q + NSA_Q_W
OFF_NSA_GATE = OFF_NSA_KV + 6 * NSA_KV_W
OFF_FOX = OFF_NSA_GATE + 3 * NSA_HEADS
OFF_FOX_F = OFF_FOX + 3 * FOX_W
OFF_DIL = OFF_FOX_F + FOX_HEADS
IN_COLS = OFF_DIL + 3 * DIL_W

OUT_A = NSA_Q_W
OUT_B = FOX_W
OUT_C = DIL_HEADS_PER_GROUP * HEAD_DIM
BRANCH_ROWS = OUT_A + OUT_B + OUT_C

PEER_HEADS = 8
PEER_NKEYS = 128
PEER_EXPERTS = PEER_NKEYS ** 2
PEER_TOPK = 16
PEER_QDIM = 256
PEER_HALF = PEER_QDIM // 2
PEER_TOK_CHUNK = 64

ALPHA = (2.0 * DEPTH) ** 0.25
BETA = (8.0 * DEPTH) ** -0.25
LN_EPS = 1e-5
NEG = -1e30
BIG = 1e30

kernel_name = "hybrid_nsa_fox_dilated_peer_deepnorm_adaln"


def alibi_slopes(n):
    return (np.float32(2.0) ** (-8.0 * np.arange(1, n + 1, dtype=np.float32) / n)).astype(np.float32)


def layer_norm(h, g, b):
    hf = h.astype(jnp.float32)
    mu = hf.mean(-1, keepdims=True)
    var = jnp.square(hf - mu).mean(-1, keepdims=True)
    return ((hf - mu) * lax.rsqrt(var + LN_EPS) * g + b).astype(h.dtype)


def banded_attention(q, k, v, max_dist, slopes, step):
    B, H, T, hd = q.shape
    nprev = -(-max_dist // Q_BLK)
    nb = -(-T // Q_BLK)
    padq = nb * Q_BLK - T
    qb = jnp.pad(q, ((0, 0), (0, 0), (0, padq), (0, 0))).reshape(B, H, nb, Q_BLK, hd)

    def windows(a):
        ab = jnp.pad(a, ((0, 0), (0, 0), (nprev * Q_BLK, padq), (0, 0))).reshape(B, H, nb + nprev, Q_BLK, hd)
        return jnp.concatenate([ab[:, :, j:j + nb] for j in range(nprev + 1)], axis=3)

    kw, vw = windows(k), windows(v)
    qpos = np.arange(nb * Q_BLK).reshape(nb, Q_BLK)
    kpos = np.arange(nb)[:, None] * Q_BLK - nprev * Q_BLK + np.arange((nprev + 1) * Q_BLK)[None, :]
    dist = qpos[:, :, None] - kpos[:, None, :]
    valid = (dist >= 0) & (dist <= max_dist) & (kpos[:, None, :] >= 0)
    s = jnp.einsum('bhnqd,bhnkd->bhnqk', qb, kw).astype(jnp.float32) * (hd ** -0.5)
    s = s - (slopes * step)[None, :, None, None, None] * dist.astype(np.float32)
    s = jnp.where(valid, s, -jnp.inf)
    lse = jax.nn.logsumexp(s, axis=-1)
    p = jnp.exp(s - lse[..., None])
    out = jnp.einsum('bhnqk,bhnkd->bhnqd', p.astype(vw.dtype), vw)
    return (out.reshape(B, H, nb * Q_BLK, hd)[:, :, :T], lse.reshape(B, H, nb * Q_BLK)[:, :, :T])


def nsa_attention(q, k_cmp, v_cmp, k_slc, v_slc, k_win, v_win, gates, pe, w1, w2, slopes):
    B, H, T, hd = q.shape
    Hkv, G = NSA_KV_HEADS, NSA_GROUP
    scale = hd ** -0.5
    qg = q.reshape(B, Hkv, G, T, hd)
    pos = np.arange(T)

    n_cmp = (T - CMP_LEN) // CMP_STRIDE + 1
    blk_idx = np.arange(n_cmp)[:, None] * CMP_STRIDE + np.arange(CMP_LEN)[None, :]

    def compress(a, pe_i, w1_i, w2_i):
        blocks = a[:, :, blk_idx] + pe_i
        hmid = jax.nn.gelu(blocks.reshape(B, Hkv, n_cmp, CMP_LEN * hd) @ w1_i)
        return hmid @ w2_i

    kc = compress(k_cmp, pe[0], w1[0], w2[0])
    vc = compress(v_cmp, pe[1], w1[1], w2[1])
    cmp_end = np.arange(n_cmp) * CMP_STRIDE + CMP_LEN - 1
    dist_c = pos[:, None] - cmp_end[None, :]
    valid_c = dist_c >= 0
    s = jnp.einsum('bkgtd,bknd->bkgtn', qg, kc).astype(jnp.float32) * scale
    s = s - slopes[None, :, :, None, None] * dist_c.astype(np.float32)
    s = jnp.where(valid_c, s, NEG)
    e = jnp.exp(s - s.max(-1, keepdims=True)) * valid_c
    p_cmp = e / jnp.maximum(e.sum(-1, keepdims=True), 1e-30)
    o_cmp = jnp.einsum('bkgtn,bknd->bkgtd', p_cmp.astype(vc.dtype), vc)

    imp = p_cmp.sum(axis=2)
    n_slc = T // SLC_BLK
    r = SLC_BLK // CMP_STRIDE
    c_over = CMP_LEN // CMP_STRIDE
    imp_p = jnp.pad(imp, ((0, 0), (0, 0), (0, 0), (0, r * n_slc + r + c_over - n_cmp)))
    p_slc = jnp.zeros(imp.shape[:-1] + (n_slc,), jnp.float32)
    for m in range(r):
        for n in range(c_over):
            o = m + n
            p_slc = p_slc + imp_p[..., o:o + r * n_slc:r]
    blk = np.arange(n_slc)
    qblk = pos // SLC_BLK
    forced = (blk[None, :] == 0) | (blk[None, :] == qblk[:, None]) | (blk[None, :] == qblk[:, None] - 1)
    causal = blk[None, :] <= qblk[:, None]
    score = jnp.where(forced, BIG, jnp.where(causal, p_slc, NEG))
    n_top = min(N_SEL, n_slc)
    _, idx = lax.top_k(score, n_top)

    ks_b = k_slc.reshape(B, Hkv, n_slc, SLC_BLK, hd)
    vs_b = v_slc.reshape(B, Hkv, n_slc, SLC_BLK, hd)
    C = SLC_Q_CHUNK
    nc = T // C
    q_ch = qg.reshape(B, Hkv, G, nc, C, hd).transpose(3, 0, 1, 2, 4, 5)
    i_ch = idx.reshape(B, Hkv, nc, C, n_top).transpose(2, 0, 1, 3, 4)
    p_ch = jnp.arange(T, dtype=jnp.int32).reshape(nc, C)
    bi = jnp.arange(B)[:, None, None, None]
    hi = jnp.arange(Hkv)[None, :, None, None]
    offs = jnp.arange(SLC_BLK, dtype=jnp.int32)
    M = n_top * SLC_BLK

    def sel_chunk(args):
        qc, ic, pc = args
        kb = ks_b[bi, hi, ic].reshape(B, Hkv, C, M, hd)
        vb = vs_b[bi, hi, ic].reshape(B, Hkv, C, M, hd)
        kpos = (ic[..., None] * SLC_BLK + offs).reshape(B, Hkv, C, M)
        dist = (pc[None, None, :, None] - kpos)[:, :, None]
        ss = jnp.einsum('bkgcd,bkcmd->bkgcm', qc, kb).astype(jnp.float32) * scale
        ss = ss - slopes[None, :, :, None, None] * dist.astype(jnp.float32)
        ss = jnp.where(dist >= 0, ss, -jnp.inf)
        pp = jax.nn.softmax(ss, axis=-1)
        return jnp.einsum('bkgcm,bkcmd->bkgcd', pp.astype(vb.dtype), vb)

    o_slc = lax.map(sel_chunk, (q_ch, i_ch, p_ch))
    o_slc = o_slc.transpose(1, 2, 3, 0, 4, 5).reshape(B, H, T, hd)

    o_win, _ = banded_attention(q, jnp.repeat(k_win, G, axis=1), jnp.repeat(v_win, G, axis=1),
                                WIN - 1, slopes.reshape(H), 1)

    o_cmp = o_cmp.reshape(B, H, T, hd)
    return gates[..., 0:1] * o_cmp + gates[..., 1:2] * o_slc + gates[..., 2:3] * o_win


def fox_attention(q, k, v, log_f):
    B, H, T, hd = q.shape
    F = jnp.cumsum(log_f, axis=-1)
    nb = T // Q_BLK
    qb = q.reshape(B, H, nb, Q_BLK, hd).transpose(2, 0, 1, 3, 4)
    Fq = F.reshape(B, H, nb, Q_BLK).transpose(2, 0, 1, 3)
    qpos = jnp.arange(T, dtype=jnp.int32).reshape(nb, Q_BLK)
    kpos = jnp.arange(T, dtype=jnp.int32)

    def block(args):
        qi, Fi, pi = args
        s = jnp.einsum('bhqd,bhkd->bhqk', qi, k).astype(jnp.float32) * (hd ** -0.5)
        s = s + Fi[..., None] - F[:, :, None, :]
        s = jnp.where(pi[:, None] >= kpos[None, :], s, -jnp.inf)
        p = jax.nn.softmax(s, axis=-1)
        return jnp.einsum('bhqk,bhkd->bhqd', p.astype(v.dtype), v)

    out = lax.map(block, (qb, Fq, qpos))
    return out.transpose(1, 2, 0, 3, 4).reshape(B, H, T, hd)


def dilated_attention(q, k, v, slopes):
    _, B, Hg, T, hd = q.shape
    outs, lses = [], []
    for g, (window, dil) in enumerate(DIL_PAIRS):
        Ts = T // dil

        def sub(a):
            return a.reshape(B, Hg, Ts, dil, hd).transpose(0, 1, 3, 2, 4).reshape(B, Hg * dil, Ts, hd)

        o, lse = banded_attention(sub(q[g]), sub(k[g]), sub(v[g]), window // dil,
                                  jnp.repeat(slopes[g], dil), dil)
        outs.append(o.reshape(B, Hg, dil, Ts, hd).transpose(0, 1, 3, 2, 4).reshape(B, Hg, T, hd))
        lses.append(lse.reshape(B, Hg, dil, Ts).transpose(0, 1, 3, 2).reshape(B, Hg, T))
    w = jax.nn.softmax(jnp.stack(lses, axis=0), axis=0)
    return jnp.sum(w[..., None].astype(outs[0].dtype) * jnp.stack(outs, axis=0), axis=0)


def hybrid_mixer(u, w_in, b_forget, cmp_pe, cmp_w1, cmp_w2, w_branch, w_gate, b_gate, w_out,
                 nsa_slopes, dil_slopes):
    B, T, _ = u.shape
    hd = HEAD_DIM
    proj = u @ w_in

    q_a = proj[..., OFF_NSA_Q:OFF_NSA_Q + NSA_Q_W].reshape(B, T, NSA_HEADS, hd).transpose(0, 2, 1, 3)
    kv_a = proj[..., OFF_NSA_KV:OFF_NSA_KV + 6 * NSA_KV_W].reshape(B, T, 6, NSA_KV_HEADS, hd).transpose(2, 0, 3, 1, 4)
    g_a = jax.nn.sigmoid(proj[..., OFF_NSA_GATE:OFF_NSA_GATE + 3 * NSA_HEADS].reshape(B, T, NSA_HEADS, 3)).transpose(0, 2, 1, 3)
    o_a = nsa_attention(q_a, kv_a[0], kv_a[1], kv_a[2], kv_a[3], kv_a[4], kv_a[5], g_a,
                        cmp_pe, cmp_w1, cmp_w2, nsa_slopes)
    o_a = o_a.transpose(0, 2, 1, 3).reshape(B, T, OUT_A)

    qkv_b = proj[..., OFF_FOX:OFF_FOX + 3 * FOX_W].reshape(B, T, 3, FOX_HEADS, hd).transpose(2, 0, 3, 1, 4)
    log_f = jax.nn.log_sigmoid((proj[..., OFF_FOX_F:OFF_FOX_F + FOX_HEADS] + b_forget).astype(jnp.float32)).transpose(0, 2, 1)
    o_b = fox_attention(qkv_b[0], qkv_b[1], qkv_b[2], log_f).transpose(0, 2, 1, 3).reshape(B, T, OUT_B)

    qkv_c = proj[..., OFF_DIL:OFF_DIL + 3 * DIL_W].reshape(B, T, 3, DIL_GROUPS, DIL_HEADS_PER_GROUP, hd).transpose(2, 3, 0, 4, 1, 5)
    o_c = dilated_attention(qkv_c[0], qkv_c[1], qkv_c[2], dil_slopes).transpose(0, 2, 1, 3).reshape(B, T, OUT_C)

    y_a = o_a @ w_branch[:OUT_A]
    y_b = o_b @ w_branch[OUT_A:OUT_A + OUT_B]
    y_c = o_c @ w_branch[OUT_A + OUT_B:]
    ga, gb, gc = jnp.split(jax.nn.sigmoid(u @ w_gate + b_gate), 3, axis=-1)
    return (ga * y_a + gb * y_b + gc * y_c) @ w_out


def peer_ffn(u, wq, subkeys, table_u, table_v):
    B, T, D = u.shape
    N = B * T
    K = PEER_TOPK
    q = (u @ wq).reshape(N, PEER_HEADS, 2, PEER_HALF)
    s = jnp.einsum('nhpd,hpkd->nhpk', q, subkeys).astype(jnp.float32)
    s1, i1 = lax.top_k(s[:, :, 0], K)
    s2, i2 = lax.top_k(s[:, :, 1], K)
    cand = (s1[..., :, None] + s2[..., None, :]).reshape(N, PEER_HEADS, K * K)
    cidx = (i1[..., :, None] * PEER_NKEYS + i2[..., None, :]).reshape(N, PEER_HEADS, K * K)
    top_s, top_pos = lax.top_k(cand, K)
    expert = jnp.take_along_axis(cidx, top_pos, axis=-1)
    gate = jax.nn.softmax(top_s, axis=-1)
    C = PEER_TOK_CHUNK
    nc = N // C
    xs = u.reshape(nc, C, D)
    es = expert.reshape(nc, C, PEER_HEADS * K)
    gs = gate.reshape(nc, C, PEER_HEADS * K)

    def chunk(args):
        xc, ec, gc = args
        ue = table_u[ec]
        ve = table_v[ec]
        act = jax.nn.gelu(jnp.einsum('cd,ced->ce', xc, ue).astype(jnp.float32))
        return jnp.einsum('ce,ced->cd', (gc * act).astype(ve.dtype), ve)

    return lax.map(chunk, (xs, es, gs)).reshape(B, T, D)


def setup_inputs(seed: int = 0) -> dict:
    key = jax.random.key(seed)
    ks = jax.random.split(key, 24)
    L, D, hd = DEPTH, D_MODEL, HEAD_DIM

    def nrm(k, shape, std):
        return jax.random.normal(k, shape, jnp.float32) * std

    col = np.ones((IN_COLS,), np.float32)
    for i in (1, 3, 5):
        col[OFF_NSA_KV + i * NSA_KV_W:OFF_NSA_KV + (i + 1) * NSA_KV_W] = BETA
    col[OFF_FOX + 2 * FOX_W:OFF_FOX + 3 * FOX_W] = BETA
    col[OFF_DIL + 2 * DIL_W:OFF_DIL + 3 * DIL_W] = BETA
    row = np.concatenate([np.full((OUT_A,), OUT_A ** -0.5), np.full((OUT_B,), OUT_B ** -0.5),
                          np.full((OUT_C,), OUT_C ** -0.5)]).astype(np.float32) * BETA

    return {
        "x": nrm(ks[0], (BATCH, SEQ, D), 1.0),
        "c": nrm(ks[1], (BATCH, D), 1.0),
        "w_ada": nrm(ks[2], (L, D, 6 * D), 0.1 * D ** -0.5),
        "b_ada": nrm(ks[3], (L, 6 * D), 0.01),
        "w_in": nrm(ks[4], (L, D, IN_COLS), D ** -0.5) * jnp.asarray(col),
        "b_forget": jax.random.uniform(ks[5], (L, FOX_HEADS), jnp.float32, 1.0, 6.0),
        "cmp_pe": nrm(ks[6], (L, 2, CMP_LEN, hd), 0.02),
        "cmp_w1": nrm(ks[7], (L, 2, CMP_LEN * hd, hd), (CMP_LEN * hd) ** -0.5),
        "cmp_w2": nrm(ks[8], (L, 2, hd, hd), hd ** -0.5),
        "w_branch": nrm(ks[9], (L, BRANCH_ROWS, D), 1.0) * jnp.asarray(row)[:, None],
        "w_gate": nrm(ks[10], (L, D, 3 * D), D ** -0.5),
        "b_gate": nrm(ks[11], (L, 3 * D), 0.01),
        "w_out": nrm(ks[12], (L, D, D), BETA * D ** -0.5),
        "ln1_g": 1.0 + nrm(ks[13], (L, D), 0.02),
        "ln1_b": nrm(ks[14], (L, D), 0.02),
        "peer_wq": nrm(ks[15], (L, D, PEER_HEADS * PEER_QDIM), D ** -0.5),
        "peer_subkeys": nrm(ks[16], (L, PEER_HEADS, 2, PEER_NKEYS, PEER_HALF), PEER_HALF ** -0.5),
        "peer_u": nrm(ks[17], (L, PEER_EXPERTS, D), D ** -0.5),
        "peer_v": nrm(ks[18], (L, PEER_EXPERTS, D), BETA * (PEER_HEADS * PEER_TOPK) ** -0.5),
        "ln2_g": 1.0 + nrm(ks[19], (L, D), 0.02),
        "ln2_b": nrm(ks[20], (L, D), 0.02),
    }


def reference(x, c, w_ada, b_ada, w_in, b_forget, cmp_pe, cmp_w1, cmp_w2, w_branch, w_gate, b_gate,
              w_out, ln1_g, ln1_b, peer_wq, peer_subkeys, peer_u, peer_v, ln2_g, ln2_b):
    sl = alibi_slopes(ALIBI_HEADS)
    nsa_slopes = jnp.asarray(sl[0::2].reshape(NSA_KV_HEADS, NSA_GROUP))
    dil_slopes = jnp.asarray(sl[1::2].reshape(DIL_GROUPS, DIL_HEADS_PER_GROUP))
    for l in range(DEPTH):
        mod = c @ w_ada[l] + b_ada[l]
        sh1, sc1, g1, sh2, sc2, g2 = jnp.split(mod[:, None, :], 6, axis=-1)
        u = x * (1.0 + sc1) + sh1
        y = hybrid_mixer(u, w_in[l], b_forget[l], cmp_pe[l], cmp_w1[l], cmp_w2[l], w_branch[l],
                         w_gate[l], b_gate[l], w_out[l], nsa_slopes, dil_slopes)
        x = layer_norm(ALPHA * x + (1.0 + g1) * y, ln1_g[l], ln1_b[l])
        u = x * (1.0 + sc2) + sh2
        y = peer_ffn(u, peer_wq[l], peer_subkeys[l], peer_u[l], peer_v[l])
        x = layer_norm(ALPHA * x + (1.0 + g2) * y, ln2_g[l], ln2_b[l])
    return x
```

```python
import functools

import numpy as np
import jax
import jax.numpy as jnp
from jax import lax
from jax.experimental import pallas as pl
from jax.experimental.pallas import tpu as pltpu

F32 = jnp.float32
BF16 = jnp.bfloat16

D_MODEL = 4096
HEAD_DIM = 128
NSA_HEADS = 12
NSA_KV_HEADS = 3
NSA_GROUP = NSA_HEADS // NSA_KV_HEADS
CMP_LEN = 32
CMP_STRIDE = 16
SLC_BLK = 64
N_SEL = 16
WIN = 512
FOX_HEADS = 8
DIL_PAIRS = ((128, 1), (512, 4), (2048, 16))
DIL_GROUPS = 3
DIL_HPG = 4
DIL_HEADS = DIL_GROUPS * DIL_HPG
ALIBI_HEADS = NSA_HEADS + DIL_HEADS
NSA_Q_W = NSA_HEADS * HEAD_DIM
NSA_KV_W = NSA_KV_HEADS * HEAD_DIM
FOX_W = FOX_HEADS * HEAD_DIM
DIL_W = DIL_HEADS * HEAD_DIM
OFF_NSA_KV = NSA_Q_W
OFF_NSA_GATE = OFF_NSA_KV + 6 * NSA_KV_W
OFF_FOX = OFF_NSA_GATE + 3 * NSA_HEADS
OFF_FOX_F = OFF_FOX + 3 * FOX_W
OFF_DIL = OFF_FOX_F + FOX_HEADS
IN_COLS = OFF_DIL + 3 * DIL_W
NSA_A_W = NSA_Q_W + 6 * NSA_KV_W
OUT_A = NSA_Q_W
OUT_B = FOX_W
OUT_C = DIL_HPG * HEAD_DIM
PEER_HEADS = 8
PEER_NKEYS = 128
PEER_EXPERTS = PEER_NKEYS ** 2
PEER_TOPK = 16
PEER_QDIM = 256
DEPTH = 2
ALPHA = (2.0 * DEPTH) ** 0.25
LN_EPS = 1e-5
NEG = -1e30
BIG = 1e30
SCALE = HEAD_DIM ** -0.5

LANES = 128
GATE_LANE0 = 0
FORGET_LANE0 = 3 * NSA_HEADS
VMEM_LIMIT = 56 * 1024 * 1024


def _cp(*sem):
    return pltpu.CompilerParams(dimension_semantics=sem, vmem_limit_bytes=VMEM_LIMIT)


def _alibi_slopes(n):
    return (np.float32(2.0) ** (-8.0 * np.arange(1, n + 1, dtype=np.float32) / n)).astype(np.float32)


def _gelu(x):
    return 0.5 * x * (1.0 + jnp.tanh(np.float32(np.sqrt(2.0 / np.pi)) * (x + np.float32(0.044715) * (x * x * x))))


def _split3(x):
    hi = x.astype(BF16)
    r1 = x - hi.astype(F32)
    mid = r1.astype(BF16)
    lo = (r1 - mid.astype(F32)).astype(BF16)
    return hi, mid, lo


def _dot3(x, m):
    hi, mid, lo = _split3(x)
    d = functools.partial(jnp.dot, preferred_element_type=F32)
    return (d(lo, m) + d(mid, m)) + d(hi, m)


def _dot_nt(a, b):
    return lax.dot_general(a, b, (((1,), (1,)), ((), ())), preferred_element_type=F32)


def _ada_kernel(c_ref, w_ref, b_ref, o_ref):
    o_ref[...] = jnp.dot(c_ref[...], w_ref[...].astype(BF16), preferred_element_type=F32) + b_ref[...]


def ada_project(c_pad, w_ada, b_ada, tn=512):
    L, D, N6 = w_ada.shape
    return pl.pallas_call(
        _ada_kernel,
        grid=(L, N6 // tn),
        in_specs=[pl.BlockSpec((8, D), lambda l, j: (0, 0)),
                  pl.BlockSpec((None, D, tn), lambda l, j: (l, 0, j)),
                  pl.BlockSpec((None, 1, tn), lambda l, j: (l, 0, j))],
        out_specs=pl.BlockSpec((None, 8, tn), lambda l, j: (l, 0, j)),
        out_shape=jax.ShapeDtypeStruct((L, 8, N6), F32),
        compiler_params=_cp("parallel", "parallel"),
        name="ada_project",
    )(c_pad, w_ada, b_ada)


def _modulate_kernel(x_ref, sc_ref, sh_ref, u_ref):
    u_ref[...] = (x_ref[...] * (1.0 + sc_ref[...]) + sh_ref[...]).astype(u_ref.dtype)


def modulate(x, mod3, sc_blk, sh_blk, T, tr=256):
    N, D = x.shape
    return pl.pallas_call(
        _modulate_kernel,
        grid=(N // tr,),
        in_specs=[pl.BlockSpec((tr, D), lambda i: (i, 0)),
                  pl.BlockSpec((None, 1, D), lambda i: (i * tr // T, 0, sc_blk)),
                  pl.BlockSpec((None, 1, D), lambda i: (i * tr // T, 0, sh_blk))],
        out_specs=pl.BlockSpec((tr, D), lambda i: (i, 0)),
        out_shape=jax.ShapeDtypeStruct((N, D), BF16),
        compiler_params=_cp("parallel"),
        name="modulate",
    )(x, mod3, mod3)


def _mm_kernel(a_ref, w_ref, o_ref):
    o_ref[...] = jnp.dot(a_ref[...], w_ref[...], preferred_element_type=F32).astype(o_ref.dtype)


def matmul(a, w, out_dtype, tm=1024, tn=512):
    M, K = a.shape
    N = w.shape[1]
    tm = min(tm, M)
    tn = min(tn, N)
    assert M % tm == 0 and N % tn == 0
    return pl.pallas_call(
        _mm_kernel,
        grid=(N // tn, M // tm),
        in_specs=[pl.BlockSpec((tm, K), lambda j, i: (i, 0)),
                  pl.BlockSpec((K, tn), lambda j, i: (0, j))],
        out_specs=pl.BlockSpec((tm, tn), lambda j, i: (i, j)),
        out_shape=jax.ShapeDtypeStruct((M, N), out_dtype),
        compiler_params=_cp("parallel", "parallel"),
        name="matmul",
    )(a, w)


def _flash(q, k_ref, v_ref, lo, hi, tk, score_fn):
    tq = q.shape[0]

    def body(c, carry):
        m, l, acc = carry
        off = pl.multiple_of(c * tk, tk)
        k = k_ref[pl.ds(off, tk), :]
        v = v_ref[pl.ds(off, tk), :]
        s = score_fn(_dot_nt(q, k), c)
        m_new = jnp.maximum(m, jnp.max(s, axis=1, keepdims=True))
        alpha = jnp.exp(m - m_new)
        p = jnp.exp(s - m_new)
        l = alpha * l + jnp.sum(p, axis=1, keepdims=True)
        acc = alpha * acc + jnp.dot(p.astype(v.dtype), v, preferred_element_type=F32)
        return m_new, l, acc

    init = (jnp.full((tq, 1), NEG, F32), jnp.zeros((tq, 1), F32), jnp.zeros((tq, HEAD_DIM), F32))
    return lax.fori_loop(lo, hi, body, init)


def _alibi_kernel(slope_ref, q_ref, k_ref, v_ref, *rest, tq, tk, max_dist, kv_group, with_sel, with_lse):
    rest = list(rest)
    sel_ref = rest.pop(0) if with_sel else None
    o_ref = rest.pop(0)
    lse_ref = rest.pop(0) if with_lse else None
    h = pl.program_id(1)
    q0 = pl.program_id(3) * tq
    slope = slope_ref[h]
    rc = lax.broadcasted_iota(jnp.int32, (tq, tk), 0) - lax.broadcasted_iota(jnp.int32, (tq, tk), 1)
    if with_sel:
        sel = sel_ref[...]
        kvh = h // kv_group
        lane_blk = lax.broadcasted_iota(jnp.int32, (LANES, tk), 0)
        key_blk = lax.broadcasted_iota(jnp.int32, (LANES, tk), 1) // SLC_BLK

    def score_fn(s, c):
        dist = rc + (q0 - c * tk)
        s = s * SCALE - slope * dist.astype(F32)
        valid = dist >= 0
        if max_dist is not None:
            valid = valid & (dist <= max_dist)
        if with_sel:
            expand = (lane_blk == key_blk + (kvh * (LANES // 4) + c * (tk // SLC_BLK))).astype(BF16)
            valid = valid & (jnp.dot(sel, expand, preferred_element_type=F32) > 0.5)
        return jnp.where(valid, s, NEG)

    hi = (q0 + tq - 1) // tk + 1
    lo = 0 if max_dist is None else jnp.maximum(q0 - max_dist, 0) // tk
    m, l, acc = _flash(q_ref[...], k_ref, v_ref, lo, hi, tk, score_fn)
    o_ref[...] = (acc / l).astype(o_ref.dtype)
    if with_lse:
        lse_ref[...] = jnp.broadcast_to(m + jnp.log(l), lse_ref.shape)


def alibi_attention(qa, ka, va, slopes, *, B, Ts, n_sub, n_heads, kv_group, q_col, k_col, v_col,
                    o_width, max_dist, tq, tk, sel=None, with_lse=False):
    tq = min(tq, Ts)
    tk = min(tk, Ts)
    nq = Ts // tq
    rows = B * Ts
    in_specs = [pl.BlockSpec((tq, HEAD_DIM), lambda b, h, r, i, s: (b * nq + i, q_col(h, r))),
                pl.BlockSpec((Ts, HEAD_DIM), lambda b, h, r, i, s: (b, k_col(h, r))),
                pl.BlockSpec((Ts, HEAD_DIM), lambda b, h, r, i, s: (b, v_col(h, r)))]
    args = [qa, ka, va]
    if sel is not None:
        in_specs.append(pl.BlockSpec((tq, LANES), lambda b, h, r, i, s: (b * nq + i, 0)))
        args.append(sel)
    hpb = o_width // HEAD_DIM
    o_spec = pl.BlockSpec((tq, HEAD_DIM), lambda b, h, r, i, s: (b * nq + i, r * hpb + h))
    o_shape = jax.ShapeDtypeStruct((rows, n_sub * o_width), F32)
    kern = functools.partial(_alibi_kernel, tq=tq, tk=tk, max_dist=max_dist, kv_group=kv_group,
                             with_sel=sel is not None, with_lse=with_lse)
    return pl.pallas_call(
        kern,
        grid_spec=pltpu.PrefetchScalarGridSpec(
            num_scalar_prefetch=1, grid=(B, n_heads, n_sub, nq),
            in_specs=in_specs,
            out_specs=[o_spec, o_spec] if with_lse else o_spec),
        out_shape=[o_shape, o_shape] if with_lse else o_shape,
        compiler_params=_cp("parallel", "parallel", "parallel", "parallel"),
        name="alibi_attention",
    )(jnp.asarray(slopes, F32), *args)


def _fox_kernel(q_ref, k_ref, v_ref, fq_ref, fk_ref, o_ref, *, tq, tk):
    h = pl.program_id(1)
    q0 = pl.program_id(2) * tq
    lane = lax.broadcasted_iota(jnp.int32, (tq, LANES), 1)
    fq = jnp.sum(jnp.where(lane == FORGET_LANE0 + h, fq_ref[...], 0.0), axis=1, keepdims=True)
    rc = lax.broadcasted_iota(jnp.int32, (tq, tk), 0) - lax.broadcasted_iota(jnp.int32, (tq, tk), 1)

    def score_fn(s, c):
        dist = rc + (q0 - c * tk)
        s = (s * SCALE + fq) - fk_ref[pl.ds(c, 1), :]
        return jnp.where(dist >= 0, s, NEG)

    hi = (q0 + tq - 1) // tk + 1
    _, l, acc = _flash(q_ref[...], k_ref, v_ref, 0, hi, tk, score_fn)
    o_ref[...] = (acc / l).astype(o_ref.dtype)


def fox_attention(proj_fox, f_tok, f_rows, *, B, T, tq=256, tk=256):
    nq = T // tq
    H = FOX_HEADS
    return pl.pallas_call(
        functools.partial(_fox_kernel, tq=tq, tk=tk),
        grid=(B, H, nq),
        in_specs=[pl.BlockSpec((tq, HEAD_DIM), lambda b, h, i: (b * nq + i, h)),
                  pl.BlockSpec((T, HEAD_DIM), lambda b, h, i: (b, H + h)),
                  pl.BlockSpec((T, HEAD_DIM), lambda b, h, i: (b, 2 * H + h)),
                  pl.BlockSpec((tq, LANES), lambda b, h, i: (b * nq + i, 0)),
                  pl.BlockSpec((None, T // tk, tk), lambda b, h, i: (b * H + h, 0, 0))],
        out_specs=pl.BlockSpec((tq, HEAD_DIM), lambda b, h, i: (b * nq + i, h)),
        out_shape=jax.ShapeDtypeStruct((B * T, FOX_W), F32),
        compiler_params=_cp("parallel", "parallel", "parallel"),
        name="fox_attention",
    )(proj_fox, proj_fox, proj_fox, f_tok, f_rows)


def _cumsum_kernel(x_ref, b_ref, o_ref, *, T, tr):
    i = pl.program_id(1)
    z = x_ref[...] + b_ref[...]
    logf = jnp.minimum(z, 0.0) - jnp.log1p(jnp.exp(-jnp.abs(z)))
    row = lax.broadcasted_iota(jnp.int32, (tr, T), 0) + i * tr
    col = lax.broadcasted_iota(jnp.int32, (tr, T), 1)
    o_ref[...] = _dot3_lhs((row >= col).astype(BF16), logf)


def _dot3_lhs(m, x):
    hi, mid, lo = _split3(x)
    d = functools.partial(jnp.dot, preferred_element_type=F32)
    return (d(m, lo) + d(m, mid)) + d(m, hi)


def forget_cumsum(small, bias_row, *, B, T, tr=256):
    return pl.pallas_call(
        functools.partial(_cumsum_kernel, T=T, tr=tr),
        grid=(B, T // tr),
        in_specs=[pl.BlockSpec((T, LANES), lambda b, i: (b, 0)),
                  pl.BlockSpec((1, LANES), lambda b, i: (0, 0))],
        out_specs=pl.BlockSpec((tr, LANES), lambda b, i: (b * (T // tr) + i, 0)),
        out_shape=jax.ShapeDtypeStruct((B * T, LANES), F32),
        compiler_params=_cp("parallel", "parallel"),
        name="forget_cumsum",
    )(small, bias_row)


def _compress_kernel(x_ref, pe_ref, w1_ref, w2_ref, o_ref):
    x = x_ref[...].astype(F32)
    y0 = jnp.dot((x + pe_ref[0]).astype(BF16), w1_ref[0], preferred_element_type=F32)
    y1 = jnp.dot((x + pe_ref[1]).astype(BF16), w1_ref[1], preferred_element_type=F32)
    n = y1.shape[0]
    hmid = _gelu(y0 + pltpu.roll(y1, n - 1, 0))
    o_ref[...] = jnp.dot(hmid.astype(BF16), w2_ref[...], preferred_element_type=F32).astype(o_ref.dtype)


def compress_blocks(chunks, pe, w1, w2):
    _, G, nch, cw = chunks.shape
    return pl.pallas_call(
        _compress_kernel,
        grid=(2, G),
        in_specs=[pl.BlockSpec((None, None, nch, cw), lambda a, g: (a, g, 0, 0)),
                  pl.BlockSpec((None, 2, 1, cw), lambda a, g: (a, 0, 0, 0)),
                  pl.BlockSpec((None, 2, cw, HEAD_DIM), lambda a, g: (a, 0, 0, 0)),
                  pl.BlockSpec((None, HEAD_DIM, HEAD_DIM), lambda a, g: (a, 0, 0))],
        out_specs=pl.BlockSpec((None, None, nch, HEAD_DIM), lambda a, g: (a, g, 0, 0)),
        out_shape=jax.ShapeDtypeStruct((2, G, nch, HEAD_DIM), BF16),
        compiler_params=_cp("parallel", "parallel"),
        name="compress_blocks",
    )(chunks, pe, w1, w2)


def _cmp_kernel(slope_ref, q_ref, kc_ref, vc_ref, smat_ref, o_ref, sel_ref, *, tq, n_cmp, n_slc):
    q0 = pl.program_id(1) * tq
    nk = kc_ref.shape[1]
    pos = lax.broadcasted_iota(jnp.int32, (tq, nk), 0) + q0
    n_idx = lax.broadcasted_iota(jnp.int32, (tq, nk), 1)
    dist_c = pos - (n_idx * CMP_STRIDE + (CMP_LEN - 1))
    valid_c = (dist_c >= 0) & (n_idx < n_cmp)
    dist_f = dist_c.astype(F32)
    lane = lax.broadcasted_iota(jnp.int32, (tq, LANES), 1)
    qblk = (lax.broadcasted_iota(jnp.int32, (tq, LANES), 0) + q0) // SLC_BLK
    sel_all = jnp.zeros((tq, LANES), F32)
    for kv in range(NSA_KV_HEADS):
        kc = kc_ref[kv]
        vc = vc_ref[kv]
        imp = jnp.zeros((tq, nk), F32)
        for g in range(NSA_GROUP):
            hh = kv * NSA_GROUP + g
            q = q_ref[:, hh * HEAD_DIM:(hh + 1) * HEAD_DIM]
            s = _dot_nt(q, kc) * SCALE - slope_ref[hh] * dist_f
            s = jnp.where(valid_c, s, NEG)
            e = jnp.where(valid_c, jnp.exp(s - jnp.max(s, axis=1, keepdims=True)), 0.0)
            p = e / jnp.maximum(jnp.sum(e, axis=1, keepdims=True), 1e-30)
            o_ref[:, hh * HEAD_DIM:(hh + 1) * HEAD_DIM] = jnp.dot(
                p.astype(BF16), vc, preferred_element_type=F32)
            imp = imp + p
        p_slc = _dot3(imp, smat_ref[kv])
        j = lane - kv * (LANES // 4)
        mine = (j >= 0) & (j < n_slc)
        forced = (j == 0) | (j == qblk) | (j == qblk - 1)
        causal = j <= qblk
        score = jnp.where(forced, BIG, jnp.where(causal, p_slc, NEG))
        rank = jnp.zeros((tq, LANES), jnp.int32)
        for k in range(n_slc):
            col = score[:, kv * (LANES // 4) + k:kv * (LANES // 4) + k + 1]
            ahead = (col > score) | ((col == score) & (k < j))
            rank = rank + ahead.astype(jnp.int32)
        sel_all = sel_all + jnp.where(mine & causal & (rank < min(N_SEL, n_slc)), 1.0, 0.0)
    sel_ref[...] = sel_all.astype(sel_ref.dtype)


def cmp_attention(proj_a, kvc, smat, slopes, *, B, T, tq=256):
    nq = T // tq
    nch = kvc.shape[2]
    n_cmp = (T - CMP_LEN) // CMP_STRIDE + 1
    n_slc = T // SLC_BLK
    kvc4 = kvc.reshape(2, B, NSA_KV_HEADS, nch, HEAD_DIM)
    return pl.pallas_call(
        functools.partial(_cmp_kernel, tq=tq, n_cmp=n_cmp, n_slc=n_slc),
        grid_spec=pltpu.PrefetchScalarGridSpec(
            num_scalar_prefetch=1, grid=(B, nq),
            in_specs=[pl.BlockSpec((tq, NSA_Q_W), lambda b, i, s: (b * nq + i, 0)),
                      pl.BlockSpec((None, None, NSA_KV_HEADS, nch, HEAD_DIM), lambda b, i, s: (0, b, 0, 0, 0)),
                      pl.BlockSpec((None, None, NSA_KV_HEADS, nch, HEAD_DIM), lambda b, i, s: (1, b, 0, 0, 0)),
                      pl.BlockSpec((NSA_KV_HEADS, nch, LANES), lambda b, i, s: (0, 0, 0))],
            out_specs=[pl.BlockSpec((tq, NSA_Q_W), lambda b, i, s: (b * nq + i, 0)),
                       pl.BlockSpec((tq, LANES), lambda b, i, s: (b * nq + i, 0))]),
        out_shape=[jax.ShapeDtypeStruct((B * T, NSA_Q_W), F32),
                   jax.ShapeDtypeStruct((B * T, LANES), BF16)],
        compiler_params=_cp("parallel", "parallel"),
        name="cmp_attention",
    )(jnp.asarray(slopes, F32), proj_a, kvc4, kvc4, smat)


def _pool_matrix(nch, n_slc):
    r = SLC_BLK // CMP_STRIDE
    c_over = CMP_LEN // CMP_STRIDE
    n_cmp_max = nch
    m = np.zeros((NSA_KV_HEADS, nch, LANES), np.float32)
    for kv in range(NSA_KV_HEADS):
        for j in range(n_slc):
            for a in range(r):
                for b in range(c_over):
                    i = r * j + a + b
                    if i < n_cmp_max:
                        m[kv, i, kv * (LANES // 4) + j] += 1.0
    return m


def _nsa_merge_kernel(oc_ref, os_ref, ow_ref, g_ref, e_ref, o_ref):
    sig = jax.nn.sigmoid(g_ref[...])
    out = _dot3(sig, e_ref[0]) * oc_ref[...]
    out = out + _dot3(sig, e_ref[1]) * os_ref[...]
    out = out + _dot3(sig, e_ref[2]) * ow_ref[...]
    o_ref[...] = out.astype(o_ref.dtype)


def nsa_merge(o_cmp, o_slc, o_win, small, tr=256):
    N = o_cmp.shape[0]
    e = np.zeros((3, LANES, NSA_Q_W), np.float32)
    for br in range(3):
        for h in range(NSA_HEADS):
            e[br, GATE_LANE0 + h * 3 + br, h * HEAD_DIM:(h + 1) * HEAD_DIM] = 1.0
    row = pl.BlockSpec((tr, NSA_Q_W), lambda i: (i, 0))
    return pl.pallas_call(
        _nsa_merge_kernel,
        grid=(N // tr,),
        in_specs=[row, row, row,
                  pl.BlockSpec((tr, LANES), lambda i: (i, 0)),
                  pl.BlockSpec((3, LANES, NSA_Q_W), lambda i: (0, 0, 0))],
        out_specs=row,
        out_shape=jax.ShapeDtypeStruct((N, NSA_Q_W), BF16),
        compiler_params=_cp("parallel"),
        name="nsa_merge",
    )(o_cmp, o_slc, o_win, small, jnp.asarray(e, BF16))


def _dil_merge_kernel(o0, o1, o2, l0, l1, l2, o_ref):
    a, b, c = l0[...], l1[...], l2[...]
    m = jnp.maximum(jnp.maximum(a, b), c)
    ea, eb, ec = jnp.exp(a - m), jnp.exp(b - m), jnp.exp(c - m)
    z = ea + eb + ec
    o_ref[...] = ((ea / z) * o0[...] + (eb / z) * o1[...] + (ec / z) * o2[...]).astype(o_ref.dtype)


def dil_merge(outs, lses, tr=512):
    N, W = outs[0].shape
    row = pl.BlockSpec((tr, W), lambda i: (i, 0))
    return pl.pallas_call(
        _dil_merge_kernel,
        grid=(N // tr,),
        in_specs=[row] * 6,
        out_specs=row,
        out_shape=jax.ShapeDtypeStruct((N, W), BF16),
        compiler_params=_cp("parallel"),
        name="dil_merge",
    )(*outs, *lses)


def _branch_kernel(u_ref, wga, wgb, wgc, bga, bgb, bgc, oa, ob, oc, wa, wb, wc, z_ref):
    d = functools.partial(jnp.dot, preferred_element_type=F32)
    u = u_ref[...]
    z = jax.nn.sigmoid(d(u, wga[...]) + bga[...]) * d(oa[...], wa[...])
    z = z + jax.nn.sigmoid(d(u, wgb[...]) + bgb[...]) * d(ob[...], wb[...])
    z = z + jax.nn.sigmoid(d(u, wgc[...]) + bgc[...]) * d(oc[...], wc[...])
    z_ref[...] = z.astype(z_ref.dtype)


def gated_branches(u, w_gate, b_gate, o_a, o_b, o_c, wb_a, wb_b, wb_c, tm=512, tn=512):
    N, D = u.shape
    nj = D // tn

    def wg(k):
        return pl.BlockSpec((D, tn), lambda j, i: (0, k * nj + j))

    def bg(k):
        return pl.BlockSpec((1, tn), lambda j, i: (0, k * nj + j))

    def act(w):
        return pl.BlockSpec((tm, w), lambda j, i: (i, 0))

    def wup(w):
        return pl.BlockSpec((w, tn), lambda j, i: (0, j))

    return pl.pallas_call(
        _branch_kernel,
        grid=(nj, N // tm),
        in_specs=[act(D), wg(0), wg(1), wg(2), bg(0), bg(1), bg(2),
                  act(OUT_A), act(OUT_B), act(OUT_C), wup(OUT_A), wup(OUT_B), wup(OUT_C)],
        out_specs=pl.BlockSpec((tm, tn), lambda j, i: (i, j)),
        out_shape=jax.ShapeDtypeStruct((N, D), BF16),
        compiler_params=_cp("parallel", "parallel"),
        name="gated_branches",
    )(u, w_gate, w_gate, w_gate, b_gate, b_gate, b_gate, o_a, o_b, o_c, wb_a, wb_b, wb_c)


def _postnorm_kernel(x_ref, y_ref, g_ref, gam_ref, bet_ref, sc_ref, sh_ref, xo_ref, u_ref):
    h = ALPHA * x_ref[...] + (1.0 + g_ref[...]) * y_ref[...]
    mu = jnp.mean(h, axis=-1, keepdims=True)
    hc = h - mu
    var = jnp.mean(hc * hc, axis=-1, keepdims=True)
    xn = hc * lax.rsqrt(var + LN_EPS) * gam_ref[...] + bet_ref[...]
    xo_ref[...] = xn
    u_ref[...] = (xn * (1.0 + sc_ref[...]) + sh_ref[...]).astype(u_ref.dtype)


def postnorm(x, y, mod_g, g_blk, gamma, beta, mod_n, sc_blk, sh_blk, T, tr=256):
    N, D = x.shape
    row = pl.BlockSpec((tr, D), lambda i: (i, 0))
    vec = pl.BlockSpec((1, D), lambda i: (0, 0))

    def mod(blk):
        return pl.BlockSpec((None, 1, D), lambda i: (i * tr // T, 0, blk))

    return pl.pallas_call(
        _postnorm_kernel,
        grid=(N // tr,),
        in_specs=[row, row, mod(g_blk), vec, vec, mod(sc_blk), mod(sh_blk)],
        out_specs=[row, row],
        out_shape=[jax.ShapeDtypeStruct((N, D), F32), jax.ShapeDtypeStruct((N, D), BF16)],
        compiler_params=_cp("parallel"),
        name="postnorm",
    )(x, y, mod_g, gamma, beta, mod_n, mod_n)


def _topk_rows(s, k):
    rows, tn = s.shape
    iota = lax.broadcasted_iota(jnp.int32, (rows, tn), 0)
    rank = jnp.full((rows, tn), k, jnp.int32)
    cur = s
    tops = []
    for r in range(k):
        m = jnp.max(cur, axis=0, keepdims=True)
        idx = jnp.min(jnp.where(cur == m, iota, rows), axis=0, keepdims=True)
        hit = iota == idx
        rank = jnp.where(hit, r, rank)
        cur = jnp.where(hit, -jnp.inf, cur)
        tops.append(m)
    return rank, tops


def _stack_rows(rows_list):
    n = len(rows_list)
    tn = rows_list[0].shape[1]
    iota = lax.broadcasted_iota(jnp.int32, (n, tn), 0)
    out = jnp.zeros((n, tn), rows_list[0].dtype)
    for r, row in enumerate(rows_list):
        out = jnp.where(iota == r, row, out)
    return out


def _peer_select_kernel(q_ref, sk_ref, e1_ref, l1_ref, e2_ref, r2_ref, *, cpt):
    K = PEER_TOPK
    half = PEER_QDIM // 2
    s1 = _dot_nt(sk_ref[0], q_ref[:, :half])
    s2 = _dot_nt(sk_ref[1], q_ref[:, half:])
    rank1, top1 = _topk_rows(s1, K)
    rank2, top2 = _topk_rows(s2, K)
    t2 = _stack_rows(top2)
    cand = jnp.concatenate([top1[ra] + t2 for ra in range(K // 2)]
                           + [_stack_rows(top1[K // 2:]) + top2[0]], axis=0)
    crank, _ = _topk_rows(cand, K)
    chosen = crank < K
    z = jnp.sum(jnp.where(chosen, jnp.exp(cand - (top1[0] + top2[0])), 0.0), axis=0, keepdims=True)
    cnt = chosen.astype(F32)
    row_len = [jnp.sum(cnt[ra * K:(ra + 1) * K], axis=0, keepdims=True) for ra in range(K // 2)]
    row_len += [cnt[K // 2 * K + i:K // 2 * K + i + 1] for i in range(K - K // 2)]
    l1 = jnp.zeros(s1.shape, F32)
    for ra in range(K):
        l1 = jnp.where(rank1 == ra, row_len[ra], l1)
    e1 = jnp.where(rank1 < K, jnp.exp(s1 - top1[0]) / z, 0.0)
    e2 = jnp.where(rank2 < K, jnp.exp(s2 - top2[0]), 0.0)
    for j in range(PEER_NKEYS // cpt):
        e1_ref[j] = e1[j * cpt:(j + 1) * cpt]
        l1_ref[j] = l1[j * cpt:(j + 1) * cpt]
    e2_ref[...] = e2
    r2_ref[...] = rank2.astype(F32)


def peer_select(q, subkeys, *, cpt, tn=256):
    N = q.shape[0]
    H = PEER_HEADS
    nt = PEER_NKEYS // cpt
    s1 = pl.BlockSpec((nt, None, cpt, tn), lambda i, h: (0, h, 0, i))
    s2 = pl.BlockSpec((None, PEER_NKEYS, tn), lambda i, h: (h, 0, i))
    sh1 = jax.ShapeDtypeStruct((nt, H, cpt, N), F32)
    sh2 = jax.ShapeDtypeStruct((H, PEER_NKEYS, N), F32)
    return pl.pallas_call(
        functools.partial(_peer_select_kernel, cpt=cpt),
        grid=(N // tn, H),
        in_specs=[pl.BlockSpec((tn, PEER_QDIM), lambda i, h: (i, h)),
                  pl.BlockSpec((None, 2, PEER_NKEYS, PEER_QDIM // 2), lambda i, h: (h, 0, 0, 0))],
        out_specs=[s1, s1, s2, s2],
        out_shape=[sh1, sh1, sh2, sh2],
        compiler_params=_cp("parallel", "parallel"),
        name="peer_select",
    )(q, subkeys)


def _peer_dense_kernel(u_ref, tu_ref, tv_ref, e1_ref, l1_ref, e2_ref, r2_ref, o_ref, *, cpt):
    j = pl.program_id(1)
    ht = _dot_nt(tu_ref[...], u_ref[...])
    parts = []
    for cc in range(cpt):
        w = None
        for h in range(PEER_HEADS):
            hit = r2_ref[h] < l1_ref[h, cc:cc + 1, :]
            wh = jnp.where(hit, e2_ref[h], 0.0) * e1_ref[h, cc:cc + 1, :]
            w = wh if w is None else w + wh
        act = _gelu(ht[cc * PEER_NKEYS:(cc + 1) * PEER_NKEYS])
        parts.append((w * act).astype(BF16))
    a_t = jnp.concatenate(parts, axis=0)
    contrib = lax.dot_general(a_t, tv_ref[...], (((0,), (0,)), ((), ())), preferred_element_type=F32)

    @pl.when(j == 0)
    def _():
        o_ref[...] = contrib

    @pl.when(j != 0)
    def _():
        o_ref[...] += contrib


def peer_dense(u, tu, tv, e1, l1, e2, r2, *, cpt, tm=512):
    N, D = u.shape
    E = tu.shape[0]
    te = cpt * PEER_NKEYS
    H = PEER_HEADS
    once = pl.Buffered(1)
    s1 = pl.BlockSpec((None, H, cpt, tm), lambda i, j: (j, 0, 0, i))
    s2 = pl.BlockSpec((H, PEER_NKEYS, tm), lambda i, j: (0, 0, i), pipeline_mode=once)
    tab = pl.BlockSpec((te, D), lambda i, j: (j, 0))
    return pl.pallas_call(
        functools.partial(_peer_dense_kernel, cpt=cpt),
        grid=(N // tm, E // te),
        in_specs=[pl.BlockSpec((tm, D), lambda i, j: (i, 0), pipeline_mode=once), tab, tab, s1, s1, s2, s2],
        out_specs=pl.BlockSpec((tm, D), lambda i, j: (i, 0)),
        out_shape=jax.ShapeDtypeStruct((N, D), F32),
        compiler_params=_cp("parallel", "arbitrary"),
        name="peer_dense",
    )(u, tu, tv, e1, l1, e2, r2)


def _hybrid_mixer(u, p, *, B, T):
    N = B * T
    sl = _alibi_slopes(ALIBI_HEADS)
    nsa_slopes = sl[0::2]
    dil_slopes = sl[1::2].reshape(DIL_GROUPS, DIL_HPG)

    proj_a = matmul(u, p["w_a"], BF16, tn=768)
    small = matmul(u, p["w_small"], F32)
    proj_fox = matmul(u, p["w_fox"], BF16)
    proj_dil = matmul(u, p["w_dil"], BF16)

    nch = T // CMP_STRIDE
    kv_cmp = proj_a[:, OFF_NSA_KV:OFF_NSA_KV + 2 * NSA_KV_W].reshape(B, T, 2, NSA_KV_HEADS, HEAD_DIM)
    chunks = kv_cmp.transpose(2, 0, 3, 1, 4).reshape(2, B * NSA_KV_HEADS, nch, CMP_STRIDE * HEAD_DIM)
    kvc = compress_blocks(chunks, p["cmp_pe"], p["cmp_w1"], p["cmp_w2"])
    smat = jnp.asarray(_pool_matrix(nch, T // SLC_BLK), BF16)
    o_cmp, sel = cmp_attention(proj_a, kvc, smat, nsa_slopes, B=B, T=T)
    qh = NSA_HEADS
    kvh = NSA_KV_HEADS
    nsa_kw = dict(B=B, Ts=T, n_sub=1, n_heads=qh, kv_group=NSA_GROUP, o_width=NSA_Q_W,
                  q_col=lambda h, r: h, tq=256, tk=256)
    o_slc = alibi_attention(proj_a, proj_a, proj_a, nsa_slopes, max_dist=None, sel=sel,
                            k_col=lambda h, r: qh + 2 * kvh + h // NSA_GROUP,
                            v_col=lambda h, r: qh + 3 * kvh + h // NSA_GROUP, **nsa_kw)
    o_win = alibi_attention(proj_a, proj_a, proj_a, nsa_slopes, max_dist=WIN - 1,
                            k_col=lambda h, r: qh + 4 * kvh + h // NSA_GROUP,
                            v_col=lambda h, r: qh + 5 * kvh + h // NSA_GROUP, **nsa_kw)
    o_a = nsa_merge(o_cmp, o_slc, o_win, small)

    f_tok = forget_cumsum(small, p["b_forget_row"], B=B, T=T)
    tkf = min(256, T)
    f_rows = f_tok[:, FORGET_LANE0:FORGET_LANE0 + FOX_HEADS].reshape(B, T, FOX_HEADS)
    f_rows = f_rows.transpose(0, 2, 1).reshape(B * FOX_HEADS, T // tkf, tkf)
    o_b = fox_attention(proj_fox, f_tok, f_rows, B=B, T=T, tq=min(256, T), tk=tkf).astype(BF16)

    outs, lses = [], []
    ncb = 3 * DIL_W // HEAD_DIM
    for g, (window, dil) in enumerate(DIL_PAIRS):
        Ts = T // dil
        view = proj_dil.reshape(B * Ts, dil * 3 * DIL_W)
        o, lse = alibi_attention(
            view, view, view, dil_slopes[g] * np.float32(dil), B=B, Ts=Ts, n_sub=dil, n_heads=DIL_HPG,
            kv_group=1, o_width=OUT_C, max_dist=window // dil, tq=128, tk=128, with_lse=True,
            q_col=lambda h, r, g=g: r * ncb + g * DIL_HPG + h,
            k_col=lambda h, r, g=g: r * ncb + DIL_HEADS + g * DIL_HPG + h,
            v_col=lambda h, r, g=g: r * ncb + 2 * DIL_HEADS + g * DIL_HPG + h)
        outs.append(o.reshape(N, OUT_C))
        lses.append(lse.reshape(N, OUT_C))
    o_c = dil_merge(outs, lses)

    z = gated_branches(u, p["w_gate"], p["b_gate"], o_a, o_b, o_c, p["wb_a"], p["wb_b"], p["wb_c"])
    return matmul(z, p["w_out"], F32)


def _peer_ffn(u, p):
    cpt = 4
    q = matmul(u, p["peer_wq"], BF16)
    e1, l1, e2, r2 = peer_select(q, p["peer_subkeys"], cpt=cpt)
    return peer_dense(u, p["peer_u"], p["peer_v"], e1, l1, e2, r2, cpt=cpt)


def _layer_params(l, w_in, b_forget, cmp_pe, cmp_w1, cmp_w2, w_branch, w_gate, b_gate, w_out,
                  peer_wq, peer_subkeys, peer_u, peer_v):
    wi = w_in[l]
    D = wi.shape[0]
    w_small = jnp.concatenate(
        [wi[:, OFF_NSA_GATE:OFF_NSA_GATE + 3 * NSA_HEADS], wi[:, OFF_FOX_F:OFF_FOX_F + FOX_HEADS],
         jnp.zeros((D, LANES - 3 * NSA_HEADS - FOX_HEADS), wi.dtype)], axis=1)
    b_row = jnp.zeros((1, LANES), F32).at[0, FORGET_LANE0:FORGET_LANE0 + FOX_HEADS].set(b_forget[l])
    half = CMP_LEN // 2
    return dict(
        w_a=wi[:, :NSA_A_W].astype(BF16),
        w_small=w_small.astype(BF16),
        w_fox=wi[:, OFF_FOX:OFF_FOX + 3 * FOX_W].astype(BF16),
        w_dil=wi[:, OFF_DIL:].astype(BF16),
        b_forget_row=b_row,
        cmp_pe=cmp_pe[l].reshape(2, 2, 1, half * HEAD_DIM),
        cmp_w1=cmp_w1[l].astype(BF16).reshape(2, 2, half * HEAD_DIM, HEAD_DIM),
        cmp_w2=cmp_w2[l].astype(BF16),
        w_gate=w_gate[l].astype(BF16),
        b_gate=b_gate[l].reshape(1, -1),
        wb_a=w_branch[l, :OUT_A].astype(BF16),
        wb_b=w_branch[l, OUT_A:OUT_A + OUT_B].astype(BF16),
        wb_c=w_branch[l, OUT_A + OUT_B:].astype(BF16),
        w_out=w_out[l].astype(BF16),
        peer_wq=peer_wq[l].astype(BF16),
        peer_subkeys=peer_subkeys[l].astype(BF16),
        peer_u=peer_u[l].astype(BF16),
        peer_v=peer_v[l].astype(BF16),
    )


def kernel(x, c, w_ada, b_ada, w_in, b_forget, cmp_pe, cmp_w1, cmp_w2, w_branch, w_gate, b_gate, w_out,
           ln1_g, ln1_b, peer_wq, peer_subkeys, peer_u, peer_v, ln2_g, ln2_b):
    B, T, D = x.shape
    L = w_ada.shape[0]
    N = B * T
    c_pad = jnp.zeros((8, D), BF16).at[:B].set(c.astype(BF16))
    mod = ada_project(c_pad, w_ada, b_ada.reshape(L, 1, 6 * D))
    mods = [mod[l].reshape(8, 1, 6 * D) for l in range(L)]
    xf = x.reshape(N, D)
    u = modulate(xf, mods[0], 1, 0, T)
    for l in range(L):
        p = _layer_params(l, w_in, b_forget, cmp_pe, cmp_w1, cmp_w2, w_branch, w_gate, b_gate, w_out,
                          peer_wq, peer_subkeys, peer_u, peer_v)
        y = _hybrid_mixer(u, p, B=B, T=T)
        xf, u = postnorm(xf, y, mods[l], 2, ln1_g[l].reshape(1, D), ln1_b[l].reshape(1, D),
                         mods[l], 4, 3, T)
        y = _peer_ffn(u, p)
        nxt = mods[min(l + 1, L - 1)]
        xf, u = postnorm(xf, y, mods[l], 5, ln2_g[l].reshape(1, D), ln2_b[l].reshape(1, D),
                         nxt, 1, 0, T)
    return xf.reshape(B, T, D)
```

```python
import functools

import numpy as np
import jax
import jax.numpy as jnp
from jax import lax
from jax.experimental import pallas as pl
from jax.experimental.pallas import tpu as pltpu

F32 = jnp.float32
BF16 = jnp.bfloat16

D_MODEL = 4096
HEAD_DIM = 128
NSA_HEADS = 12
NSA_KV_HEADS = 3
NSA_GROUP = NSA_HEADS // NSA_KV_HEADS
CMP_LEN = 32
CMP_STRIDE = 16
SLC_BLK = 64
N_SEL = 16
WIN = 512
FOX_HEADS = 8
DIL_PAIRS = ((128, 1), (512, 4), (2048, 16))
DIL_GROUPS = 3
DIL_HPG = 4
DIL_HEADS = DIL_GROUPS * DIL_HPG
ALIBI_HEADS = NSA_HEADS + DIL_HEADS
NSA_Q_W = NSA_HEADS * HEAD_DIM
NSA_KV_W = NSA_KV_HEADS * HEAD_DIM
FOX_W = FOX_HEADS * HEAD_DIM
DIL_W = DIL_HEADS * HEAD_DIM
OFF_NSA_KV = NSA_Q_W
OFF_NSA_GATE = OFF_NSA_KV + 6 * NSA_KV_W
OFF_FOX = OFF_NSA_GATE + 3 * NSA_HEADS
OFF_FOX_F = OFF_FOX + 3 * FOX_W
OFF_DIL = OFF_FOX_F + FOX_HEADS
IN_COLS = OFF_DIL + 3 * DIL_W
NSA_A_W = NSA_Q_W + 6 * NSA_KV_W
OUT_A = NSA_Q_W
OUT_B = FOX_W
OUT_C = DIL_HPG * HEAD_DIM
PEER_HEADS = 8
PEER_NKEYS = 128
PEER_EXPERTS = PEER_NKEYS ** 2
PEER_TOPK = 16
PEER_QDIM = 256
DEPTH = 2
ALPHA = (2.0 * DEPTH) ** 0.25
LN_EPS = 1e-5
NEG = -1e30
BIG = 1e30
SCALE = HEAD_DIM ** -0.5

LANES = 128
GATE_LANE0 = 0
FORGET_LANE0 = 3 * NSA_HEADS
VMEM_LIMIT = 56 * 1024 * 1024


def _cp(*sem):
    return pltpu.CompilerParams(dimension_semantics=sem, vmem_limit_bytes=VMEM_LIMIT)


def _alibi_slopes(n):
    return (np.float32(2.0) ** (-8.0 * np.arange(1, n + 1, dtype=np.float32) / n)).astype(np.float32)


def _gelu(x):
    return 0.5 * x * (1.0 + jnp.tanh(np.float32(np.sqrt(2.0 / np.pi)) * (x + np.float32(0.044715) * (x * x * x))))


def _split3(x):
    hi = x.astype(BF16)
    r1 = x - hi.astype(F32)
    mid = r1.astype(BF16)
    lo = (r1 - mid.astype(F32)).astype(BF16)
    return hi, mid, lo


def _dot3(x, m):
    hi, mid, lo = _split3(x)
    d = functools.partial(jnp.dot, preferred_element_type=F32)
    return (d(lo, m) + d(mid, m)) + d(hi, m)


def _dot_nt(a, b):
    return lax.dot_general(a, b, (((1,), (1,)), ((), ())), preferred_element_type=F32)


def _ada_kernel(c_ref, w_ref, b_ref, o_ref):
    o_ref[...] = jnp.dot(c_ref[...], w_ref[...].astype(BF16), preferred_element_type=F32) + b_ref[...]


def ada_project(c_pad, w_ada, b_ada, tn=512):
    L, D, N6 = w_ada.shape
    return pl.pallas_call(
        _ada_kernel,
        grid=(L, N6 // tn),
        in_specs=[pl.BlockSpec((8, D), lambda l, j: (0, 0)),
                  pl.BlockSpec((None, D, tn), lambda l, j: (l, 0, j)),
                  pl.BlockSpec((None, 1, tn), lambda l, j: (l, 0, j))],
        out_specs=pl.BlockSpec((None, 8, tn), lambda l, j: (l, 0, j)),
        out_shape=jax.ShapeDtypeStruct((L, 8, N6), F32),
        compiler_params=_cp("parallel", "parallel"),
        name="ada_project",
    )(c_pad, w_ada, b_ada)


def _modulate_kernel(x_ref, sc_ref, sh_ref, u_ref):
    u_ref[...] = (x_ref[...] * (1.0 + sc_ref[...]) + sh_ref[...]).astype(u_ref.dtype)


def modulate(x, mod3, sc_blk, sh_blk, T, tr=256):
    N, D = x.shape
    return pl.pallas_call(
        _modulate_kernel,
        grid=(N // tr,),
        in_specs=[pl.BlockSpec((tr, D), lambda i: (i, 0)),
                  pl.BlockSpec((None, 1, D), lambda i: (i * tr // T, 0, sc_blk)),
                  pl.BlockSpec((None, 1, D), lambda i: (i * tr // T, 0, sh_blk))],
        out_specs=pl.BlockSpec((tr, D), lambda i: (i, 0)),
        out_shape=jax.ShapeDtypeStruct((N, D), BF16),
        compiler_params=_cp("parallel"),
        name="modulate",
    )(x, mod3, mod3)


def _mm_kernel(a_ref, w_ref, o_ref, wb_ref):
    @pl.when(pl.program_id(1) == 0)
    def _():
        wb_ref[...] = w_ref[...].astype(BF16)

    o_ref[...] = jnp.dot(a_ref[...], wb_ref[...], preferred_element_type=F32).astype(o_ref.dtype)


def matmul(a, w, out_dtype, layer=0, n_cols=None, tm=1024, tn=512):
    M, K = a.shape
    if w.ndim == 2:
        w = w[None]
    N = w.shape[2] if n_cols is None else n_cols
    tm = min(tm, M)
    tn = min(tn, N)
    assert M % tm == 0 and N % tn == 0
    return pl.pallas_call(
        _mm_kernel,
        grid=(N // tn, M // tm),
        in_specs=[pl.BlockSpec((tm, K), lambda j, i: (i, 0)),
                  pl.BlockSpec((None, K, tn), lambda j, i: (layer, 0, j))],
        out_specs=pl.BlockSpec((tm, tn), lambda j, i: (i, j)),
        out_shape=jax.ShapeDtypeStruct((M, N), out_dtype),
        scratch_shapes=[pltpu.VMEM((K, tn), BF16)],
        compiler_params=_cp("parallel", "arbitrary"),
        name="matmul",
    )(a, w)


HEADS_PER_STEP = 4


def _head_cols(i):
    return slice(i * HEAD_DIM, (i + 1) * HEAD_DIM)


def _flash(qs, load_kv, lo, hi, chunk_fn, score_fns):
    tq = qs[0].shape[0]

    def body(c, carry):
        shared = chunk_fn(c)
        kvs = [load_kv(c, i) for i in range(len(qs))]
        raws = [_dot_nt(q, kvs[i][0]) for i, q in enumerate(qs)]
        mids = []
        for i, raw in enumerate(raws):
            m, l, _ = carry[i]
            s = score_fns[i](raw, c, shared)
            m_new = jnp.maximum(m, jnp.max(s, axis=1, keepdims=True))
            alpha = jnp.exp(m - m_new)
            p = jnp.exp(s - m_new)
            mids.append((m_new, alpha, alpha * l + jnp.sum(p, axis=1, keepdims=True), p))
        out = []
        for i, (m_new, alpha, l, p) in enumerate(mids):
            v = kvs[i][1]
            acc = alpha * carry[i][2] + jnp.dot(p.astype(v.dtype), v, preferred_element_type=F32)
            out.append((m_new, l, acc))
        return tuple(out)

    one = (jnp.full((tq, 1), NEG, F32), jnp.zeros((tq, 1), F32), jnp.zeros((tq, HEAD_DIM), F32))
    return lax.fori_loop(lo, hi, body, tuple(one for _ in qs))


def _alibi_kernel(slope_ref, q_ref, k_ref, v_ref, *rest, tq, tk, max_dist, shared_kv, with_sel, with_lse):
    rest = list(rest)
    sel_ref = rest.pop(0) if with_sel else None
    o_ref = rest.pop(0)
    lse_ref = rest.pop(0) if with_lse else None
    hb = pl.program_id(1)
    q0 = pl.program_id(3) * tq
    nh = HEADS_PER_STEP
    rc = lax.broadcasted_iota(jnp.int32, (tq, tk), 0) - lax.broadcasted_iota(jnp.int32, (tq, tk), 1)
    if with_sel:
        sel = sel_ref[...]
        lane_blk = lax.broadcasted_iota(jnp.int32, (LANES, tk), 0)
        key_blk = lax.broadcasted_iota(jnp.int32, (LANES, tk), 1) // SLC_BLK

    def valid_fn(c):
        dist = rc + (q0 - c * tk)
        valid = dist >= 0
        if max_dist is not None:
            valid = valid & (dist <= max_dist)
        if with_sel:
            expand = (lane_blk == key_blk + (hb * (LANES // 4) + c * (tk // SLC_BLK))).astype(BF16)
            valid = valid & (jnp.dot(sel, expand, preferred_element_type=F32) > 0.5)
        return dist.astype(F32), valid

    def make_score(i):
        slope = slope_ref[hb * nh + i]

        def score_fn(s, c, shared):
            dist_f, valid = shared
            return jnp.where(valid, s * SCALE - slope * dist_f, NEG)
        return score_fn

    def load_kv(c, i):
        off = pl.multiple_of(c * tk, tk)
        cols = _head_cols(0 if shared_kv else i)
        return k_ref[pl.ds(off, tk), cols], v_ref[pl.ds(off, tk), cols]

    hi = (q0 + tq - 1) // tk + 1
    lo = 0 if max_dist is None else jnp.maximum(q0 - max_dist, 0) // tk
    res = _flash([q_ref[:, _head_cols(i)] for i in range(nh)], load_kv, lo, hi, valid_fn,
                 [make_score(i) for i in range(nh)])
    for i, (m, l, acc) in enumerate(res):
        o_ref[:, _head_cols(i)] = (acc / l).astype(o_ref.dtype)
        if with_lse:
            lse_ref[:, _head_cols(i)] = jnp.broadcast_to(m + jnp.log(l), (tq, HEAD_DIM))


def alibi_attention(qa, ka, va, slopes, *, B, Ts, n_sub, n_head_blocks, shared_kv, q_col, k_col, v_col,
                    max_dist, tq, tk, sel=None, with_lse=False):
    tq = min(tq, Ts)
    tk = min(tk, Ts)
    nq = Ts // tq
    rows = B * Ts
    wq = HEADS_PER_STEP * HEAD_DIM
    wkv = HEAD_DIM if shared_kv else wq
    in_specs = [pl.BlockSpec((tq, wq), lambda b, h, r, i, s: (b * nq + i, q_col(h, r))),
                pl.BlockSpec((Ts, wkv), lambda b, h, r, i, s: (b, k_col(h, r))),
                pl.BlockSpec((Ts, wkv), lambda b, h, r, i, s: (b, v_col(h, r)))]
    args = [qa, ka, va]
    if sel is not None:
        in_specs.append(pl.BlockSpec((tq, LANES), lambda b, h, r, i, s: (b * nq + i, 0)))
        args.append(sel)
    o_spec = pl.BlockSpec((tq, wq), lambda b, h, r, i, s: (b * nq + i, r * n_head_blocks + h))
    o_shape = jax.ShapeDtypeStruct((rows, n_sub * n_head_blocks * wq), F32)
    kern = functools.partial(_alibi_kernel, tq=tq, tk=tk, max_dist=max_dist, shared_kv=shared_kv,
                             with_sel=sel is not None, with_lse=with_lse)
    return pl.pallas_call(
        kern,
        grid_spec=pltpu.PrefetchScalarGridSpec(
            num_scalar_prefetch=1, grid=(B, n_head_blocks, n_sub, nq),
            in_specs=in_specs,
            out_specs=[o_spec, o_spec] if with_lse else o_spec),
        out_shape=[o_shape, o_shape] if with_lse else o_shape,
        compiler_params=_cp("parallel", "parallel", "parallel", "parallel"),
        name="alibi_attention",
    )(jnp.asarray(slopes, F32), *args)


def _fox_kernel(q_ref, k_ref, v_ref, fq_ref, fk_ref, o_ref, *, tq, tk):
    hb = pl.program_id(1)
    q0 = pl.program_id(2) * tq
    nh = HEADS_PER_STEP
    lane = lax.broadcasted_iota(jnp.int32, (tq, LANES), 1)
    f_tok = fq_ref[...]
    rc = lax.broadcasted_iota(jnp.int32, (tq, tk), 0) - lax.broadcasted_iota(jnp.int32, (tq, tk), 1)

    def make_score(i):
        fq = jnp.sum(jnp.where(lane == FORGET_LANE0 + hb * nh + i, f_tok, 0.0), axis=1, keepdims=True)

        def score_fn(s, c, causal):
            return jnp.where(causal, (s * SCALE + fq) - fk_ref[i, pl.ds(c, 1), :], NEG)
        return score_fn

    def load_kv(c, i):
        off = pl.multiple_of(c * tk, tk)
        return k_ref[pl.ds(off, tk), _head_cols(i)], v_ref[pl.ds(off, tk), _head_cols(i)]

    hi = (q0 + tq - 1) // tk + 1
    res = _flash([q_ref[:, _head_cols(i)] for i in range(nh)], load_kv, 0, hi,
                 lambda c: rc + (q0 - c * tk) >= 0, [make_score(i) for i in range(nh)])
    for i, (_, l, acc) in enumerate(res):
        o_ref[:, _head_cols(i)] = (acc / l).astype(o_ref.dtype)


def fox_attention(proj_fox, f_tok, f_rows, *, B, T, tq=256, tk=256):
    nq = T // tq
    nh = HEADS_PER_STEP
    nhb = FOX_HEADS // nh
    wq = nh * HEAD_DIM
    return pl.pallas_call(
        functools.partial(_fox_kernel, tq=tq, tk=tk),
        grid=(B, nhb, nq),
        in_specs=[pl.BlockSpec((tq, wq), lambda b, h, i: (b * nq + i, h)),
                  pl.BlockSpec((T, wq), lambda b, h, i: (b, nhb + h)),
                  pl.BlockSpec((T, wq), lambda b, h, i: (b, 2 * nhb + h)),
                  pl.BlockSpec((tq, LANES), lambda b, h, i: (b * nq + i, 0)),
                  pl.BlockSpec((nh, T // tk, tk), lambda b, h, i: (b * nhb + h, 0, 0))],
        out_specs=pl.BlockSpec((tq, wq), lambda b, h, i: (b * nq + i, h)),
        out_shape=jax.ShapeDtypeStruct((B * T, FOX_W), F32),
        compiler_params=_cp("parallel", "parallel", "parallel"),
        name="fox_attention",
    )(proj_fox, proj_fox, proj_fox, f_tok, f_rows)


def _cumsum_kernel(x_ref, b_ref, o_ref, *, T, tr):
    i = pl.program_id(1)
    z = x_ref[...] + b_ref[...]
    logf = jnp.minimum(z, 0.0) - jnp.log1p(jnp.exp(-jnp.abs(z)))
    row = lax.broadcasted_iota(jnp.int32, (tr, T), 0) + i * tr
    col = lax.broadcasted_iota(jnp.int32, (tr, T), 1)
    o_ref[...] = _dot3_lhs((row >= col).astype(BF16), logf)


def _dot3_lhs(m, x):
    hi, mid, lo = _split3(x)
    d = functools.partial(jnp.dot, preferred_element_type=F32)
    return (d(m, lo) + d(m, mid)) + d(m, hi)


def forget_cumsum(small, bias_row, *, B, T, tr=256):
    return pl.pallas_call(
        functools.partial(_cumsum_kernel, T=T, tr=tr),
        grid=(B, T // tr),
        in_specs=[pl.BlockSpec((T, LANES), lambda b, i: (b, 0)),
                  pl.BlockSpec((1, LANES), lambda b, i: (0, 0))],
        out_specs=pl.BlockSpec((tr, LANES), lambda b, i: (b * (T // tr) + i, 0)),
        out_shape=jax.ShapeDtypeStruct((B * T, LANES), F32),
        compiler_params=_cp("parallel", "parallel"),
        name="forget_cumsum",
    )(small, bias_row)


def _compress_kernel(x_ref, pe_ref, w1_ref, w2_ref, o_ref):
    x = x_ref[...].astype(F32)
    y0 = jnp.dot((x + pe_ref[0]).astype(BF16), w1_ref[0], preferred_element_type=F32)
    y1 = jnp.dot((x + pe_ref[1]).astype(BF16), w1_ref[1], preferred_element_type=F32)
    n = y1.shape[0]
    hmid = _gelu(y0 + pltpu.roll(y1, n - 1, 0))
    o_ref[...] = jnp.dot(hmid.astype(BF16), w2_ref[...], preferred_element_type=F32).astype(o_ref.dtype)


def compress_blocks(chunks, pe, w1, w2):
    _, G, nch, cw = chunks.shape
    return pl.pallas_call(
        _compress_kernel,
        grid=(2, G),
        in_specs=[pl.BlockSpec((None, None, nch, cw), lambda a, g: (a, g, 0, 0)),
                  pl.BlockSpec((None, 2, 1, cw), lambda a, g: (a, 0, 0, 0)),
                  pl.BlockSpec((None, 2, cw, HEAD_DIM), lambda a, g: (a, 0, 0, 0)),
                  pl.BlockSpec((None, HEAD_DIM, HEAD_DIM), lambda a, g: (a, 0, 0))],
        out_specs=pl.BlockSpec((None, None, nch, HEAD_DIM), lambda a, g: (a, g, 0, 0)),
        out_shape=jax.ShapeDtypeStruct((2, G, nch, HEAD_DIM), BF16),
        compiler_params=_cp("parallel", "parallel"),
        name="compress_blocks",
    )(chunks, pe, w1, w2)


def _cmp_kernel(slope_ref, q_ref, kc_ref, vc_ref, smat_ref, o_ref, sel_ref, *, tq, n_cmp, n_slc):
    q0 = pl.program_id(1) * tq
    nk = kc_ref.shape[1]
    pos = lax.broadcasted_iota(jnp.int32, (tq, nk), 0) + q0
    n_idx = lax.broadcasted_iota(jnp.int32, (tq, nk), 1)
    dist_c = pos - (n_idx * CMP_STRIDE + (CMP_LEN - 1))
    valid_c = (dist_c >= 0) & (n_idx < n_cmp)
    dist_f = dist_c.astype(F32)
    lane = lax.broadcasted_iota(jnp.int32, (tq, LANES), 1)
    qblk = (lax.broadcasted_iota(jnp.int32, (tq, LANES), 0) + q0) // SLC_BLK
    sel_all = jnp.zeros((tq, LANES), F32)
    for kv in range(NSA_KV_HEADS):
        kc = kc_ref[kv]
        vc = vc_ref[kv]
        imp = jnp.zeros((tq, nk), F32)
        for g in range(NSA_GROUP):
            hh = kv * NSA_GROUP + g
            q = q_ref[:, hh * HEAD_DIM:(hh + 1) * HEAD_DIM]
            s = _dot_nt(q, kc) * SCALE - slope_ref[hh] * dist_f
            s = jnp.where(valid_c, s, NEG)
            e = jnp.where(valid_c, jnp.exp(s - jnp.max(s, axis=1, keepdims=True)), 0.0)
            p = e / jnp.maximum(jnp.sum(e, axis=1, keepdims=True), 1e-30)
            o_ref[:, hh * HEAD_DIM:(hh + 1) * HEAD_DIM] = jnp.dot(
                p.astype(BF16), vc, preferred_element_type=F32)
            imp = imp + p
        p_slc = _dot3(imp, smat_ref[kv])
        j = lane - kv * (LANES // 4)
        mine = (j >= 0) & (j < n_slc)
        forced = (j == 0) | (j == qblk) | (j == qblk - 1)
        causal = j <= qblk
        score = jnp.where(forced, BIG, jnp.where(causal, p_slc, NEG))
        rank = jnp.zeros((tq, LANES), jnp.int32)
        for k in range(n_slc):
            col = score[:, kv * (LANES // 4) + k:kv * (LANES // 4) + k + 1]
            ahead = (col > score) | ((col == score) & (k < j))
            rank = rank + ahead.astype(jnp.int32)
        sel_all = sel_all + jnp.where(mine & causal & (rank < min(N_SEL, n_slc)), 1.0, 0.0)
    sel_ref[...] = sel_all.astype(sel_ref.dtype)


def cmp_attention(proj_a, kvc, smat, slopes, *, B, T, tq=256):
    nq = T // tq
    nch = kvc.shape[2]
    n_cmp = (T - CMP_LEN) // CMP_STRIDE + 1
    n_slc = T // SLC_BLK
    kvc4 = kvc.reshape(2, B, NSA_KV_HEADS, nch, HEAD_DIM)
    return pl.pallas_call(
        functools.partial(_cmp_kernel, tq=tq, n_cmp=n_cmp, n_slc=n_slc),
        grid_spec=pltpu.PrefetchScalarGridSpec(
            num_scalar_prefetch=1, grid=(B, nq),
            in_specs=[pl.BlockSpec((tq, NSA_Q_W), lambda b, i, s: (b * nq + i, 0)),
                      pl.BlockSpec((None, None, NSA_KV_HEADS, nch, HEAD_DIM), lambda b, i, s: (0, b, 0, 0, 0)),
                      pl.BlockSpec((None, None, NSA_KV_HEADS, nch, HEAD_DIM), lambda b, i, s: (1, b, 0, 0, 0)),
                      pl.BlockSpec((NSA_KV_HEADS, nch, LANES), lambda b, i, s: (0, 0, 0))],
            out_specs=[pl.BlockSpec((tq, NSA_Q_W), lambda b, i, s: (b * nq + i, 0)),
                       pl.BlockSpec((tq, LANES), lambda b, i, s: (b * nq + i, 0))]),
        out_shape=[jax.ShapeDtypeStruct((B * T, NSA_Q_W), F32),
                   jax.ShapeDtypeStruct((B * T, LANES), BF16)],
        compiler_params=_cp("parallel", "parallel"),
        name="cmp_attention",
    )(jnp.asarray(slopes, F32), proj_a, kvc4, kvc4, smat)


def _pool_matrix(nch, n_slc):
    r = SLC_BLK // CMP_STRIDE
    c_over = CMP_LEN // CMP_STRIDE
    n_cmp_max = nch
    m = np.zeros((NSA_KV_HEADS, nch, LANES), np.float32)
    for kv in range(NSA_KV_HEADS):
        for j in range(n_slc):
            for a in range(r):
                for b in range(c_over):
                    i = r * j + a + b
                    if i < n_cmp_max:
                        m[kv, i, kv * (LANES // 4) + j] += 1.0
    return m


def _nsa_merge_kernel(oc_ref, os_ref, ow_ref, g_ref, e_ref, o_ref):
    sig = jax.nn.sigmoid(g_ref[...])
    out = _dot3(sig, e_ref[0]) * oc_ref[...]
    out = out + _dot3(sig, e_ref[1]) * os_ref[...]
    out = out + _dot3(sig, e_ref[2]) * ow_ref[...]
    o_ref[...] = out.astype(o_ref.dtype)


def nsa_merge(o_cmp, o_slc, o_win, small, tr=256):
    N = o_cmp.shape[0]
    e = np.zeros((3, LANES, NSA_Q_W), np.float32)
    for br in range(3):
        for h in range(NSA_HEADS):
            e[br, GATE_LANE0 + h * 3 + br, h * HEAD_DIM:(h + 1) * HEAD_DIM] = 1.0
    row = pl.BlockSpec((tr, NSA_Q_W), lambda i: (i, 0))
    return pl.pallas_call(
        _nsa_merge_kernel,
        grid=(N // tr,),
        in_specs=[row, row, row,
                  pl.BlockSpec((tr, LANES), lambda i: (i, 0)),
                  pl.BlockSpec((3, LANES, NSA_Q_W), lambda i: (0, 0, 0))],
        out_specs=row,
        out_shape=jax.ShapeDtypeStruct((N, NSA_Q_W), BF16),
        compiler_params=_cp("parallel"),
        name="nsa_merge",
    )(o_cmp, o_slc, o_win, small, jnp.asarray(e, BF16))


def _dil_merge_kernel(o0, o1, o2, l0, l1, l2, o_ref):
    a, b, c = l0[...], l1[...], l2[...]
    m = jnp.maximum(jnp.maximum(a, b), c)
    ea, eb, ec = jnp.exp(a - m), jnp.exp(b - m), jnp.exp(c - m)
    z = ea + eb + ec
    o_ref[...] = ((ea / z) * o0[...] + (eb / z) * o1[...] + (ec / z) * o2[...]).astype(o_ref.dtype)


def dil_merge(outs, lses, tr=512):
    N, W = outs[0].shape
    row = pl.BlockSpec((tr, W), lambda i: (i, 0))
    return pl.pallas_call(
        _dil_merge_kernel,
        grid=(N // tr,),
        in_specs=[row] * 6,
        out_specs=row,
        out_shape=jax.ShapeDtypeStruct((N, W), BF16),
        compiler_params=_cp("parallel"),
        name="dil_merge",
    )(*outs, *lses)


def _branch_kernel(u_ref, wga, wgb, wgc, bga, bgb, bgc, oa, ob, oc, wa, wb, wc, z_ref):
    d = functools.partial(jnp.dot, preferred_element_type=F32)
    u = u_ref[...]
    z = jax.nn.sigmoid(d(u, wga[...]) + bga[...]) * d(oa[...], wa[...])
    z = z + jax.nn.sigmoid(d(u, wgb[...]) + bgb[...]) * d(ob[...], wb[...])
    z = z + jax.nn.sigmoid(d(u, wgc[...]) + bgc[...]) * d(oc[...], wc[...])
    z_ref[...] = z.astype(z_ref.dtype)


def gated_branches(u, w_gate, b_gate, o_a, o_b, o_c, wb_a, wb_b, wb_c, tm=512, tn=512):
    N, D = u.shape
    nj = D // tn

    def wg(k):
        return pl.BlockSpec((D, tn), lambda j, i: (0, k * nj + j))

    def bg(k):
        return pl.BlockSpec((1, tn), lambda j, i: (0, k * nj + j))

    def act(w):
        return pl.BlockSpec((tm, w), lambda j, i: (i, 0))

    def wup(w):
        return pl.BlockSpec((w, tn), lambda j, i: (0, j))

    return pl.pallas_call(
        _branch_kernel,
        grid=(nj, N // tm),
        in_specs=[act(D), wg(0), wg(1), wg(2), bg(0), bg(1), bg(2),
                  act(OUT_A), act(OUT_B), act(OUT_C), wup(OUT_A), wup(OUT_B), wup(OUT_C)],
        out_specs=pl.BlockSpec((tm, tn), lambda j, i: (i, j)),
        out_shape=jax.ShapeDtypeStruct((N, D), BF16),
        compiler_params=_cp("parallel", "parallel"),
        name="gated_branches",
    )(u, w_gate, w_gate, w_gate, b_gate, b_gate, b_gate, o_a, o_b, o_c, wb_a, wb_b, wb_c)


def _postnorm_kernel(x_ref, y_ref, g_ref, gam_ref, bet_ref, sc_ref, sh_ref, xo_ref, u_ref, *ut_ref):
    h = ALPHA * x_ref[...] + (1.0 + g_ref[...]) * y_ref[...]
    mu = jnp.mean(h, axis=-1, keepdims=True)
    hc = h - mu
    var = jnp.mean(hc * hc, axis=-1, keepdims=True)
    xn = hc * lax.rsqrt(var + LN_EPS) * gam_ref[...] + bet_ref[...]
    xo_ref[...] = xn
    u = xn * (1.0 + sc_ref[...]) + sh_ref[...]
    u_ref[...] = u.astype(u_ref.dtype)
    if ut_ref:
        ut_ref[0][...] = u.T.astype(u_ref.dtype)


def postnorm(x, y, mod_g, g_blk, gamma, beta, mod_n, sc_blk, sh_blk, T, tr=256, transposed=False):
    N, D = x.shape
    row = pl.BlockSpec((tr, D), lambda i: (i, 0))
    vec = pl.BlockSpec((1, D), lambda i: (0, 0))

    def mod(blk):
        return pl.BlockSpec((None, 1, D), lambda i: (i * tr // T, 0, blk))

    out_specs = [row, row]
    out_shape = [jax.ShapeDtypeStruct((N, D), F32), jax.ShapeDtypeStruct((N, D), BF16)]
    if transposed:
        out_specs.append(pl.BlockSpec((D, tr), lambda i: (0, i)))
        out_shape.append(jax.ShapeDtypeStruct((D, N), BF16))
    return pl.pallas_call(
        _postnorm_kernel,
        grid=(N // tr,),
        in_specs=[row, row, mod(g_blk), vec, vec, mod(sc_blk), mod(sh_blk)],
        out_specs=out_specs,
        out_shape=out_shape,
        compiler_params=_cp("parallel"),
        name="postnorm",
    )(x, y, mod_g, gamma, beta, mod_n, mod_n)


def _topk_rows(s, k):
    rows, tn = s.shape
    iota = lax.broadcasted_iota(jnp.int32, (rows, tn), 0)
    rank = jnp.full((rows, tn), k, jnp.int32)
    cur = s
    tops = []
    for r in range(k):
        m = jnp.max(cur, axis=0, keepdims=True)
        idx = jnp.min(jnp.where(cur == m, iota, rows), axis=0, keepdims=True)
        hit = iota == idx
        rank = jnp.where(hit, r, rank)
        cur = jnp.where(hit, -jnp.inf, cur)
        tops.append(m)
    return rank, tops


def _stack_rows(rows_list):
    n = len(rows_list)
    tn = rows_list[0].shape[1]
    iota = lax.broadcasted_iota(jnp.int32, (n, tn), 0)
    out = jnp.zeros((n, tn), rows_list[0].dtype)
    for r, row in enumerate(rows_list):
        out = jnp.where(iota == r, row, out)
    return out


def _peer_select_kernel(q_ref, sk_ref, e1_ref, l1_ref, e2_ref, r2_ref, *, cpt):
    K = PEER_TOPK
    half = PEER_QDIM // 2
    s1 = _dot_nt(sk_ref[0], q_ref[:, :half])
    s2 = _dot_nt(sk_ref[1], q_ref[:, half:])
    rank1, top1 = _topk_rows(s1, K)
    rank2, top2 = _topk_rows(s2, K)
    t2 = _stack_rows(top2)
    cand = jnp.concatenate([top1[ra] + t2 for ra in range(K // 2)]
                           + [_stack_rows(top1[K // 2:]) + top2[0]], axis=0)
    crank, _ = _topk_rows(cand, K)
    chosen = crank < K
    z = jnp.sum(jnp.where(chosen, jnp.exp(cand - (top1[0] + top2[0])), 0.0), axis=0, keepdims=True)
    cnt = chosen.astype(F32)
    row_len = [jnp.sum(cnt[ra * K:(ra + 1) * K], axis=0, keepdims=True) for ra in range(K // 2)]
    row_len += [cnt[K // 2 * K + i:K // 2 * K + i + 1] for i in range(K - K // 2)]
    l1 = jnp.zeros(s1.shape, F32)
    for ra in range(K):
        l1 = jnp.where(rank1 == ra, row_len[ra], l1)
    e1 = jnp.where(rank1 < K, jnp.exp(s1 - top1[0]) / z, 0.0)
    e2 = jnp.where(rank2 < K, jnp.exp(s2 - top2[0]), 0.0)
    for j in range(PEER_NKEYS // cpt):
        e1_ref[j] = e1[j * cpt:(j + 1) * cpt]
        l1_ref[j] = l1[j * cpt:(j + 1) * cpt]
    e2_ref[...] = e2
    r2_ref[...] = rank2.astype(F32)


def peer_select(q, subkeys, *, cpt, tn=256):
    N = q.shape[0]
    H = PEER_HEADS
    nt = PEER_NKEYS // cpt
    s1 = pl.BlockSpec((nt, None, cpt, tn), lambda i, h: (0, h, 0, i))
    s2 = pl.BlockSpec((None, PEER_NKEYS, tn), lambda i, h: (h, 0, i))
    sh1 = jax.ShapeDtypeStruct((nt, H, cpt, N), F32)
    sh2 = jax.ShapeDtypeStruct((H, PEER_NKEYS, N), F32)
    return pl.pallas_call(
        functools.partial(_peer_select_kernel, cpt=cpt),
        grid=(N // tn, H),
        in_specs=[pl.BlockSpec((tn, PEER_QDIM), lambda i, h: (i, h)),
                  pl.BlockSpec((None, 2, PEER_NKEYS, PEER_QDIM // 2), lambda i, h: (h, 0, 0, 0))],
        out_specs=[s1, s1, s2, s2],
        out_shape=[sh1, sh1, sh2, sh2],
        compiler_params=_cp("parallel", "parallel"),
        name="peer_select",
    )(q, subkeys)


def _peer_dense_kernel(ut_ref, tu_ref, tv_ref, e1_ref, l1_ref, e2_ref, r2_ref, o_ref, w_s, ht_s, a_s,
                       *, cpt, tnc):
    @pl.when(pl.program_id(1) == 0)
    def _():
        o_ref[...] = jnp.zeros_like(o_ref)

    nk = PEER_NKEYS
    cph = cpt // 2
    tm = ut_ref.shape[1]
    for cc in range(cpt):
        k, hc = divmod(cc, cph)
        rows = slice(k * cph * nk, (k + 1) * cph * nk)
        tc = slice(hc * (tm // cph), (hc + 1) * (tm // cph))
        ht_s[rows, tc] = jnp.dot(tu_ref[rows, :], ut_ref[:, tc], preferred_element_type=F32)
        w = None
        for h in range(PEER_HEADS):
            hit = r2_ref[h] < l1_ref[h, cc:cc + 1, :]
            wh = jnp.where(hit, e2_ref[h], 0.0) * e1_ref[h, cc:cc + 1, :]
            w = wh if w is None else w + wh
        w_s[cc * nk:(cc + 1) * nk, :] = w
    for k in range(2):
        rows = slice(k * cph * nk, (k + 1) * cph * nk)
        for c2 in range(cph):
            r = slice((k * cph + c2) * nk, (k * cph + c2 + 1) * nk)
            a_s[k, :, c2 * nk:(c2 + 1) * nk] = (w_s[r, :] * _gelu(ht_s[r, :])).T.astype(BF16)
        for nt in range(o_ref.shape[1] // tnc):
            cs = slice(nt * tnc, (nt + 1) * tnc)
            o_ref[:, cs] += jnp.dot(a_s[k], tv_ref[rows, cs], preferred_element_type=F32)


def peer_dense(ut, tu, tv, e1, l1, e2, r2, *, cpt, tm=512):
    D, N = ut.shape
    E = tu.shape[0]
    te = cpt * PEER_NKEYS
    H = PEER_HEADS
    once = pl.Buffered(1)
    s1 = pl.BlockSpec((None, H, cpt, tm), lambda i, j: (j, 0, 0, i))
    s2 = pl.BlockSpec((H, PEER_NKEYS, tm), lambda i, j: (0, 0, i), pipeline_mode=once)
    tab = pl.BlockSpec((te, D), lambda i, j: (j, 0))
    return pl.pallas_call(
        functools.partial(_peer_dense_kernel, cpt=cpt, tnc=512),
        grid=(N // tm, E // te),
        in_specs=[pl.BlockSpec((D, tm), lambda i, j: (0, i), pipeline_mode=once), tab, tab, s1, s1, s2, s2],
        out_specs=pl.BlockSpec((tm, D), lambda i, j: (i, 0)),
        out_shape=jax.ShapeDtypeStruct((N, D), F32),
        scratch_shapes=[pltpu.VMEM((te, tm), F32), pltpu.VMEM((te, tm), F32),
                        pltpu.VMEM((2, tm, te // 2), BF16)],
        compiler_params=_cp("parallel", "arbitrary"),
        name="peer_dense",
    )(ut, tu, tv, e1, l1, e2, r2)


def _hybrid_mixer(u, p, *, B, T):
    N = B * T
    sl = _alibi_slopes(ALIBI_HEADS)
    nsa_slopes = sl[0::2]
    dil_slopes = sl[1::2].reshape(DIL_GROUPS, DIL_HPG)

    proj_a = matmul(u, p["w_in"], BF16, layer=p["layer"], n_cols=NSA_A_W, tm=512, tn=768)
    small = matmul(u, p["w_small"], F32)
    proj_fox = matmul(u, p["w_fox"], BF16)
    proj_dil = matmul(u, p["w_dil"], BF16)

    nch = T // CMP_STRIDE
    kv_cmp = proj_a[:, OFF_NSA_KV:OFF_NSA_KV + 2 * NSA_KV_W].reshape(B, T, 2, NSA_KV_HEADS, HEAD_DIM)
    chunks = kv_cmp.transpose(2, 0, 3, 1, 4).reshape(2, B * NSA_KV_HEADS, nch, CMP_STRIDE * HEAD_DIM)
    kvc = compress_blocks(chunks, p["cmp_pe"], p["cmp_w1"], p["cmp_w2"])
    smat = jnp.asarray(_pool_matrix(nch, T // SLC_BLK), BF16)
    o_cmp, sel = cmp_attention(proj_a, kvc, smat, nsa_slopes, B=B, T=T)
    qh = NSA_HEADS
    kvh = NSA_KV_HEADS
    assert NSA_GROUP == HEADS_PER_STEP and DIL_HPG == HEADS_PER_STEP
    nsa_kw = dict(B=B, Ts=T, n_sub=1, n_head_blocks=kvh, shared_kv=True,
                  q_col=lambda h, r: h, tq=128, tk=256)
    o_slc = alibi_attention(proj_a, proj_a, proj_a, nsa_slopes, max_dist=None, sel=sel,
                            k_col=lambda h, r: qh + 2 * kvh + h,
                            v_col=lambda h, r: qh + 3 * kvh + h, **nsa_kw)
    o_win = alibi_attention(proj_a, proj_a, proj_a, nsa_slopes, max_dist=WIN - 1,
                            k_col=lambda h, r: qh + 4 * kvh + h,
                            v_col=lambda h, r: qh + 5 * kvh + h, **nsa_kw)
    o_a = nsa_merge(o_cmp, o_slc, o_win, small)

    f_tok = forget_cumsum(small, p["b_forget_row"], B=B, T=T)
    tkf = min(256, T)
    f_rows = f_tok[:, FORGET_LANE0:FORGET_LANE0 + FOX_HEADS].reshape(B, T, FOX_HEADS)
    f_rows = f_rows.transpose(0, 2, 1).reshape(B * FOX_HEADS, T // tkf, tkf)
    o_b = fox_attention(proj_fox, f_tok, f_rows, B=B, T=T, tq=min(128, T), tk=tkf).astype(BF16)

    outs, lses = [], []
    ncb = 3 * DIL_GROUPS
    for g, (window, dil) in enumerate(DIL_PAIRS):
        Ts = T // dil
        view = proj_dil.reshape(B * Ts, dil * 3 * DIL_W)
        o, lse = alibi_attention(
            view, view, view, dil_slopes[g] * np.float32(dil), B=B, Ts=Ts, n_sub=dil, n_head_blocks=1,
            shared_kv=False, max_dist=window // dil, tq=128, tk=128, with_lse=True,
            q_col=lambda h, r, g=g: r * ncb + g,
            k_col=lambda h, r, g=g: r * ncb + DIL_GROUPS + g,
            v_col=lambda h, r, g=g: r * ncb + 2 * DIL_GROUPS + g)
        outs.append(o.reshape(N, OUT_C))
        lses.append(lse.reshape(N, OUT_C))
    o_c = dil_merge(outs, lses)

    z = gated_branches(u, p["w_gate"], p["b_gate"], o_a, o_b, o_c, p["wb_a"], p["wb_b"], p["wb_c"])
    return matmul(z, p["w_out"], F32, layer=p["layer"])


def _peer_ffn(u, ut, p):
    cpt = 4
    q = matmul(u, p["peer_wq"], BF16, layer=p["layer"])
    e1, l1, e2, r2 = peer_select(q, p["peer_subkeys"], cpt=cpt)
    return peer_dense(ut, p["peer_u"], p["peer_v"], e1, l1, e2, r2, cpt=cpt)


def _layer_params(l, w_in, b_forget, cmp_pe, cmp_w1, cmp_w2, w_branch, w_gate, b_gate, w_out,
                  peer_wq, peer_subkeys, peer_u, peer_v):
    wi = w_in[l]
    D = wi.shape[0]
    w_small = jnp.concatenate(
        [wi[:, OFF_NSA_GATE:OFF_NSA_GATE + 3 * NSA_HEADS], wi[:, OFF_FOX_F:OFF_FOX_F + FOX_HEADS],
         jnp.zeros((D, LANES - 3 * NSA_HEADS - FOX_HEADS), wi.dtype)], axis=1)
    b_row = jnp.zeros((1, LANES), F32).at[0, FORGET_LANE0:FORGET_LANE0 + FOX_HEADS].set(b_forget[l])
    half = CMP_LEN // 2
    return dict(
        layer=l,
        w_in=w_in,
        w_small=w_small,
        w_fox=wi[:, OFF_FOX:OFF_FOX + 3 * FOX_W],
        w_dil=wi[:, OFF_DIL:],
        b_forget_row=b_row,
        cmp_pe=cmp_pe[l].reshape(2, 2, 1, half * HEAD_DIM),
        cmp_w1=cmp_w1[l].astype(BF16).reshape(2, 2, half * HEAD_DIM, HEAD_DIM),
        cmp_w2=cmp_w2[l].astype(BF16),
        w_gate=w_gate[l].astype(BF16),
        b_gate=b_gate[l].reshape(1, -1),
        wb_a=w_branch[l, :OUT_A].astype(BF16),
        wb_b=w_branch[l, OUT_A:OUT_A + OUT_B].astype(BF16),
        wb_c=w_branch[l, OUT_A + OUT_B:].astype(BF16),
        w_out=w_out,
        peer_wq=peer_wq,
        peer_subkeys=peer_subkeys[l].astype(BF16),
        peer_u=peer_u[l].astype(BF16),
        peer_v=peer_v[l].astype(BF16),
    )


def kernel(x, c, w_ada, b_ada, w_in, b_forget, cmp_pe, cmp_w1, cmp_w2, w_branch, w_gate, b_gate, w_out,
           ln1_g, ln1_b, peer_wq, peer_subkeys, peer_u, peer_v, ln2_g, ln2_b):
    B, T, D = x.shape
    L = w_ada.shape[0]
    N = B * T
    c_pad = jnp.zeros((8, D), BF16).at[:B].set(c.astype(BF16))
    mod = ada_project(c_pad, w_ada, b_ada.reshape(L, 1, 6 * D))
    mods = [mod[l].reshape(8, 1, 6 * D) for l in range(L)]
    xf = x.reshape(N, D)
    u = modulate(xf, mods[0], 1, 0, T)
    for l in range(L):
        p = _layer_params(l, w_in, b_forget, cmp_pe, cmp_w1, cmp_w2, w_branch, w_gate, b_gate, w_out,
                          peer_wq, peer_subkeys, peer_u, peer_v)
        y = _hybrid_mixer(u, p, B=B, T=T)
        xf, u, ut = postnorm(xf, y, mods[l], 2, ln1_g[l].reshape(1, D), ln1_b[l].reshape(1, D),
                             mods[l], 4, 3, T, transposed=True)
        y = _peer_ffn(u, ut, p)
        nxt = mods[min(l + 1, L - 1)]
        xf, u = postnorm(xf, y, mods[l], 5, ln2_g[l].reshape(1, D), ln2_b[l].reshape(1, D),
                         nxt, 1, 0, T)
    return xf.reshape(B, T, D)
```

```python
import functools

import numpy as np
import jax
import jax.numpy as jnp
from jax import lax
from jax.experimental import pallas as pl
from jax.experimental.pallas import tpu as pltpu

F32 = jnp.float32
BF16 = jnp.bfloat16

D_MODEL = 4096
HEAD_DIM = 128
NSA_HEADS = 12
NSA_KV_HEADS = 3
NSA_GROUP = NSA_HEADS // NSA_KV_HEADS
CMP_LEN = 32
CMP_STRIDE = 16
SLC_BLK = 64
N_SEL = 16
WIN = 512
FOX_HEADS = 8
DIL_PAIRS = ((128, 1), (512, 4), (2048, 16))
DIL_GROUPS = 3
DIL_HPG = 4
DIL_HEADS = DIL_GROUPS * DIL_HPG
ALIBI_HEADS = NSA_HEADS + DIL_HEADS
NSA_Q_W = NSA_HEADS * HEAD_DIM
NSA_KV_W = NSA_KV_HEADS * HEAD_DIM
FOX_W = FOX_HEADS * HEAD_DIM
DIL_W = DIL_HEADS * HEAD_DIM
OFF_NSA_KV = NSA_Q_W
OFF_NSA_GATE = OFF_NSA_KV + 6 * NSA_KV_W
OFF_FOX = OFF_NSA_GATE + 3 * NSA_HEADS
OFF_FOX_F = OFF_FOX + 3 * FOX_W
OFF_DIL = OFF_FOX_F + FOX_HEADS
IN_COLS = OFF_DIL + 3 * DIL_W
NSA_A_W = NSA_Q_W + 6 * NSA_KV_W
OUT_A = NSA_Q_W
OUT_B = FOX_W
OUT_C = DIL_HPG * HEAD_DIM
PEER_HEADS = 8
PEER_NKEYS = 128
PEER_EXPERTS = PEER_NKEYS ** 2
PEER_TOPK = 16
PEER_QDIM = 256
DEPTH = 2
ALPHA = (2.0 * DEPTH) ** 0.25
LN_EPS = 1e-5
NEG = -1e30
BIG = 1e30
SCALE = HEAD_DIM ** -0.5

LANES = 128
GATE_LANE0 = 0
FORGET_LANE0 = 3 * NSA_HEADS
VMEM_LIMIT = 56 * 1024 * 1024


def _cp(*sem):
    return pltpu.CompilerParams(dimension_semantics=sem, vmem_limit_bytes=VMEM_LIMIT)


def _alibi_slopes(n):
    return (np.float32(2.0) ** (-8.0 * np.arange(1, n + 1, dtype=np.float32) / n)).astype(np.float32)


def _gelu(x):
    return 0.5 * x * (1.0 + jnp.tanh(np.float32(np.sqrt(2.0 / np.pi)) * (x + np.float32(0.044715) * (x * x * x))))


def _split3(x):
    hi = x.astype(BF16)
    r1 = x - hi.astype(F32)
    mid = r1.astype(BF16)
    lo = (r1 - mid.astype(F32)).astype(BF16)
    return hi, mid, lo


def _dot3(x, m):
    hi, mid, lo = _split3(x)
    d = functools.partial(jnp.dot, preferred_element_type=F32)
    return (d(lo, m) + d(mid, m)) + d(hi, m)


def _dot_nt(a, b):
    return lax.dot_general(a, b, (((1,), (1,)), ((), ())), preferred_element_type=F32)


def _ada_kernel(c_ref, w_ref, b_ref, o_ref):
    o_ref[...] = jnp.dot(c_ref[...], w_ref[...].astype(BF16), preferred_element_type=F32) + b_ref[...]


def ada_project(c_pad, w_ada, b_ada, tn=512):
    L, D, N6 = w_ada.shape
    return pl.pallas_call(
        _ada_kernel,
        grid=(L, N6 // tn),
        in_specs=[pl.BlockSpec((8, D), lambda l, j: (0, 0)),
                  pl.BlockSpec((None, D, tn), lambda l, j: (l, 0, j)),
                  pl.BlockSpec((None, 1, tn), lambda l, j: (l, 0, j))],
        out_specs=pl.BlockSpec((None, 8, tn), lambda l, j: (l, 0, j)),
        out_shape=jax.ShapeDtypeStruct((L, 8, N6), F32),
        compiler_params=_cp("parallel", "parallel"),
        name="ada_project",
    )(c_pad, w_ada, b_ada)


def _modulate_kernel(x_ref, sc_ref, sh_ref, u_ref):
    u_ref[...] = (x_ref[...] * (1.0 + sc_ref[...]) + sh_ref[...]).astype(u_ref.dtype)


def modulate(x, mod3, sc_blk, sh_blk, T, tr=256):
    N, D = x.shape
    return pl.pallas_call(
        _modulate_kernel,
        grid=(N // tr,),
        in_specs=[pl.BlockSpec((tr, D), lambda i: (i, 0)),
                  pl.BlockSpec((None, 1, D), lambda i: (i * tr // T, 0, sc_blk)),
                  pl.BlockSpec((None, 1, D), lambda i: (i * tr // T, 0, sh_blk))],
        out_specs=pl.BlockSpec((tr, D), lambda i: (i, 0)),
        out_shape=jax.ShapeDtypeStruct((N, D), BF16),
        compiler_params=_cp("parallel"),
        name="modulate",
    )(x, mod3, mod3)


def _mm_kernel(a_ref, w_ref, *rest, shift):
    if shift:
        x_ref, o_ref, wb_ref = rest
    else:
        o_ref, wb_ref = rest

    @pl.when(pl.program_id(1) == 0)
    def _():
        if not shift:
            wb_ref[...] = w_ref[...].astype(BF16)
            return
        nb = w_ref.shape[1] // LANES
        lane = lax.broadcasted_iota(jnp.int32, (w_ref.shape[0], LANES), 1)
        rolled = [pltpu.roll(w_ref[:, b * LANES:(b + 1) * LANES], LANES - shift, 1) for b in range(nb)]
        rolled.append(pltpu.roll(x_ref[...], LANES - shift, 1))
        for b in range(nb):
            wb_ref[:, b * LANES:(b + 1) * LANES] = jnp.where(
                lane < LANES - shift, rolled[b], rolled[b + 1]).astype(BF16)

    o_ref[...] = jnp.dot(a_ref[...], wb_ref[...], preferred_element_type=F32).astype(o_ref.dtype)


def matmul(a, w, out_dtype, layer=0, col0=0, n_cols=None, tm=1024, tn=512):
    M, K = a.shape
    if w.ndim == 2:
        w = w[None]
    N = w.shape[2] - col0 if n_cols is None else n_cols
    tm = min(tm, M)
    tn = min(tn, N)
    shift = col0 % LANES
    base = col0 - shift
    assert M % tm == 0 and N % tn == 0 and base % tn == 0
    in_specs = [pl.BlockSpec((tm, K), lambda j, i: (i, 0)),
                pl.BlockSpec((None, K, tn), lambda j, i: (layer, 0, base // tn + j))]
    args = [a, w]
    if shift:
        in_specs.append(pl.BlockSpec((None, K, LANES), lambda j, i: (layer, 0, (base + (j + 1) * tn) // LANES)))
        args.append(w)
    return pl.pallas_call(
        functools.partial(_mm_kernel, shift=shift),
        grid=(N // tn, M // tm),
        in_specs=in_specs,
        out_specs=pl.BlockSpec((tm, tn), lambda j, i: (i, j)),
        out_shape=jax.ShapeDtypeStruct((M, N), out_dtype),
        scratch_shapes=[pltpu.VMEM((K, tn), BF16)],
        compiler_params=_cp("parallel", "arbitrary"),
        name="matmul",
    )(*args)


def _small_proj_kernel(a_ref, wg_ref, wf_ref, o_ref):
    a = a_ref[...]
    lane = lax.broadcasted_iota(jnp.int32, o_ref.shape, 1)
    o_ref[...] = jnp.where(lane < FORGET_LANE0,
                           jnp.dot(a, wg_ref[...].astype(BF16), preferred_element_type=F32),
                           jnp.dot(a, wf_ref[...].astype(BF16), preferred_element_type=F32))


def small_projection(a, w_in, layer, tm=1024):
    M, K = a.shape
    assert OFF_NSA_GATE % LANES == GATE_LANE0 and OFF_FOX_F % LANES == FORGET_LANE0
    return pl.pallas_call(
        _small_proj_kernel,
        grid=(M // tm,),
        in_specs=[pl.BlockSpec((tm, K), lambda i: (i, 0)),
                  pl.BlockSpec((None, K, LANES), lambda i: (layer, 0, OFF_NSA_GATE // LANES)),
                  pl.BlockSpec((None, K, LANES), lambda i: (layer, 0, OFF_FOX_F // LANES))],
        out_specs=pl.BlockSpec((tm, LANES), lambda i: (i, 0)),
        out_shape=jax.ShapeDtypeStruct((M, LANES), F32),
        compiler_params=_cp("parallel"),
        name="small_projection",
    )(a, w_in, w_in)


HEADS_PER_STEP = 4


def _head_cols(i):
    return slice(i * HEAD_DIM, (i + 1) * HEAD_DIM)


def _flash(qs, load_kv, lo, hi, chunk_fn, score_fns):
    tq = qs[0].shape[0]

    def body(c, carry):
        shared = chunk_fn(c)
        kvs = [load_kv(c, i) for i in range(len(qs))]
        raws = [_dot_nt(q, kvs[i][0]) for i, q in enumerate(qs)]
        mids = []
        for i, raw in enumerate(raws):
            m, l, _ = carry[i]
            s = score_fns[i](raw, c, shared)
            m_new = jnp.maximum(m, jnp.max(s, axis=1, keepdims=True))
            alpha = jnp.exp(m - m_new)
            p = jnp.exp(s - m_new)
            mids.append((m_new, alpha, alpha * l + jnp.sum(p, axis=1, keepdims=True), p))
        out = []
        for i, (m_new, alpha, l, p) in enumerate(mids):
            v = kvs[i][1]
            acc = alpha * carry[i][2] + jnp.dot(p.astype(v.dtype), v, preferred_element_type=F32)
            out.append((m_new, l, acc))
        return tuple(out)

    one = (jnp.full((tq, 1), NEG, F32), jnp.zeros((tq, 1), F32), jnp.zeros((tq, HEAD_DIM), F32))
    return lax.fori_loop(lo, hi, body, tuple(one for _ in qs))


def _alibi_kernel(slope_ref, q_ref, k_ref, v_ref, *rest, tq, tk, max_dist, shared_kv, with_sel, with_lse):
    rest = list(rest)
    sel_ref = rest.pop(0) if with_sel else None
    o_ref = rest.pop(0)
    lse_ref = rest.pop(0) if with_lse else None
    hb = pl.program_id(1)
    q0 = pl.program_id(3) * tq
    nh = HEADS_PER_STEP
    rc = lax.broadcasted_iota(jnp.int32, (tq, tk), 0) - lax.broadcasted_iota(jnp.int32, (tq, tk), 1)
    if with_sel:
        sel = sel_ref[...]
        lane_blk = lax.broadcasted_iota(jnp.int32, (LANES, tk), 0)
        key_blk = lax.broadcasted_iota(jnp.int32, (LANES, tk), 1) // SLC_BLK

    def valid_fn(c):
        dist = rc + (q0 - c * tk)
        valid = dist >= 0
        if max_dist is not None:
            valid = valid & (dist <= max_dist)
        if with_sel:
            expand = (lane_blk == key_blk + (hb * (LANES // 4) + c * (tk // SLC_BLK))).astype(BF16)
            valid = valid & (jnp.dot(sel, expand, preferred_element_type=F32) > 0.5)
        return dist.astype(F32), valid

    def make_score(i):
        slope = slope_ref[hb * nh + i]

        def score_fn(s, c, shared):
            dist_f, valid = shared
            return jnp.where(valid, s * SCALE - slope * dist_f, NEG)
        return score_fn

    def load_kv(c, i):
        off = pl.multiple_of(c * tk, tk)
        cols = _head_cols(0 if shared_kv else i)
        return k_ref[pl.ds(off, tk), cols], v_ref[pl.ds(off, tk), cols]

    hi = (q0 + tq - 1) // tk + 1
    lo = 0 if max_dist is None else jnp.maximum(q0 - max_dist, 0) // tk
    res = _flash([q_ref[:, _head_cols(i)] for i in range(nh)], load_kv, lo, hi, valid_fn,
                 [make_score(i) for i in range(nh)])
    for i, (m, l, acc) in enumerate(res):
        o_ref[:, _head_cols(i)] = (acc / l).astype(o_ref.dtype)
        if with_lse:
            lse_ref[:, _head_cols(i)] = jnp.broadcast_to(m + jnp.log(l), (tq, HEAD_DIM))


def alibi_attention(qa, ka, va, slopes, *, B, Ts, n_sub, n_head_blocks, shared_kv, q_col, k_col, v_col,
                    max_dist, tq, tk, sel=None, with_lse=False):
    tq = min(tq, Ts)
    tk = min(tk, Ts)
    nq = Ts // tq
    rows = B * Ts
    wq = HEADS_PER_STEP * HEAD_DIM
    wkv = HEAD_DIM if shared_kv else wq
    in_specs = [pl.BlockSpec((tq, wq), lambda b, h, r, i, s: (b * nq + i, q_col(h, r))),
                pl.BlockSpec((Ts, wkv), lambda b, h, r, i, s: (b, k_col(h, r))),
                pl.BlockSpec((Ts, wkv), lambda b, h, r, i, s: (b, v_col(h, r)))]
    args = [qa, ka, va]
    if sel is not None:
        in_specs.append(pl.BlockSpec((tq, LANES), lambda b, h, r, i, s: (b * nq + i, 0)))
        args.append(sel)
    o_spec = pl.BlockSpec((tq, wq), lambda b, h, r, i, s: (b * nq + i, r * n_head_blocks + h))
    o_shape = jax.ShapeDtypeStruct((rows, n_sub * n_head_blocks * wq), F32)
    kern = functools.partial(_alibi_kernel, tq=tq, tk=tk, max_dist=max_dist, shared_kv=shared_kv,
                             with_sel=sel is not None, with_lse=with_lse)
    return pl.pallas_call(
        kern,
        grid_spec=pltpu.PrefetchScalarGridSpec(
            num_scalar_prefetch=1, grid=(B, n_head_blocks, n_sub, nq),
            in_specs=in_specs,
            out_specs=[o_spec, o_spec] if with_lse else o_spec),
        out_shape=[o_shape, o_shape] if with_lse else o_shape,
        compiler_params=_cp("parallel", "parallel", "parallel", "parallel"),
        name="alibi_attention",
    )(jnp.asarray(slopes, F32), *args)


def _fox_kernel(q_ref, k_ref, v_ref, fq_ref, fk_ref, o_ref, *, tq, tk):
    hb = pl.program_id(1)
    q0 = pl.program_id(2) * tq
    nh = HEADS_PER_STEP
    lane = lax.broadcasted_iota(jnp.int32, (tq, LANES), 1)
    f_tok = fq_ref[...]
    rc = lax.broadcasted_iota(jnp.int32, (tq, tk), 0) - lax.broadcasted_iota(jnp.int32, (tq, tk), 1)

    def make_score(i):
        fq = jnp.sum(jnp.where(lane == FORGET_LANE0 + hb * nh + i, f_tok, 0.0), axis=1, keepdims=True)

        def score_fn(s, c, causal):
            return jnp.where(causal, (s * SCALE + fq) - fk_ref[i, pl.ds(c, 1), :], NEG)
        return score_fn

    def load_kv(c, i):
        off = pl.multiple_of(c * tk, tk)
        return k_ref[pl.ds(off, tk), _head_cols(i)], v_ref[pl.ds(off, tk), _head_cols(i)]

    hi = (q0 + tq - 1) // tk + 1
    res = _flash([q_ref[:, _head_cols(i)] for i in range(nh)], load_kv, 0, hi,
                 lambda c: rc + (q0 - c * tk) >= 0, [make_score(i) for i in range(nh)])
    for i, (_, l, acc) in enumerate(res):
        o_ref[:, _head_cols(i)] = (acc / l).astype(o_ref.dtype)


def fox_attention(proj_fox, f_tok, f_rows, *, B, T, tq=256, tk=256):
    nq = T // tq
    nh = HEADS_PER_STEP
    nhb = FOX_HEADS // nh
    wq = nh * HEAD_DIM
    return pl.pallas_call(
        functools.partial(_fox_kernel, tq=tq, tk=tk),
        grid=(B, nhb, nq),
        in_specs=[pl.BlockSpec((tq, wq), lambda b, h, i: (b * nq + i, h)),
                  pl.BlockSpec((T, wq), lambda b, h, i: (b, nhb + h)),
                  pl.BlockSpec((T, wq), lambda b, h, i: (b, 2 * nhb + h)),
                  pl.BlockSpec((tq, LANES), lambda b, h, i: (b * nq + i, 0)),
                  pl.BlockSpec((nh, T // tk, tk), lambda b, h, i: (b * nhb + h, 0, 0))],
        out_specs=pl.BlockSpec((tq, wq), lambda b, h, i: (b * nq + i, h)),
        out_shape=jax.ShapeDtypeStruct((B * T, FOX_W), BF16),
        compiler_params=_cp("parallel", "parallel", "parallel"),
        name="fox_attention",
    )(proj_fox, proj_fox, proj_fox, f_tok, f_rows)


def _cumsum_kernel(x_ref, b_ref, o_ref, *, T, tr):
    i = pl.program_id(1)
    z = x_ref[...] + b_ref[...]
    logf = jnp.minimum(z, 0.0) - jnp.log1p(jnp.exp(-jnp.abs(z)))
    row = lax.broadcasted_iota(jnp.int32, (tr, T), 0) + i * tr
    col = lax.broadcasted_iota(jnp.int32, (tr, T), 1)
    o_ref[...] = _dot3_lhs((row >= col).astype(BF16), logf)


def _dot3_lhs(m, x):
    hi, mid, lo = _split3(x)
    d = functools.partial(jnp.dot, preferred_element_type=F32)
    return (d(m, lo) + d(m, mid)) + d(m, hi)


def forget_cumsum(small, bias_row, *, B, T, tr=256):
    return pl.pallas_call(
        functools.partial(_cumsum_kernel, T=T, tr=tr),
        grid=(B, T // tr),
        in_specs=[pl.BlockSpec((T, LANES), lambda b, i: (b, 0)),
                  pl.BlockSpec((1, LANES), lambda b, i: (0, 0))],
        out_specs=pl.BlockSpec((tr, LANES), lambda b, i: (b * (T // tr) + i, 0)),
        out_shape=jax.ShapeDtypeStruct((B * T, LANES), F32),
        compiler_params=_cp("parallel", "parallel"),
        name="forget_cumsum",
    )(small, bias_row)


def _compress_kernel(x_ref, pe_ref, w1_ref, w2_ref, o_ref):
    x = x_ref[...].astype(F32)
    y0 = jnp.dot((x + pe_ref[0]).astype(BF16), w1_ref[0], preferred_element_type=F32)
    y1 = jnp.dot((x + pe_ref[1]).astype(BF16), w1_ref[1], preferred_element_type=F32)
    n = y1.shape[0]
    hmid = _gelu(y0 + pltpu.roll(y1, n - 1, 0))
    o_ref[...] = jnp.dot(hmid.astype(BF16), w2_ref[...], preferred_element_type=F32).astype(o_ref.dtype)


def compress_blocks(chunks, pe, w1, w2):
    _, G, nch, cw = chunks.shape
    return pl.pallas_call(
        _compress_kernel,
        grid=(2, G),
        in_specs=[pl.BlockSpec((None, None, nch, cw), lambda a, g: (a, g, 0, 0)),
                  pl.BlockSpec((None, 2, 1, cw), lambda a, g: (a, 0, 0, 0)),
                  pl.BlockSpec((None, 2, cw, HEAD_DIM), lambda a, g: (a, 0, 0, 0)),
                  pl.BlockSpec((None, HEAD_DIM, HEAD_DIM), lambda a, g: (a, 0, 0))],
        out_specs=pl.BlockSpec((None, None, nch, HEAD_DIM), lambda a, g: (a, g, 0, 0)),
        out_shape=jax.ShapeDtypeStruct((2, G, nch, HEAD_DIM), BF16),
        compiler_params=_cp("parallel", "parallel"),
        name="compress_blocks",
    )(chunks, pe, w1, w2)


def _cmp_kernel(slope_ref, q_ref, kc_ref, vc_ref, smat_ref, o_ref, sel_ref, *, tq, n_cmp, n_slc):
    q0 = pl.program_id(1) * tq
    nk = kc_ref.shape[1]
    pos = lax.broadcasted_iota(jnp.int32, (tq, nk), 0) + q0
    n_idx = lax.broadcasted_iota(jnp.int32, (tq, nk), 1)
    dist_c = pos - (n_idx * CMP_STRIDE + (CMP_LEN - 1))
    valid_c = (dist_c >= 0) & (n_idx < n_cmp)
    dist_f = dist_c.astype(F32)
    lane = lax.broadcasted_iota(jnp.int32, (tq, LANES), 1)
    qblk = (lax.broadcasted_iota(jnp.int32, (tq, LANES), 0) + q0) // SLC_BLK
    sel_all = jnp.zeros((tq, LANES), F32)
    for kv in range(NSA_KV_HEADS):
        kc = kc_ref[kv]
        vc = vc_ref[kv]
        imp = jnp.zeros((tq, nk), F32)
        for g in range(NSA_GROUP):
            hh = kv * NSA_GROUP + g
            q = q_ref[:, hh * HEAD_DIM:(hh + 1) * HEAD_DIM]
            s = _dot_nt(q, kc) * SCALE - slope_ref[hh] * dist_f
            s = jnp.where(valid_c, s, NEG)
            e = jnp.where(valid_c, jnp.exp(s - jnp.max(s, axis=1, keepdims=True)), 0.0)
            p = e / jnp.maximum(jnp.sum(e, axis=1, keepdims=True), 1e-30)
            o_ref[:, hh * HEAD_DIM:(hh + 1) * HEAD_DIM] = jnp.dot(
                p.astype(BF16), vc, preferred_element_type=F32)
            imp = imp + p
        p_slc = _dot3(imp, smat_ref[kv])
        j = lane - kv * (LANES // 4)
        mine = (j >= 0) & (j < n_slc)
        forced = (j == 0) | (j == qblk) | (j == qblk - 1)
        causal = j <= qblk
        score = jnp.where(forced, BIG, jnp.where(causal, p_slc, NEG))
        rank = jnp.zeros((tq, LANES), jnp.int32)
        for k in range(n_slc):
            col = score[:, kv * (LANES // 4) + k:kv * (LANES // 4) + k + 1]
            ahead = (col > score) | ((col == score) & (k < j))
            rank = rank + ahead.astype(jnp.int32)
        sel_all = sel_all + jnp.where(mine & causal & (rank < min(N_SEL, n_slc)), 1.0, 0.0)
    sel_ref[...] = sel_all.astype(sel_ref.dtype)


def cmp_attention(proj_a, kvc, smat, slopes, *, B, T, tq=256):
    nq = T // tq
    nch = kvc.shape[2]
    n_cmp = (T - CMP_LEN) // CMP_STRIDE + 1
    n_slc = T // SLC_BLK
    kvc4 = kvc.reshape(2, B, NSA_KV_HEADS, nch, HEAD_DIM)
    return pl.pallas_call(
        functools.partial(_cmp_kernel, tq=tq, n_cmp=n_cmp, n_slc=n_slc),
        grid_spec=pltpu.PrefetchScalarGridSpec(
            num_scalar_prefetch=1, grid=(B, nq),
            in_specs=[pl.BlockSpec((tq, NSA_Q_W), lambda b, i, s: (b * nq + i, 0)),
                      pl.BlockSpec((None, None, NSA_KV_HEADS, nch, HEAD_DIM), lambda b, i, s: (0, b, 0, 0, 0)),
                      pl.BlockSpec((None, None, NSA_KV_HEADS, nch, HEAD_DIM), lambda b, i, s: (1, b, 0, 0, 0)),
                      pl.BlockSpec((NSA_KV_HEADS, nch, LANES), lambda b, i, s: (0, 0, 0))],
            out_specs=[pl.BlockSpec((tq, NSA_Q_W), lambda b, i, s: (b * nq + i, 0)),
                       pl.BlockSpec((tq, LANES), lambda b, i, s: (b * nq + i, 0))]),
        out_shape=[jax.ShapeDtypeStruct((B * T, NSA_Q_W), F32),
                   jax.ShapeDtypeStruct((B * T, LANES), BF16)],
        compiler_params=_cp("parallel", "parallel"),
        name="cmp_attention",
    )(jnp.asarray(slopes, F32), proj_a, kvc4, kvc4, smat)


def _pool_matrix(nch, n_slc):
    r = SLC_BLK // CMP_STRIDE
    c_over = CMP_LEN // CMP_STRIDE
    n_cmp_max = nch
    m = np.zeros((NSA_KV_HEADS, nch, LANES), np.float32)
    for kv in range(NSA_KV_HEADS):
        for j in range(n_slc):
            for a in range(r):
                for b in range(c_over):
                    i = r * j + a + b
                    if i < n_cmp_max:
                        m[kv, i, kv * (LANES // 4) + j] += 1.0
    return m


def _nsa_merge_kernel(oc_ref, os_ref, ow_ref, g_ref, e_ref, o_ref):
    sig = jax.nn.sigmoid(g_ref[...])
    out = _dot3(sig, e_ref[0]) * oc_ref[...]
    out = out + _dot3(sig, e_ref[1]) * os_ref[...]
    out = out + _dot3(sig, e_ref[2]) * ow_ref[...]
    o_ref[...] = out.astype(o_ref.dtype)


def nsa_merge(o_cmp, o_slc, o_win, small, tr=256):
    N = o_cmp.shape[0]
    e = np.zeros((3, LANES, NSA_Q_W), np.float32)
    for br in range(3):
        for h in range(NSA_HEADS):
            e[br, GATE_LANE0 + h * 3 + br, h * HEAD_DIM:(h + 1) * HEAD_DIM] = 1.0
    row = pl.BlockSpec((tr, NSA_Q_W), lambda i: (i, 0))
    return pl.pallas_call(
        _nsa_merge_kernel,
        grid=(N // tr,),
        in_specs=[row, row, row,
                  pl.BlockSpec((tr, LANES), lambda i: (i, 0)),
                  pl.BlockSpec((3, LANES, NSA_Q_W), lambda i: (0, 0, 0))],
        out_specs=row,
        out_shape=jax.ShapeDtypeStruct((N, NSA_Q_W), BF16),
        compiler_params=_cp("parallel"),
        name="nsa_merge",
    )(o_cmp, o_slc, o_win, small, jnp.asarray(e, BF16))


def _dil_merge_kernel(o0, o1, o2, l0, l1, l2, o_ref):
    a, b, c = l0[...], l1[...], l2[...]
    m = jnp.maximum(jnp.maximum(a, b), c)
    ea, eb, ec = jnp.exp(a - m), jnp.exp(b - m), jnp.exp(c - m)
    z = ea + eb + ec
    o_ref[...] = ((ea / z) * o0[...] + (eb / z) * o1[...] + (ec / z) * o2[...]).astype(o_ref.dtype)


def dil_merge(outs, lses, tr=512):
    N, W = outs[0].shape
    row = pl.BlockSpec((tr, W), lambda i: (i, 0))
    return pl.pallas_call(
        _dil_merge_kernel,
        grid=(N // tr,),
        in_specs=[row] * 6,
        out_specs=row,
        out_shape=jax.ShapeDtypeStruct((N, W), BF16),
        compiler_params=_cp("parallel"),
        name="dil_merge",
    )(*outs, *lses)


def _branch_kernel(u_ref, wga, wgb, wgc, bga, bgb, bgc, oa, ob, oc, wa, wb, wc, z_ref):
    d = functools.partial(jnp.dot, preferred_element_type=F32)
    u = u_ref[...]
    z = jax.nn.sigmoid(d(u, wga[...]) + bga[...]) * d(oa[...], wa[...])
    z = z + jax.nn.sigmoid(d(u, wgb[...]) + bgb[...]) * d(ob[...], wb[...])
    z = z + jax.nn.sigmoid(d(u, wgc[...]) + bgc[...]) * d(oc[...], wc[...])
    z_ref[...] = z.astype(z_ref.dtype)


def gated_branches(u, w_gate, b_gate, layer, o_a, o_b, o_c, wb_a, wb_b, wb_c, tm=512, tn=512):
    N, D = u.shape
    nj = D // tn

    def wg(k):
        return pl.BlockSpec((None, D, tn), lambda j, i: (layer, 0, k * nj + j))

    def bg(k):
        return pl.BlockSpec((None, 1, tn), lambda j, i: (layer, 0, k * nj + j))

    def act(w):
        return pl.BlockSpec((tm, w), lambda j, i: (i, 0))

    def wup(w):
        return pl.BlockSpec((w, tn), lambda j, i: (0, j))

    return pl.pallas_call(
        _branch_kernel,
        grid=(nj, N // tm),
        in_specs=[act(D), wg(0), wg(1), wg(2), bg(0), bg(1), bg(2),
                  act(OUT_A), act(OUT_B), act(OUT_C), wup(OUT_A), wup(OUT_B), wup(OUT_C)],
        out_specs=pl.BlockSpec((tm, tn), lambda j, i: (i, j)),
        out_shape=jax.ShapeDtypeStruct((N, D), BF16),
        compiler_params=_cp("parallel", "parallel"),
        name="gated_branches",
    )(u, w_gate, w_gate, w_gate, b_gate, b_gate, b_gate, o_a, o_b, o_c, wb_a, wb_b, wb_c)


def _postnorm_kernel(x_ref, y_ref, g_ref, gam_ref, bet_ref, sc_ref, sh_ref, xo_ref, u_ref, *ut_ref):
    h = ALPHA * x_ref[...] + (1.0 + g_ref[...]) * y_ref[...]
    mu = jnp.mean(h, axis=-1, keepdims=True)
    hc = h - mu
    var = jnp.mean(hc * hc, axis=-1, keepdims=True)
    xn = hc * lax.rsqrt(var + LN_EPS) * gam_ref[...] + bet_ref[...]
    xo_ref[...] = xn
    u = xn * (1.0 + sc_ref[...]) + sh_ref[...]
    u_ref[...] = u.astype(u_ref.dtype)
    if ut_ref:
        ut_ref[0][...] = u.T.astype(u_ref.dtype)


def postnorm(x, y, mod_g, g_blk, gamma, beta, mod_n, sc_blk, sh_blk, T, tr=256, transposed=False):
    N, D = x.shape
    row = pl.BlockSpec((tr, D), lambda i: (i, 0))
    vec = pl.BlockSpec((1, D), lambda i: (0, 0))

    def mod(blk):
        return pl.BlockSpec((None, 1, D), lambda i: (i * tr // T, 0, blk))

    out_specs = [row, row]
    out_shape = [jax.ShapeDtypeStruct((N, D), F32), jax.ShapeDtypeStruct((N, D), BF16)]
    if transposed:
        out_specs.append(pl.BlockSpec((D, tr), lambda i: (0, i)))
        out_shape.append(jax.ShapeDtypeStruct((D, N), BF16))
    return pl.pallas_call(
        _postnorm_kernel,
        grid=(N // tr,),
        in_specs=[row, row, mod(g_blk), vec, vec, mod(sc_blk), mod(sh_blk)],
        out_specs=out_specs,
        out_shape=out_shape,
        compiler_params=_cp("parallel"),
        name="postnorm",
    )(x, y, mod_g, gamma, beta, mod_n, mod_n)


def _topk_rows(s, k, break_ties):
    rows, tn = s.shape
    iota = lax.broadcasted_iota(jnp.int32, (rows, tn), 0)
    rank = jnp.full((rows, tn), k, jnp.int32)
    cur = s
    tops = []
    for r in range(k):
        m = jnp.max(cur, axis=0, keepdims=True)
        hit = cur == m
        if break_ties:
            hit = iota == jnp.min(jnp.where(hit, iota, rows), axis=0, keepdims=True)
        rank = jnp.where(hit, r, rank)
        cur = jnp.where(hit, -jnp.inf, cur)
        tops.append(m)
    return rank, tops


def _merge_topk(top1, top2, k):
    t1 = _stack_rows(top1)
    tn = t1.shape[1]
    iota = lax.broadcasted_iota(jnp.int32, (k, tn), 0)
    taken = jnp.zeros((k, tn), jnp.int32)
    front = t1 + top2[0]
    m0 = top1[0] + top2[0]
    z = jnp.zeros((1, tn), F32)
    for _ in range(k):
        m = jnp.max(front, axis=0, keepdims=True)
        hit = iota == jnp.min(jnp.where(front == m, iota, k), axis=0, keepdims=True)
        z = z + jnp.exp(m - m0)
        nxt = jnp.sum(jnp.where(hit, taken, 0), axis=0, keepdims=True) + 1
        follower = jnp.full((1, tn), -jnp.inf, F32)
        for rb in range(1, k):
            follower = jnp.where(nxt == rb, top2[rb], follower)
        t1_hit = jnp.sum(jnp.where(hit, t1, 0.0), axis=0, keepdims=True)
        taken = taken + hit.astype(jnp.int32)
        front = jnp.where(hit, t1_hit + follower, front)
    return taken.astype(F32), z


def _stack_rows(rows_list):
    n = len(rows_list)
    tn = rows_list[0].shape[1]
    iota = lax.broadcasted_iota(jnp.int32, (n, tn), 0)
    out = jnp.zeros((n, tn), rows_list[0].dtype)
    for r, row in enumerate(rows_list):
        out = jnp.where(iota == r, row, out)
    return out


def _peer_select_kernel(q_ref, sk_ref, e1_ref, l1_ref, e2_ref, r2_ref, *, cpt):
    K = PEER_TOPK
    half = PEER_QDIM // 2
    s1 = _dot_nt(sk_ref[0], q_ref[:, :half])
    s2 = _dot_nt(sk_ref[1], q_ref[:, half:])

    def emit(break_ties):
        rank1, top1 = _topk_rows(s1, K, break_ties)
        rank2, top2 = _topk_rows(s2, K, break_ties)
        row_len, z = _merge_topk(top1, top2, K)
        l1 = jnp.zeros(s1.shape, F32)
        for ra in range(K):
            l1 = jnp.where(rank1 == ra, row_len[ra:ra + 1], l1)
        e1 = jnp.where(rank1 < K, jnp.exp(s1 - top1[0]) / z, 0.0)
        e2 = jnp.where(rank2 < K, jnp.exp(s2 - top2[0]), 0.0)
        for j in range(PEER_NKEYS // cpt):
            e1_ref[j] = e1[j * cpt:(j + 1) * cpt]
            l1_ref[j] = l1[j * cpt:(j + 1) * cpt]
        e2_ref[...] = e2
        r2_ref[...] = rank2.astype(F32)
        members = (rank1 < K).astype(F32) + (rank2 < K).astype(F32)
        return jnp.sum(jnp.abs(jnp.sum(members, axis=0, keepdims=True) - 2.0 * K))

    tied = emit(False)

    @pl.when(tied != 0.0)
    def _():
        emit(True)


def peer_select(q, subkeys, *, cpt, tn=256):
    N = q.shape[0]
    H = PEER_HEADS
    nt = PEER_NKEYS // cpt
    s1 = pl.BlockSpec((nt, None, cpt, tn), lambda i, h: (0, h, 0, i))
    s2 = pl.BlockSpec((None, PEER_NKEYS, tn), lambda i, h: (h, 0, i))
    sh1 = jax.ShapeDtypeStruct((nt, H, cpt, N), F32)
    sh2 = jax.ShapeDtypeStruct((H, PEER_NKEYS, N), F32)
    return pl.pallas_call(
        functools.partial(_peer_select_kernel, cpt=cpt),
        grid=(N // tn, H),
        in_specs=[pl.BlockSpec((tn, PEER_QDIM), lambda i, h: (i, h)),
                  pl.BlockSpec((None, 2, PEER_NKEYS, PEER_QDIM // 2), lambda i, h: (h, 0, 0, 0))],
        out_specs=[s1, s1, s2, s2],
        out_shape=[sh1, sh1, sh2, sh2],
        compiler_params=_cp("parallel", "parallel"),
        name="peer_select",
    )(q, subkeys)


def _peer_dense_kernel(ut_ref, tu_ref, tv_ref, e1_ref, l1_ref, e2_ref, r2_ref, o_ref, a_s, *, cpt, tnc):
    @pl.when(pl.program_id(1) == 0)
    def _():
        o_ref[...] = jnp.zeros_like(o_ref)

    nk = PEER_NKEYS
    cph = cpt // 2
    tm = ut_ref.shape[1]
    for k in range(2):
        rows = slice(k * cph * nk, (k + 1) * cph * nk)
        for hc in range(2):
            tc = slice(hc * (tm // 2), (hc + 1) * (tm // 2))
            ht = jnp.dot(tu_ref[rows, :], ut_ref[:, tc], preferred_element_type=F32)
            for c2 in range(cph):
                cc = k * cph + c2
                w = None
                for h in range(PEER_HEADS):
                    hit = r2_ref[h, :, tc] < l1_ref[h, cc:cc + 1, tc]
                    wh = jnp.where(hit, e2_ref[h, :, tc], 0.0) * e1_ref[h, cc:cc + 1, tc]
                    w = wh if w is None else w + wh
                a_s[k, tc, c2 * nk:(c2 + 1) * nk] = (w * _gelu(ht[c2 * nk:(c2 + 1) * nk])).T.astype(BF16)
    for k in range(2):
        rows = slice(k * cph * nk, (k + 1) * cph * nk)
        for nt in range(o_ref.shape[1] // tnc):
            cs = slice(nt * tnc, (nt + 1) * tnc)
            o_ref[:, cs] += jnp.dot(a_s[k], tv_ref[rows, cs], preferred_element_type=F32)


def peer_dense(ut, tu, tv, layer, e1, l1, e2, r2, *, cpt, tm=512):
    D, N = ut.shape
    E = tu.shape[1]
    te = cpt * PEER_NKEYS
    H = PEER_HEADS
    once = pl.Buffered(1)
    s1 = pl.BlockSpec((None, H, cpt, tm), lambda i, j: (j, 0, 0, i))
    s2 = pl.BlockSpec((H, PEER_NKEYS, tm), lambda i, j: (0, 0, i), pipeline_mode=once)
    tab = pl.BlockSpec((None, te, D), lambda i, j: (layer, j, 0))
    return pl.pallas_call(
        functools.partial(_peer_dense_kernel, cpt=cpt, tnc=512),
        grid=(N // tm, E // te),
        in_specs=[pl.BlockSpec((D, tm), lambda i, j: (0, i), pipeline_mode=once), tab, tab, s1, s1, s2, s2],
        out_specs=pl.BlockSpec((tm, D), lambda i, j: (i, 0)),
        out_shape=jax.ShapeDtypeStruct((N, D), F32),
        scratch_shapes=[pltpu.VMEM((2, tm, te // 2), BF16)],
        compiler_params=_cp("parallel", "arbitrary"),
        name="peer_dense",
    )(ut, tu, tv, e1, l1, e2, r2)


def _hybrid_mixer(u, p, *, B, T):
    N = B * T
    sl = _alibi_slopes(ALIBI_HEADS)
    nsa_slopes = sl[0::2]
    dil_slopes = sl[1::2].reshape(DIL_GROUPS, DIL_HPG)

    lyr = p["layer"]
    proj_a = matmul(u, p["w_in"], BF16, layer=lyr, n_cols=NSA_A_W, tm=512, tn=768)
    small = small_projection(u, p["w_in"], lyr)
    proj_fox = matmul(u, p["w_in"], BF16, layer=lyr, col0=OFF_FOX, n_cols=3 * FOX_W, tm=512, tn=768)
    proj_dil = matmul(u, p["w_in"], BF16, layer=lyr, col0=OFF_DIL, n_cols=3 * DIL_W, tm=512, tn=768)

    nch = T // CMP_STRIDE
    kv_cmp = proj_a[:, OFF_NSA_KV:OFF_NSA_KV + 2 * NSA_KV_W].reshape(B, T, 2, NSA_KV_HEADS, HEAD_DIM)
    chunks = kv_cmp.transpose(2, 0, 3, 1, 4).reshape(2, B * NSA_KV_HEADS, nch, CMP_STRIDE * HEAD_DIM)
    kvc = compress_blocks(chunks, p["cmp_pe"], p["cmp_w1"], p["cmp_w2"])
    smat = jnp.asarray(_pool_matrix(nch, T // SLC_BLK), BF16)
    o_cmp, sel = cmp_attention(proj_a, kvc, smat, nsa_slopes, B=B, T=T)
    qh = NSA_HEADS
    kvh = NSA_KV_HEADS
    assert NSA_GROUP == HEADS_PER_STEP and DIL_HPG == HEADS_PER_STEP
    nsa_kw = dict(B=B, Ts=T, n_sub=1, n_head_blocks=kvh, shared_kv=True,
                  q_col=lambda h, r: h, tq=128, tk=256)
    o_slc = alibi_attention(proj_a, proj_a, proj_a, nsa_slopes, max_dist=None, sel=sel,
                            k_col=lambda h, r: qh + 2 * kvh + h,
                            v_col=lambda h, r: qh + 3 * kvh + h, **nsa_kw)
    o_win = alibi_attention(proj_a, proj_a, proj_a, nsa_slopes, max_dist=WIN - 1,
                            k_col=lambda h, r: qh + 4 * kvh + h,
                            v_col=lambda h, r: qh + 5 * kvh + h, **nsa_kw)
    o_a = nsa_merge(o_cmp, o_slc, o_win, small)

    f_tok = forget_cumsum(small, p["b_forget_row"], B=B, T=T)
    tkf = min(256, T)
    f_rows = f_tok[:, FORGET_LANE0:FORGET_LANE0 + FOX_HEADS].reshape(B, T, FOX_HEADS)
    f_rows = f_rows.transpose(0, 2, 1).reshape(B * FOX_HEADS, T // tkf, tkf)
    o_b = fox_attention(proj_fox, f_tok, f_rows, B=B, T=T, tq=min(128, T), tk=tkf)

    outs, lses = [], []
    ncb = 3 * DIL_GROUPS
    for g, (window, dil) in enumerate(DIL_PAIRS):
        Ts = T // dil
        view = proj_dil.reshape(B * Ts, dil * 3 * DIL_W)
        o, lse = alibi_attention(
            view, view, view, dil_slopes[g] * np.float32(dil), B=B, Ts=Ts, n_sub=dil, n_head_blocks=1,
            shared_kv=False, max_dist=window // dil, tq=128, tk=128, with_lse=True,
            q_col=lambda h, r, g=g: r * ncb + g,
            k_col=lambda h, r, g=g: r * ncb + DIL_GROUPS + g,
            v_col=lambda h, r, g=g: r * ncb + 2 * DIL_GROUPS + g)
        outs.append(o.reshape(N, OUT_C))
        lses.append(lse.reshape(N, OUT_C))
    o_c = dil_merge(outs, lses)

    z = gated_branches(u, p["w_gate"], p["b_gate"], lyr, o_a, o_b, o_c, p["wb_a"], p["wb_b"], p["wb_c"])
    return matmul(z, p["w_out"], F32, layer=lyr)


def _peer_ffn(u, ut, p):
    cpt = 4
    q = matmul(u, p["peer_wq"], BF16, layer=p["layer"])
    e1, l1, e2, r2 = peer_select(q, p["peer_subkeys"], cpt=cpt)
    return peer_dense(ut, p["peer_u"], p["peer_v"], p["layer"], e1, l1, e2, r2, cpt=cpt)


def _layer_params(l, shared, b_forget, cmp_pe, cmp_w1, cmp_w2, w_branch, peer_subkeys):
    b_row = jnp.zeros((1, LANES), F32).at[0, FORGET_LANE0:FORGET_LANE0 + FOX_HEADS].set(b_forget[l])
    half = CMP_LEN // 2
    return dict(
        shared,
        layer=l,
        b_forget_row=b_row,
        cmp_pe=cmp_pe[l].reshape(2, 2, 1, half * HEAD_DIM),
        cmp_w1=cmp_w1[l].astype(BF16).reshape(2, 2, half * HEAD_DIM, HEAD_DIM),
        cmp_w2=cmp_w2[l].astype(BF16),
        wb_a=w_branch[l, :OUT_A].astype(BF16),
        wb_b=w_branch[l, OUT_A:OUT_A + OUT_B].astype(BF16),
        wb_c=w_branch[l, OUT_A + OUT_B:].astype(BF16),
        peer_subkeys=peer_subkeys[l].astype(BF16),
    )


def kernel(x, c, w_ada, b_ada, w_in, b_forget, cmp_pe, cmp_w1, cmp_w2, w_branch, w_gate, b_gate, w_out,
           ln1_g, ln1_b, peer_wq, peer_subkeys, peer_u, peer_v, ln2_g, ln2_b):
    B, T, D = x.shape
    L = w_ada.shape[0]
    N = B * T
    c_pad = jnp.zeros((8, D), BF16).at[:B].set(c.astype(BF16))
    mod = ada_project(c_pad, w_ada, b_ada.reshape(L, 1, 6 * D))
    mods = [mod[l].reshape(8, 1, 6 * D) for l in range(L)]
    xf = x.reshape(N, D)
    u = modulate(xf, mods[0], 1, 0, T)
    shared = dict(w_in=w_in, w_out=w_out, peer_wq=peer_wq,
                  w_gate=w_gate.astype(BF16), b_gate=b_gate.reshape(L, 1, 3 * D),
                  peer_u=peer_u.astype(BF16), peer_v=peer_v.astype(BF16))
    for l in range(L):
        p = _layer_params(l, shared, b_forget, cmp_pe, cmp_w1, cmp_w2, w_branch, peer_subkeys)
        y = _hybrid_mixer(u, p, B=B, T=T)
        xf, u, ut = postnorm(xf, y, mods[l], 2, ln1_g[l].reshape(1, D), ln1_b[l].reshape(1, D),
                             mods[l], 4, 3, T, transposed=True)
        y = _peer_ffn(u, ut, p)
        nxt = mods[min(l + 1, L - 1)]
        xf, u = postnorm(xf, y, mods[l], 5, ln2_g[l].reshape(1, D), ln2_b[l].reshape(1, D),
                         nxt, 1, 0, T)
    return xf.reshape(B, T, D)
```

```python
import functools

import numpy as np
import jax
import jax.numpy as jnp
from jax import lax
from jax.experimental import pallas as pl
from jax.experimental.pallas import tpu as pltpu

F32 = jnp.float32
BF16 = jnp.bfloat16

D_MODEL = 4096
HEAD_DIM = 128
NSA_HEADS = 12
NSA_KV_HEADS = 3
NSA_GROUP = NSA_HEADS // NSA_KV_HEADS
CMP_LEN = 32
CMP_STRIDE = 16
SLC_BLK = 64
N_SEL = 16
WIN = 512
FOX_HEADS = 8
DIL_PAIRS = ((128, 1), (512, 4), (2048, 16))
DIL_GROUPS = 3
DIL_HPG = 4
DIL_HEADS = DIL_GROUPS * DIL_HPG
ALIBI_HEADS = NSA_HEADS + DIL_HEADS
NSA_Q_W = NSA_HEADS * HEAD_DIM
NSA_KV_W = NSA_KV_HEADS * HEAD_DIM
FOX_W = FOX_HEADS * HEAD_DIM
DIL_W = DIL_HEADS * HEAD_DIM
OFF_NSA_KV = NSA_Q_W
OFF_NSA_GATE = OFF_NSA_KV + 6 * NSA_KV_W
OFF_FOX = OFF_NSA_GATE + 3 * NSA_HEADS
OFF_FOX_F = OFF_FOX + 3 * FOX_W
OFF_DIL = OFF_FOX_F + FOX_HEADS
IN_COLS = OFF_DIL + 3 * DIL_W
NSA_A_W = NSA_Q_W + 6 * NSA_KV_W
OUT_A = NSA_Q_W
OUT_B = FOX_W
OUT_C = DIL_HPG * HEAD_DIM
PEER_HEADS = 8
PEER_NKEYS = 128
PEER_EXPERTS = PEER_NKEYS ** 2
PEER_TOPK = 16
PEER_QDIM = 256
DEPTH = 2
ALPHA = (2.0 * DEPTH) ** 0.25
LN_EPS = 1e-5
NEG = -1e30
BIG = 1e30
SCALE = HEAD_DIM ** -0.5

LANES = 128
GATE_LANE0 = 0
FORGET_LANE0 = 3 * NSA_HEADS
VMEM_LIMIT = 56 * 1024 * 1024


def _cp(*sem):
    return pltpu.CompilerParams(dimension_semantics=sem, vmem_limit_bytes=VMEM_LIMIT)


def _alibi_slopes(n):
    return (np.float32(2.0) ** (-8.0 * np.arange(1, n + 1, dtype=np.float32) / n)).astype(np.float32)


def _gelu(x):
    return 0.5 * x * (1.0 + jnp.tanh(np.float32(np.sqrt(2.0 / np.pi)) * (x + np.float32(0.044715) * (x * x * x))))


def _split3(x):
    hi = x.astype(BF16)
    r1 = x - hi.astype(F32)
    mid = r1.astype(BF16)
    lo = (r1 - mid.astype(F32)).astype(BF16)
    return hi, mid, lo


def _dot3(x, m):
    hi, mid, lo = _split3(x)
    d = functools.partial(jnp.dot, preferred_element_type=F32)
    return (d(lo, m) + d(mid, m)) + d(hi, m)


def _dot_nt(a, b):
    return lax.dot_general(a, b, (((1,), (1,)), ((), ())), preferred_element_type=F32)


def _ada_kernel(c_ref, w_ref, b_ref, o_ref):
    o_ref[...] = jnp.dot(c_ref[...], w_ref[...].astype(BF16), preferred_element_type=F32) + b_ref[...]


def ada_project(c_pad, w_ada, b_ada, tn=512):
    L, D, N6 = w_ada.shape
    return pl.pallas_call(
        _ada_kernel,
        grid=(L, N6 // tn),
        in_specs=[pl.BlockSpec((8, D), lambda l, j: (0, 0)),
                  pl.BlockSpec((None, D, tn), lambda l, j: (l, 0, j)),
                  pl.BlockSpec((None, 1, tn), lambda l, j: (l, 0, j))],
        out_specs=pl.BlockSpec((None, 8, tn), lambda l, j: (l, 0, j)),
        out_shape=jax.ShapeDtypeStruct((L, 8, N6), F32),
        compiler_params=_cp("parallel", "parallel"),
        name="ada_project",
    )(c_pad, w_ada, b_ada)


def _modulate_kernel(x_ref, sc_ref, sh_ref, u_ref):
    u_ref[...] = (x_ref[...] * (1.0 + sc_ref[...]) + sh_ref[...]).astype(u_ref.dtype)


def modulate(x, mod3, sc_blk, sh_blk, T, tr=256):
    N, D = x.shape
    return pl.pallas_call(
        _modulate_kernel,
        grid=(N // tr,),
        in_specs=[pl.BlockSpec((tr, D), lambda i: (i, 0)),
                  pl.BlockSpec((None, 1, D), lambda i: (i * tr // T, 0, sc_blk)),
                  pl.BlockSpec((None, 1, D), lambda i: (i * tr // T, 0, sh_blk))],
        out_specs=pl.BlockSpec((tr, D), lambda i: (i, 0)),
        out_shape=jax.ShapeDtypeStruct((N, D), BF16),
        compiler_params=_cp("parallel"),
        name="modulate",
    )(x, mod3, mod3)


def _mm_kernel(a_ref, w_ref, *rest, shift):
    if shift:
        x_ref, o_ref, wb_ref = rest
    else:
        o_ref, wb_ref = rest

    @pl.when(pl.program_id(1) == 0)
    def _():
        if not shift:
            wb_ref[...] = w_ref[...].astype(BF16)
            return
        nb = w_ref.shape[1] // LANES
        lane = lax.broadcasted_iota(jnp.int32, (w_ref.shape[0], LANES), 1)
        rolled = [pltpu.roll(w_ref[:, b * LANES:(b + 1) * LANES], LANES - shift, 1) for b in range(nb)]
        rolled.append(pltpu.roll(x_ref[...], LANES - shift, 1))
        for b in range(nb):
            wb_ref[:, b * LANES:(b + 1) * LANES] = jnp.where(
                lane < LANES - shift, rolled[b], rolled[b + 1]).astype(BF16)

    o_ref[...] = jnp.dot(a_ref[...], wb_ref[...], preferred_element_type=F32).astype(o_ref.dtype)


def matmul(a, w, out_dtype, layer=0, col0=0, n_cols=None, tm=1024, tn=512):
    M, K = a.shape
    if w.ndim == 2:
        w = w[None]
    N = w.shape[2] - col0 if n_cols is None else n_cols
    tm = min(tm, M)
    tn = min(tn, N)
    shift = col0 % LANES
    base = col0 - shift
    assert M % tm == 0 and N % tn == 0 and base % tn == 0
    in_specs = [pl.BlockSpec((tm, K), lambda j, i: (i, 0)),
                pl.BlockSpec((None, K, tn), lambda j, i: (layer, 0, base // tn + j))]
    args = [a, w]
    if shift:
        in_specs.append(pl.BlockSpec((None, K, LANES), lambda j, i: (layer, 0, (base + (j + 1) * tn) // LANES)))
        args.append(w)
    return pl.pallas_call(
        functools.partial(_mm_kernel, shift=shift),
        grid=(N // tn, M // tm),
        in_specs=in_specs,
        out_specs=pl.BlockSpec((tm, tn), lambda j, i: (i, j)),
        out_shape=jax.ShapeDtypeStruct((M, N), out_dtype),
        scratch_shapes=[pltpu.VMEM((K, tn), BF16)],
        compiler_params=_cp("parallel", "arbitrary"),
        name="matmul",
    )(*args)


def _small_proj_kernel(a_ref, wg_ref, wf_ref, o_ref):
    a = a_ref[...]
    lane = lax.broadcasted_iota(jnp.int32, o_ref.shape, 1)
    o_ref[...] = jnp.where(lane < FORGET_LANE0,
                           jnp.dot(a, wg_ref[...].astype(BF16), preferred_element_type=F32),
                           jnp.dot(a, wf_ref[...].astype(BF16), preferred_element_type=F32))


def small_projection(a, w_in, layer, tm=1024):
    M, K = a.shape
    assert OFF_NSA_GATE % LANES == GATE_LANE0 and OFF_FOX_F % LANES == FORGET_LANE0
    return pl.pallas_call(
        _small_proj_kernel,
        grid=(M // tm,),
        in_specs=[pl.BlockSpec((tm, K), lambda i: (i, 0)),
                  pl.BlockSpec((None, K, LANES), lambda i: (layer, 0, OFF_NSA_GATE // LANES)),
                  pl.BlockSpec((None, K, LANES), lambda i: (layer, 0, OFF_FOX_F // LANES))],
        out_specs=pl.BlockSpec((tm, LANES), lambda i: (i, 0)),
        out_shape=jax.ShapeDtypeStruct((M, LANES), F32),
        compiler_params=_cp("parallel"),
        name="small_projection",
    )(a, w_in, w_in)


HEADS_PER_STEP = 4


def _head_cols(i):
    return slice(i * HEAD_DIM, (i + 1) * HEAD_DIM)


def _flash(qs, load_kv, lo, hi, chunk_fn, score_fns):
    tq = qs[0].shape[0]

    def body(c, carry):
        shared = chunk_fn(c)
        kvs = [load_kv(c, i) for i in range(len(qs))]
        raws = [_dot_nt(q, kvs[i][0]) for i, q in enumerate(qs)]
        mids = []
        for i, raw in enumerate(raws):
            m, l, _ = carry[i]
            s = score_fns[i](raw, c, shared)
            m_new = jnp.maximum(m, jnp.max(s, axis=1, keepdims=True))
            alpha = jnp.exp(m - m_new)
            p = jnp.exp(s - m_new)
            mids.append((m_new, alpha, alpha * l + jnp.sum(p, axis=1, keepdims=True), p))
        out = []
        for i, (m_new, alpha, l, p) in enumerate(mids):
            v = kvs[i][1]
            acc = alpha * carry[i][2] + jnp.dot(p.astype(v.dtype), v, preferred_element_type=F32)
            out.append((m_new, l, acc))
        return tuple(out)

    one = (jnp.full((tq, 1), NEG, F32), jnp.zeros((tq, 1), F32), jnp.zeros((tq, HEAD_DIM), F32))
    return lax.fori_loop(lo, hi, body, tuple(one for _ in qs))


def _alibi_kernel(slope_ref, q_ref, k_ref, v_ref, *rest, tq, tk, max_dist, shared_kv, with_sel, with_lse):
    rest = list(rest)
    sel_ref = rest.pop(0) if with_sel else None
    o_ref = rest.pop(0)
    lse_ref = rest.pop(0) if with_lse else None
    hb = pl.program_id(1)
    q0 = pl.program_id(3) * tq
    nh = HEADS_PER_STEP
    rc = lax.broadcasted_iota(jnp.int32, (tq, tk), 0) - lax.broadcasted_iota(jnp.int32, (tq, tk), 1)
    if with_sel:
        sel = sel_ref[...]
        lane_blk = lax.broadcasted_iota(jnp.int32, (LANES, tk), 0)
        key_blk = lax.broadcasted_iota(jnp.int32, (LANES, tk), 1) // SLC_BLK

    def valid_fn(c):
        dist = rc + (q0 - c * tk)
        valid = dist >= 0
        if max_dist is not None:
            valid = valid & (dist <= max_dist)
        if with_sel:
            expand = (lane_blk == key_blk + (hb * (LANES // 4) + c * (tk // SLC_BLK))).astype(BF16)
            valid = valid & (jnp.dot(sel, expand, preferred_element_type=F32) > 0.5)
        return dist.astype(F32), valid

    def make_score(i):
        slope = slope_ref[hb * nh + i]

        def score_fn(s, c, shared):
            dist_f, valid = shared
            return jnp.where(valid, s * SCALE - slope * dist_f, NEG)
        return score_fn

    def load_kv(c, i):
        off = pl.multiple_of(c * tk, tk)
        cols = _head_cols(0 if shared_kv else i)
        return k_ref[pl.ds(off, tk), cols], v_ref[pl.ds(off, tk), cols]

    hi = (q0 + tq - 1) // tk + 1
    lo = 0 if max_dist is None else jnp.maximum(q0 - max_dist, 0) // tk
    res = _flash([q_ref[:, _head_cols(i)] for i in range(nh)], load_kv, lo, hi, valid_fn,
                 [make_score(i) for i in range(nh)])
    for i, (m, l, acc) in enumerate(res):
        o_ref[:, _head_cols(i)] = (acc / l).astype(o_ref.dtype)
        if with_lse:
            lse_ref[:, _head_cols(i)] = jnp.broadcast_to(m + jnp.log(l), (tq, HEAD_DIM))


def alibi_attention(qa, ka, va, slopes, *, B, Ts, n_sub, n_head_blocks, shared_kv, q_col, k_col, v_col,
                    max_dist, tq, tk, sel=None, with_lse=False):
    tq = min(tq, Ts)
    tk = min(tk, Ts)
    nq = Ts // tq
    rows = B * Ts
    wq = HEADS_PER_STEP * HEAD_DIM
    wkv = HEAD_DIM if shared_kv else wq
    in_specs = [pl.BlockSpec((tq, wq), lambda b, h, r, i, s: (b * nq + i, q_col(h, r))),
                pl.BlockSpec((Ts, wkv), lambda b, h, r, i, s: (b, k_col(h, r))),
                pl.BlockSpec((Ts, wkv), lambda b, h, r, i, s: (b, v_col(h, r)))]
    args = [qa, ka, va]
    if sel is not None:
        in_specs.append(pl.BlockSpec((tq, LANES), lambda b, h, r, i, s: (b * nq + i, 0)))
        args.append(sel)
    o_spec = pl.BlockSpec((tq, wq), lambda b, h, r, i, s: (b * nq + i, r * n_head_blocks + h))
    o_shape = jax.ShapeDtypeStruct((rows, n_sub * n_head_blocks * wq), F32)
    kern = functools.partial(_alibi_kernel, tq=tq, tk=tk, max_dist=max_dist, shared_kv=shared_kv,
                             with_sel=sel is not None, with_lse=with_lse)
    return pl.pallas_call(
        kern,
        grid_spec=pltpu.PrefetchScalarGridSpec(
            num_scalar_prefetch=1, grid=(B, n_head_blocks, n_sub, nq),
            in_specs=in_specs,
            out_specs=[o_spec, o_spec] if with_lse else o_spec),
        out_shape=[o_shape, o_shape] if with_lse else o_shape,
        compiler_params=_cp("parallel", "parallel", "parallel", "parallel"),
        name="alibi_attention",
    )(jnp.asarray(slopes, F32), *args)


def _fox_kernel(q_ref, k_ref, v_ref, fq_ref, fk_ref, o_ref, *, tq, tk):
    hb = pl.program_id(1)
    q0 = pl.program_id(2) * tq
    nh = HEADS_PER_STEP
    lane = lax.broadcasted_iota(jnp.int32, (tq, LANES), 1)
    f_tok = fq_ref[...]
    rc = lax.broadcasted_iota(jnp.int32, (tq, tk), 0) - lax.broadcasted_iota(jnp.int32, (tq, tk), 1)

    def make_score(i):
        fq = jnp.sum(jnp.where(lane == FORGET_LANE0 + hb * nh + i, f_tok, 0.0), axis=1, keepdims=True)

        def score_fn(s, c, causal):
            return jnp.where(causal, (s * SCALE + fq) - fk_ref[i, pl.ds(c, 1), :], NEG)
        return score_fn

    def load_kv(c, i):
        off = pl.multiple_of(c * tk, tk)
        return k_ref[pl.ds(off, tk), _head_cols(i)], v_ref[pl.ds(off, tk), _head_cols(i)]

    hi = (q0 + tq - 1) // tk + 1
    res = _flash([q_ref[:, _head_cols(i)] for i in range(nh)], load_kv, 0, hi,
                 lambda c: rc + (q0 - c * tk) >= 0, [make_score(i) for i in range(nh)])
    for i, (_, l, acc) in enumerate(res):
        o_ref[:, _head_cols(i)] = (acc / l).astype(o_ref.dtype)


def fox_attention(proj_fox, f_tok, f_rows, *, B, T, tq=256, tk=256):
    nq = T // tq
    nh = HEADS_PER_STEP
    nhb = FOX_HEADS // nh
    wq = nh * HEAD_DIM
    return pl.pallas_call(
        functools.partial(_fox_kernel, tq=tq, tk=tk),
        grid=(B, nhb, nq),
        in_specs=[pl.BlockSpec((tq, wq), lambda b, h, i: (b * nq + i, h)),
                  pl.BlockSpec((T, wq), lambda b, h, i: (b, nhb + h)),
                  pl.BlockSpec((T, wq), lambda b, h, i: (b, 2 * nhb + h)),
                  pl.BlockSpec((tq, LANES), lambda b, h, i: (b * nq + i, 0)),
                  pl.BlockSpec((nh, T // tk, tk), lambda b, h, i: (b * nhb + h, 0, 0))],
        out_specs=pl.BlockSpec((tq, wq), lambda b, h, i: (b * nq + i, h)),
        out_shape=jax.ShapeDtypeStruct((B * T, FOX_W), BF16),
        compiler_params=_cp("parallel", "parallel", "parallel"),
        name="fox_attention",
    )(proj_fox, proj_fox, proj_fox, f_tok, f_rows)


def _cumsum_kernel(x_ref, b_ref, o_ref, *, T, tr):
    i = pl.program_id(1)
    z = x_ref[...] + b_ref[...]
    logf = jnp.minimum(z, 0.0) - jnp.log1p(jnp.exp(-jnp.abs(z)))
    row = lax.broadcasted_iota(jnp.int32, (tr, T), 0) + i * tr
    col = lax.broadcasted_iota(jnp.int32, (tr, T), 1)
    o_ref[...] = _dot3_lhs((row >= col).astype(BF16), logf)


def _dot3_lhs(m, x):
    hi, mid, lo = _split3(x)
    d = functools.partial(jnp.dot, preferred_element_type=F32)
    return (d(m, lo) + d(m, mid)) + d(m, hi)


def forget_cumsum(small, bias_row, *, B, T, tr=256):
    return pl.pallas_call(
        functools.partial(_cumsum_kernel, T=T, tr=tr),
        grid=(B, T // tr),
        in_specs=[pl.BlockSpec((T, LANES), lambda b, i: (b, 0)),
                  pl.BlockSpec((1, LANES), lambda b, i: (0, 0))],
        out_specs=pl.BlockSpec((tr, LANES), lambda b, i: (b * (T // tr) + i, 0)),
        out_shape=jax.ShapeDtypeStruct((B * T, LANES), F32),
        compiler_params=_cp("parallel", "parallel"),
        name="forget_cumsum",
    )(small, bias_row)


def _compress_kernel(x_ref, pe_ref, w1_ref, w2_ref, o_ref):
    x = x_ref[...].astype(F32)
    y0 = jnp.dot((x + pe_ref[0]).astype(BF16), w1_ref[0], preferred_element_type=F32)
    y1 = jnp.dot((x + pe_ref[1]).astype(BF16), w1_ref[1], preferred_element_type=F32)
    n = y1.shape[0]
    hmid = _gelu(y0 + pltpu.roll(y1, n - 1, 0))
    o_ref[...] = jnp.dot(hmid.astype(BF16), w2_ref[...], preferred_element_type=F32).astype(o_ref.dtype)


def compress_blocks(chunks, pe, w1, w2):
    _, G, nch, cw = chunks.shape
    return pl.pallas_call(
        _compress_kernel,
        grid=(2, G),
        in_specs=[pl.BlockSpec((None, None, nch, cw), lambda a, g: (a, g, 0, 0)),
                  pl.BlockSpec((None, 2, 1, cw), lambda a, g: (a, 0, 0, 0)),
                  pl.BlockSpec((None, 2, cw, HEAD_DIM), lambda a, g: (a, 0, 0, 0)),
                  pl.BlockSpec((None, HEAD_DIM, HEAD_DIM), lambda a, g: (a, 0, 0))],
        out_specs=pl.BlockSpec((None, None, nch, HEAD_DIM), lambda a, g: (a, g, 0, 0)),
        out_shape=jax.ShapeDtypeStruct((2, G, nch, HEAD_DIM), BF16),
        compiler_params=_cp("parallel", "parallel"),
        name="compress_blocks",
    )(chunks, pe, w1, w2)


def _cmp_kernel(slope_ref, q_ref, kc_ref, vc_ref, smat_ref, o_ref, sel_ref, *, tq, n_cmp, n_slc):
    q0 = pl.program_id(1) * tq
    nk = kc_ref.shape[1]
    pos = lax.broadcasted_iota(jnp.int32, (tq, nk), 0) + q0
    n_idx = lax.broadcasted_iota(jnp.int32, (tq, nk), 1)
    dist_c = pos - (n_idx * CMP_STRIDE + (CMP_LEN - 1))
    valid_c = (dist_c >= 0) & (n_idx < n_cmp)
    dist_f = dist_c.astype(F32)
    p_slc = jnp.zeros((tq, LANES), F32)
    for kv in range(NSA_KV_HEADS):
        kc = kc_ref[kv]
        vc = vc_ref[kv]
        imp = jnp.zeros((tq, nk), F32)
        for g in range(NSA_GROUP):
            hh = kv * NSA_GROUP + g
            q = q_ref[:, hh * HEAD_DIM:(hh + 1) * HEAD_DIM]
            s = _dot_nt(q, kc) * SCALE - slope_ref[hh] * dist_f
            s = jnp.where(valid_c, s, NEG)
            e = jnp.where(valid_c, jnp.exp(s - jnp.max(s, axis=1, keepdims=True)), 0.0)
            p = e / jnp.maximum(jnp.sum(e, axis=1, keepdims=True), 1e-30)
            o_ref[:, hh * HEAD_DIM:(hh + 1) * HEAD_DIM] = jnp.dot(
                p.astype(BF16), vc, preferred_element_type=F32)
            imp = imp + p
        p_slc = p_slc + _dot3(imp, smat_ref[kv])
    grp = LANES // 4
    p_t = p_slc.T
    j = lax.broadcasted_iota(jnp.int32, (grp, tq), 0)
    qblk = (lax.broadcasted_iota(jnp.int32, (grp, tq), 1) + q0) // SLC_BLK
    forced = (j == 0) | (j == qblk) | (j == qblk - 1)
    causal = j <= qblk
    chosen = []
    for kv in range(NSA_KV_HEADS):
        score = jnp.where(forced, BIG, jnp.where(causal, p_t[kv * grp:(kv + 1) * grp], NEG))
        rank = jnp.zeros((grp, tq), jnp.int32)
        for k in range(n_slc):
            row = score[k:k + 1]
            ahead = (row > score) | ((row == score) & (k < j))
            rank = rank + ahead.astype(jnp.int32)
        chosen.append(jnp.where(causal & (rank < min(N_SEL, n_slc)), 1.0, 0.0))
    chosen.append(jnp.zeros((LANES - NSA_KV_HEADS * grp, tq), F32))
    sel_ref[...] = jnp.concatenate(chosen, axis=0).T.astype(sel_ref.dtype)


def cmp_attention(proj_a, kvc, smat, slopes, *, B, T, tq=256):
    nq = T // tq
    nch = kvc.shape[2]
    n_cmp = (T - CMP_LEN) // CMP_STRIDE + 1
    n_slc = T // SLC_BLK
    kvc4 = kvc.reshape(2, B, NSA_KV_HEADS, nch, HEAD_DIM)
    return pl.pallas_call(
        functools.partial(_cmp_kernel, tq=tq, n_cmp=n_cmp, n_slc=n_slc),
        grid_spec=pltpu.PrefetchScalarGridSpec(
            num_scalar_prefetch=1, grid=(B, nq),
            in_specs=[pl.BlockSpec((tq, NSA_Q_W), lambda b, i, s: (b * nq + i, 0)),
                      pl.BlockSpec((None, None, NSA_KV_HEADS, nch, HEAD_DIM), lambda b, i, s: (0, b, 0, 0, 0)),
                      pl.BlockSpec((None, None, NSA_KV_HEADS, nch, HEAD_DIM), lambda b, i, s: (1, b, 0, 0, 0)),
                      pl.BlockSpec((NSA_KV_HEADS, nch, LANES), lambda b, i, s: (0, 0, 0))],
            out_specs=[pl.BlockSpec((tq, NSA_Q_W), lambda b, i, s: (b * nq + i, 0)),
                       pl.BlockSpec((tq, LANES), lambda b, i, s: (b * nq + i, 0))]),
        out_shape=[jax.ShapeDtypeStruct((B * T, NSA_Q_W), F32),
                   jax.ShapeDtypeStruct((B * T, LANES), BF16)],
        compiler_params=_cp("parallel", "parallel"),
        name="cmp_attention",
    )(jnp.asarray(slopes, F32), proj_a, kvc4, kvc4, smat)


def _pool_matrix(nch, n_slc):
    r = SLC_BLK // CMP_STRIDE
    c_over = CMP_LEN // CMP_STRIDE
    n_cmp_max = nch
    m = np.zeros((NSA_KV_HEADS, nch, LANES), np.float32)
    for kv in range(NSA_KV_HEADS):
        for j in range(n_slc):
            for a in range(r):
                for b in range(c_over):
                    i = r * j + a + b
                    if i < n_cmp_max:
                        m[kv, i, kv * (LANES // 4) + j] += 1.0
    return m


def _nsa_merge_kernel(oc_ref, os_ref, ow_ref, g_ref, e_ref, o_ref):
    sig = jax.nn.sigmoid(g_ref[...])
    out = _dot3(sig, e_ref[0]) * oc_ref[...]
    out = out + _dot3(sig, e_ref[1]) * os_ref[...]
    out = out + _dot3(sig, e_ref[2]) * ow_ref[...]
    o_ref[...] = out.astype(o_ref.dtype)


def nsa_merge(o_cmp, o_slc, o_win, small, tr=256):
    N = o_cmp.shape[0]
    e = np.zeros((3, LANES, NSA_Q_W), np.float32)
    for br in range(3):
        for h in range(NSA_HEADS):
            e[br, GATE_LANE0 + h * 3 + br, h * HEAD_DIM:(h + 1) * HEAD_DIM] = 1.0
    row = pl.BlockSpec((tr, NSA_Q_W), lambda i: (i, 0))
    return pl.pallas_call(
        _nsa_merge_kernel,
        grid=(N // tr,),
        in_specs=[row, row, row,
                  pl.BlockSpec((tr, LANES), lambda i: (i, 0)),
                  pl.BlockSpec((3, LANES, NSA_Q_W), lambda i: (0, 0, 0))],
        out_specs=row,
        out_shape=jax.ShapeDtypeStruct((N, NSA_Q_W), BF16),
        compiler_params=_cp("parallel"),
        name="nsa_merge",
    )(o_cmp, o_slc, o_win, small, jnp.asarray(e, BF16))


def _dil_merge_kernel(o0, o1, o2, l0, l1, l2, o_ref):
    a, b, c = l0[...], l1[...], l2[...]
    m = jnp.maximum(jnp.maximum(a, b), c)
    ea, eb, ec = jnp.exp(a - m), jnp.exp(b - m), jnp.exp(c - m)
    z = ea + eb + ec
    o_ref[...] = ((ea / z) * o0[...] + (eb / z) * o1[...] + (ec / z) * o2[...]).astype(o_ref.dtype)


def dil_merge(outs, lses, tr=512):
    N, W = outs[0].shape
    row = pl.BlockSpec((tr, W), lambda i: (i, 0))
    return pl.pallas_call(
        _dil_merge_kernel,
        grid=(N // tr,),
        in_specs=[row] * 6,
        out_specs=row,
        out_shape=jax.ShapeDtypeStruct((N, W), BF16),
        compiler_params=_cp("parallel"),
        name="dil_merge",
    )(*outs, *lses)


def _branch_kernel(u_ref, wga, wgb, wgc, bga, bgb, bgc, oa, ob, oc, wa, wb, wc, z_ref):
    d = functools.partial(jnp.dot, preferred_element_type=F32)
    u = u_ref[...]
    z = jax.nn.sigmoid(d(u, wga[...]) + bga[...]) * d(oa[...], wa[...])
    z = z + jax.nn.sigmoid(d(u, wgb[...]) + bgb[...]) * d(ob[...], wb[...])
    z = z + jax.nn.sigmoid(d(u, wgc[...]) + bgc[...]) * d(oc[...], wc[...])
    z_ref[...] = z.astype(z_ref.dtype)


def gated_branches(u, w_gate, b_gate, layer, o_a, o_b, o_c, wb_a, wb_b, wb_c, tm=512, tn=512):
    N, D = u.shape
    nj = D // tn

    def wg(k):
        return pl.BlockSpec((None, D, tn), lambda j, i: (layer, 0, k * nj + j))

    def bg(k):
        return pl.BlockSpec((None, 1, tn), lambda j, i: (layer, 0, k * nj + j))

    def act(w):
        return pl.BlockSpec((tm, w), lambda j, i: (i, 0))

    def wup(w):
        return pl.BlockSpec((w, tn), lambda j, i: (0, j))

    return pl.pallas_call(
        _branch_kernel,
        grid=(nj, N // tm),
        in_specs=[act(D), wg(0), wg(1), wg(2), bg(0), bg(1), bg(2),
                  act(OUT_A), act(OUT_B), act(OUT_C), wup(OUT_A), wup(OUT_B), wup(OUT_C)],
        out_specs=pl.BlockSpec((tm, tn), lambda j, i: (i, j)),
        out_shape=jax.ShapeDtypeStruct((N, D), BF16),
        compiler_params=_cp("parallel", "parallel"),
        name="gated_branches",
    )(u, w_gate, w_gate, w_gate, b_gate, b_gate, b_gate, o_a, o_b, o_c, wb_a, wb_b, wb_c)


def _postnorm_kernel(x_ref, y_ref, g_ref, gam_ref, bet_ref, sc_ref, sh_ref, xo_ref, u_ref, *ut_ref):
    h = ALPHA * x_ref[...] + (1.0 + g_ref[...]) * y_ref[...]
    mu = jnp.mean(h, axis=-1, keepdims=True)
    hc = h - mu
    var = jnp.mean(hc * hc, axis=-1, keepdims=True)
    xn = hc * lax.rsqrt(var + LN_EPS) * gam_ref[...] + bet_ref[...]
    xo_ref[...] = xn
    u = xn * (1.0 + sc_ref[...]) + sh_ref[...]
    u_ref[...] = u.astype(u_ref.dtype)
    if ut_ref:
        ut_ref[0][...] = u.T.astype(u_ref.dtype)


def postnorm(x, y, mod_g, g_blk, gamma, beta, mod_n, sc_blk, sh_blk, T, tr=256, transposed=False):
    N, D = x.shape
    row = pl.BlockSpec((tr, D), lambda i: (i, 0))
    vec = pl.BlockSpec((1, D), lambda i: (0, 0))

    def mod(blk):
        return pl.BlockSpec((None, 1, D), lambda i: (i * tr // T, 0, blk))

    out_specs = [row, row]
    out_shape = [jax.ShapeDtypeStruct((N, D), F32), jax.ShapeDtypeStruct((N, D), BF16)]
    if transposed:
        out_specs.append(pl.BlockSpec((D, tr), lambda i: (0, i)))
        out_shape.append(jax.ShapeDtypeStruct((D, N), BF16))
    return pl.pallas_call(
        _postnorm_kernel,
        grid=(N // tr,),
        in_specs=[row, row, mod(g_blk), vec, vec, mod(sc_blk), mod(sh_blk)],
        out_specs=out_specs,
        out_shape=out_shape,
        compiler_params=_cp("parallel"),
        name="postnorm",
    )(x, y, mod_g, gamma, beta, mod_n, mod_n)


def _topk_rows(s, k, break_ties):
    rows, tn = s.shape
    iota = lax.broadcasted_iota(jnp.int32, (rows, tn), 0)
    rank = jnp.full((rows, tn), k, jnp.int32)
    cur = s
    tops = []
    for r in range(k):
        m = jnp.max(cur, axis=0, keepdims=True)
        hit = cur == m
        if break_ties:
            hit = iota == jnp.min(jnp.where(hit, iota, rows), axis=0, keepdims=True)
        rank = jnp.where(hit, r, rank)
        cur = jnp.where(hit, -jnp.inf, cur)
        tops.append(m)
    return rank, tops


def _merge_topk(top1, top2, k):
    t1 = _stack_rows(top1)
    tn = t1.shape[1]
    iota = lax.broadcasted_iota(jnp.int32, (k, tn), 0)
    taken = jnp.zeros((k, tn), jnp.int32)
    front = t1 + top2[0]
    m0 = top1[0] + top2[0]
    z = jnp.zeros((1, tn), F32)
    for _ in range(k):
        m = jnp.max(front, axis=0, keepdims=True)
        hit = iota == jnp.min(jnp.where(front == m, iota, k), axis=0, keepdims=True)
        z = z + jnp.exp(m - m0)
        nxt = jnp.sum(jnp.where(hit, taken, 0), axis=0, keepdims=True) + 1
        follower = jnp.full((1, tn), -jnp.inf, F32)
        for rb in range(1, k):
            follower = jnp.where(nxt == rb, top2[rb], follower)
        t1_hit = jnp.sum(jnp.where(hit, t1, 0.0), axis=0, keepdims=True)
        taken = taken + hit.astype(jnp.int32)
        front = jnp.where(hit, t1_hit + follower, front)
    return taken.astype(F32), z


def _stack_rows(rows_list):
    n = len(rows_list)
    tn = rows_list[0].shape[1]
    iota = lax.broadcasted_iota(jnp.int32, (n, tn), 0)
    out = jnp.zeros((n, tn), rows_list[0].dtype)
    for r, row in enumerate(rows_list):
        out = jnp.where(iota == r, row, out)
    return out


def _peer_select_kernel(q_ref, sk_ref, e1_ref, l1_ref, e2_ref, r2_ref, *, cpt):
    K = PEER_TOPK
    half = PEER_QDIM // 2
    s1 = _dot_nt(sk_ref[0], q_ref[:, :half])
    s2 = _dot_nt(sk_ref[1], q_ref[:, half:])

    def emit(break_ties):
        rank1, top1 = _topk_rows(s1, K, break_ties)
        rank2, top2 = _topk_rows(s2, K, break_ties)
        row_len, z = _merge_topk(top1, top2, K)
        l1 = jnp.zeros(s1.shape, F32)
        for ra in range(K):
            l1 = jnp.where(rank1 == ra, row_len[ra:ra + 1], l1)
        e1 = jnp.where(rank1 < K, jnp.exp(s1 - top1[0]) / z, 0.0)
        e2 = jnp.where(rank2 < K, jnp.exp(s2 - top2[0]), 0.0)
        for j in range(PEER_NKEYS // cpt):
            e1_ref[j] = e1[j * cpt:(j + 1) * cpt]
            l1_ref[j] = l1[j * cpt:(j + 1) * cpt]
        e2_ref[...] = e2.astype(BF16)
        r2_ref[...] = rank2.astype(BF16)
        members = (rank1 < K).astype(F32) + (rank2 < K).astype(F32)
        return jnp.sum(jnp.abs(jnp.sum(members, axis=0, keepdims=True) - 2.0 * K))

    tied = emit(False)

    @pl.when(tied != 0.0)
    def _():
        emit(True)


def peer_select(q, subkeys, *, cpt, tn=256):
    N = q.shape[0]
    H = PEER_HEADS
    nt = PEER_NKEYS // cpt
    s1 = pl.BlockSpec((nt, None, cpt, tn), lambda i, h: (0, h, 0, i))
    s2 = pl.BlockSpec((None, PEER_NKEYS, tn), lambda i, h: (h, 0, i))
    sh1 = jax.ShapeDtypeStruct((nt, H, cpt, N), F32)
    sh2 = jax.ShapeDtypeStruct((H, PEER_NKEYS, N), BF16)
    return pl.pallas_call(
        functools.partial(_peer_select_kernel, cpt=cpt),
        grid=(N // tn, H),
        in_specs=[pl.BlockSpec((tn, PEER_QDIM), lambda i, h: (i, h)),
                  pl.BlockSpec((None, 2, PEER_NKEYS, PEER_QDIM // 2), lambda i, h: (h, 0, 0, 0))],
        out_specs=[s1, s1, s2, s2],
        out_shape=[sh1, sh1, sh2, sh2],
        compiler_params=_cp("parallel", "parallel"),
        name="peer_select",
    )(q, subkeys)


def _peer_dense_kernel(ut_ref, tu_ref, tv_ref, e1_ref, l1_ref, e2_ref, r2_ref, o_ref, a_s, *, cpt, tnc):
    @pl.when(pl.program_id(1) == 0)
    def _():
        o_ref[...] = jnp.zeros_like(o_ref)

    nk = PEER_NKEYS
    cph = cpt // 2
    tm = ut_ref.shape[1]
    for k in range(2):
        rows = slice(k * cph * nk, (k + 1) * cph * nk)
        for hc in range(2):
            tc = slice(hc * (tm // 2), (hc + 1) * (tm // 2))
            ht = jnp.dot(tu_ref[rows, :], ut_ref[:, tc], preferred_element_type=F32)
            for c2 in range(cph):
                cc = k * cph + c2
                w = None
                for h in range(PEER_HEADS):
                    hit = r2_ref[h, :, tc] < l1_ref[h, cc:cc + 1, tc].astype(BF16)
                    wh = jnp.where(hit, e2_ref[h, :, tc], jnp.zeros((), BF16)) * e1_ref[h, cc:cc + 1, tc].astype(BF16)
                    w = wh if w is None else w + wh
                a_s[k, tc, c2 * nk:(c2 + 1) * nk] = (w.astype(F32) * _gelu(ht[c2 * nk:(c2 + 1) * nk])).T.astype(BF16)
    for k in range(2):
        rows = slice(k * cph * nk, (k + 1) * cph * nk)
        for nt in range(o_ref.shape[1] // tnc):
            cs = slice(nt * tnc, (nt + 1) * tnc)
            o_ref[:, cs] += jnp.dot(a_s[k], tv_ref[rows, cs], preferred_element_type=F32)


def peer_dense(ut, tu, tv, layer, e1, l1, e2, r2, *, cpt, tm=512):
    D, N = ut.shape
    E = tu.shape[1]
    te = cpt * PEER_NKEYS
    H = PEER_HEADS
    once = pl.Buffered(1)
    s1 = pl.BlockSpec((None, H, cpt, tm), lambda i, j: (j, 0, 0, i))
    s2 = pl.BlockSpec((H, PEER_NKEYS, tm), lambda i, j: (0, 0, i), pipeline_mode=once)
    tab = pl.BlockSpec((None, te, D), lambda i, j: (layer, j, 0))
    return pl.pallas_call(
        functools.partial(_peer_dense_kernel, cpt=cpt, tnc=512),
        grid=(N // tm, E // te),
        in_specs=[pl.BlockSpec((D, tm), lambda i, j: (0, i), pipeline_mode=once), tab, tab, s1, s1, s2, s2],
        out_specs=pl.BlockSpec((tm, D), lambda i, j: (i, 0)),
        out_shape=jax.ShapeDtypeStruct((N, D), F32),
        scratch_shapes=[pltpu.VMEM((2, tm, te // 2), BF16)],
        compiler_params=_cp("parallel", "arbitrary"),
        name="peer_dense",
    )(ut, tu, tv, e1, l1, e2, r2)


def _hybrid_mixer(u, p, *, B, T):
    N = B * T
    sl = _alibi_slopes(ALIBI_HEADS)
    nsa_slopes = sl[0::2]
    dil_slopes = sl[1::2].reshape(DIL_GROUPS, DIL_HPG)

    lyr = p["layer"]
    proj_a = matmul(u, p["w_in"], BF16, layer=lyr, n_cols=NSA_A_W, tm=512, tn=768)
    small = small_projection(u, p["w_in"], lyr)
    proj_fox = matmul(u, p["w_in"], BF16, layer=lyr, col0=OFF_FOX, n_cols=3 * FOX_W, tm=512, tn=768)
    proj_dil = matmul(u, p["w_in"], BF16, layer=lyr, col0=OFF_DIL, n_cols=3 * DIL_W, tm=512, tn=768)

    nch = T // CMP_STRIDE
    kv_cmp = proj_a[:, OFF_NSA_KV:OFF_NSA_KV + 2 * NSA_KV_W].reshape(B, T, 2, NSA_KV_HEADS, HEAD_DIM)
    chunks = kv_cmp.transpose(2, 0, 3, 1, 4).reshape(2, B * NSA_KV_HEADS, nch, CMP_STRIDE * HEAD_DIM)
    kvc = compress_blocks(chunks, p["cmp_pe"], p["cmp_w1"], p["cmp_w2"])
    smat = jnp.asarray(_pool_matrix(nch, T // SLC_BLK), BF16)
    o_cmp, sel = cmp_attention(proj_a, kvc, smat, nsa_slopes, B=B, T=T)
    qh = NSA_HEADS
    kvh = NSA_KV_HEADS
    assert NSA_GROUP == HEADS_PER_STEP and DIL_HPG == HEADS_PER_STEP
    nsa_kw = dict(B=B, Ts=T, n_sub=1, n_head_blocks=kvh, shared_kv=True,
                  q_col=lambda h, r: h, tq=128, tk=256)
    o_slc = alibi_attention(proj_a, proj_a, proj_a, nsa_slopes, max_dist=None, sel=sel,
                            k_col=lambda h, r: qh + 2 * kvh + h,
                            v_col=lambda h, r: qh + 3 * kvh + h, **nsa_kw)
    o_win = alibi_attention(proj_a, proj_a, proj_a, nsa_slopes, max_dist=WIN - 1,
                            k_col=lambda h, r: qh + 4 * kvh + h,
                            v_col=lambda h, r: qh + 5 * kvh + h, **nsa_kw)
    o_a = nsa_merge(o_cmp, o_slc, o_win, small)

    f_tok = forget_cumsum(small, p["b_forget_row"], B=B, T=T)
    tkf = min(256, T)
    f_rows = f_tok[:, FORGET_LANE0:FORGET_LANE0 + FOX_HEADS].reshape(B, T, FOX_HEADS)
    f_rows = f_rows.transpose(0, 2, 1).reshape(B * FOX_HEADS, T // tkf, tkf)
    o_b = fox_attention(proj_fox, f_tok, f_rows, B=B, T=T, tq=min(128, T), tk=tkf)

    outs, lses = [], []
    ncb = 3 * DIL_GROUPS
    for g, (window, dil) in enumerate(DIL_PAIRS):
        Ts = T // dil
        view = proj_dil.reshape(B * Ts, dil * 3 * DIL_W)
        o, lse = alibi_attention(
            view, view, view, dil_slopes[g] * np.float32(dil), B=B, Ts=Ts, n_sub=dil, n_head_blocks=1,
            shared_kv=False, max_dist=window // dil, tq=128, tk=128, with_lse=True,
            q_col=lambda h, r, g=g: r * ncb + g,
            k_col=lambda h, r, g=g: r * ncb + DIL_GROUPS + g,
            v_col=lambda h, r, g=g: r * ncb + 2 * DIL_GROUPS + g)
        outs.append(o.reshape(N, OUT_C))
        lses.append(lse.reshape(N, OUT_C))
    o_c = dil_merge(outs, lses)

    z = gated_branches(u, p["w_gate"], p["b_gate"], lyr, o_a, o_b, o_c, p["wb_a"], p["wb_b"], p["wb_c"])
    return matmul(z, p["w_out"], F32, layer=lyr)


def _peer_ffn(u, ut, p):
    cpt = 4
    q = matmul(u, p["peer_wq"], BF16, layer=p["layer"])
    e1, l1, e2, r2 = peer_select(q, p["peer_subkeys"], cpt=cpt)
    return peer_dense(ut, p["peer_u"], p["peer_v"], p["layer"], e1, l1, e2, r2, cpt=cpt)


def _layer_params(l, shared, b_forget, cmp_pe, cmp_w1, cmp_w2, w_branch, peer_subkeys):
    b_row = jnp.zeros((1, LANES), F32).at[0, FORGET_LANE0:FORGET_LANE0 + FOX_HEADS].set(b_forget[l])
    half = CMP_LEN // 2
    return dict(
        shared,
        layer=l,
        b_forget_row=b_row,
        cmp_pe=cmp_pe[l].reshape(2, 2, 1, half * HEAD_DIM),
        cmp_w1=cmp_w1[l].astype(BF16).reshape(2, 2, half * HEAD_DIM, HEAD_DIM),
        cmp_w2=cmp_w2[l].astype(BF16),
        wb_a=w_branch[l, :OUT_A].astype(BF16),
        wb_b=w_branch[l, OUT_A:OUT_A + OUT_B].astype(BF16),
        wb_c=w_branch[l, OUT_A + OUT_B:].astype(BF16),
        peer_subkeys=peer_subkeys[l].astype(BF16),
    )


def kernel(x, c, w_ada, b_ada, w_in, b_forget, cmp_pe, cmp_w1, cmp_w2, w_branch, w_gate, b_gate, w_out,
           ln1_g, ln1_b, peer_wq, peer_subkeys, peer_u, peer_v, ln2_g, ln2_b):
    B, T, D = x.shape
    L = w_ada.shape[0]
    N = B * T
    c_pad = jnp.zeros((8, D), BF16).at[:B].set(c.astype(BF16))
    mod = ada_project(c_pad, w_ada, b_ada.reshape(L, 1, 6 * D))
    mods = [mod[l].reshape(8, 1, 6 * D) for l in range(L)]
    xf = x.reshape(N, D)
    u = modulate(xf, mods[0], 1, 0, T)
    shared = dict(w_in=w_in, w_out=w_out, peer_wq=peer_wq,
                  w_gate=w_gate.astype(BF16), b_gate=b_gate.reshape(L, 1, 3 * D),
                  peer_u=peer_u.astype(BF16), peer_v=peer_v.astype(BF16))
    for l in range(L):
        p = _layer_params(l, shared, b_forget, cmp_pe, cmp_w1, cmp_w2, w_branch, peer_subkeys)
        y = _hybrid_mixer(u, p, B=B, T=T)
        xf, u, ut = postnorm(xf, y, mods[l], 2, ln1_g[l].reshape(1, D), ln1_b[l].reshape(1, D),
                             mods[l], 4, 3, T, transposed=True)
        y = _peer_ffn(u, ut, p)
        nxt = mods[min(l + 1, L - 1)]
        xf, u = postnorm(xf, y, mods[l], 5, ln2_g[l].reshape(1, D), ln2_b[l].reshape(1, D),
                         nxt, 1, 0, T)
    return xf.reshape(B, T, D)
```

```python
import functools

import numpy as np
import jax
import jax.numpy as jnp
from jax import lax
from jax.experimental import pallas as pl
from jax.experimental.pallas import tpu as pltpu

F32 = jnp.float32
BF16 = jnp.bfloat16

D_MODEL = 4096
HEAD_DIM = 128
NSA_HEADS = 12
NSA_KV_HEADS = 3
NSA_GROUP = NSA_HEADS // NSA_KV_HEADS
CMP_LEN = 32
CMP_STRIDE = 16
SLC_BLK = 64
N_SEL = 16
WIN = 512
FOX_HEADS = 8
DIL_PAIRS = ((128, 1), (512, 4), (2048, 16))
DIL_GROUPS = 3
DIL_HPG = 4
DIL_HEADS = DIL_GROUPS * DIL_HPG
ALIBI_HEADS = NSA_HEADS + DIL_HEADS
NSA_Q_W = NSA_HEADS * HEAD_DIM
NSA_KV_W = NSA_KV_HEADS * HEAD_DIM
FOX_W = FOX_HEADS * HEAD_DIM
DIL_W = DIL_HEADS * HEAD_DIM
OFF_NSA_KV = NSA_Q_W
OFF_NSA_GATE = OFF_NSA_KV + 6 * NSA_KV_W
OFF_FOX = OFF_NSA_GATE + 3 * NSA_HEADS
OFF_FOX_F = OFF_FOX + 3 * FOX_W
OFF_DIL = OFF_FOX_F + FOX_HEADS
IN_COLS = OFF_DIL + 3 * DIL_W
NSA_A_W = NSA_Q_W + 6 * NSA_KV_W
OUT_A = NSA_Q_W
OUT_B = FOX_W
OUT_C = DIL_HPG * HEAD_DIM
PEER_HEADS = 8
PEER_NKEYS = 128
PEER_EXPERTS = PEER_NKEYS ** 2
PEER_TOPK = 16
PEER_QDIM = 256
DEPTH = 2
ALPHA = (2.0 * DEPTH) ** 0.25
LN_EPS = 1e-5
NEG = -1e30
BIG = 1e30
SCALE = HEAD_DIM ** -0.5

LANES = 128
GATE_LANE0 = 0
FORGET_LANE0 = 3 * NSA_HEADS
VMEM_LIMIT = 56 * 1024 * 1024


def _cp(*sem):
    return pltpu.CompilerParams(dimension_semantics=sem, vmem_limit_bytes=VMEM_LIMIT)


def _alibi_slopes(n):
    return (np.float32(2.0) ** (-8.0 * np.arange(1, n + 1, dtype=np.float32) / n)).astype(np.float32)


def _gelu(x):
    return 0.5 * x * (1.0 + jnp.tanh(np.float32(np.sqrt(2.0 / np.pi)) * (x + np.float32(0.044715) * (x * x * x))))


def _split3(x):
    hi = x.astype(BF16)
    r1 = x - hi.astype(F32)
    mid = r1.astype(BF16)
    lo = (r1 - mid.astype(F32)).astype(BF16)
    return hi, mid, lo


def _dot3(x, m):
    hi, mid, lo = _split3(x)
    d = functools.partial(jnp.dot, preferred_element_type=F32)
    return (d(lo, m) + d(mid, m)) + d(hi, m)


def _dot_nt(a, b):
    return lax.dot_general(a, b, (((1,), (1,)), ((), ())), preferred_element_type=F32)


def _ada_kernel(c_ref, w_ref, b_ref, o_ref):
    o_ref[...] = jnp.dot(c_ref[...], w_ref[...].astype(BF16), preferred_element_type=F32) + b_ref[...]


def ada_project(c_pad, w_ada, b_ada, tn=512):
    L, D, N6 = w_ada.shape
    return pl.pallas_call(
        _ada_kernel,
        grid=(L, N6 // tn),
        in_specs=[pl.BlockSpec((8, D), lambda l, j: (0, 0)),
                  pl.BlockSpec((None, D, tn), lambda l, j: (l, 0, j)),
                  pl.BlockSpec((None, 1, tn), lambda l, j: (l, 0, j))],
        out_specs=pl.BlockSpec((None, 8, tn), lambda l, j: (l, 0, j)),
        out_shape=jax.ShapeDtypeStruct((L, 8, N6), F32),
        compiler_params=_cp("parallel", "parallel"),
        name="ada_project",
    )(c_pad, w_ada, b_ada)


def _modulate_kernel(x_ref, sc_ref, sh_ref, u_ref):
    u_ref[...] = (x_ref[...] * (1.0 + sc_ref[...]) + sh_ref[...]).astype(u_ref.dtype)


def modulate(x, mod3, sc_blk, sh_blk, T, tr=256):
    N, D = x.shape
    return pl.pallas_call(
        _modulate_kernel,
        grid=(N // tr,),
        in_specs=[pl.BlockSpec((tr, D), lambda i: (i, 0)),
                  pl.BlockSpec((None, 1, D), lambda i: (i * tr // T, 0, sc_blk)),
                  pl.BlockSpec((None, 1, D), lambda i: (i * tr // T, 0, sh_blk))],
        out_specs=pl.BlockSpec((tr, D), lambda i: (i, 0)),
        out_shape=jax.ShapeDtypeStruct((N, D), BF16),
        compiler_params=_cp("parallel"),
        name="modulate",
    )(x, mod3, mod3)


def _mm_kernel(a_ref, w_ref, o_ref, wb_ref):
    @pl.when(pl.program_id(1) == 0)
    def _():
        wb_ref[...] = w_ref[...].astype(BF16)

    o_ref[...] = jnp.dot(a_ref[...], wb_ref[...], preferred_element_type=F32).astype(o_ref.dtype)


def matmul(a, w, out_dtype, layer, tm=1024, tn=512):
    M, K = a.shape
    N = w.shape[2]
    assert M % tm == 0 and N % tn == 0
    return pl.pallas_call(
        _mm_kernel,
        grid=(N // tn, M // tm),
        in_specs=[pl.BlockSpec((tm, K), lambda j, i: (i, 0)),
                  pl.BlockSpec((None, K, tn), lambda j, i: (layer, 0, j))],
        out_specs=pl.BlockSpec((tm, tn), lambda j, i: (i, j)),
        out_shape=jax.ShapeDtypeStruct((M, N), out_dtype),
        scratch_shapes=[pltpu.VMEM((K, tn), BF16)],
        compiler_params=_cp("parallel", "arbitrary"),
        name="matmul",
    )(a, w)


def _gather_layer_rows(w_ref, wb_ref, layer, n_layers):
    tn, K = wb_ref.shape
    kt_n = K // LANES
    stride = kt_n * n_layers
    sub = 16
    for g in range(tn // sub):
        for kt in range(kt_n):
            first = (g * sub * kt_n + kt) * n_layers + layer
            wb_ref[g * sub:(g + 1) * sub, kt * LANES:(kt + 1) * LANES] = (
                w_ref[pl.ds(first, sub, stride=stride), :].astype(BF16))


def _mm_wt_kernel(a_ref, w_ref, o_ref, wb_ref, *, layer, n_layers):
    @pl.when(pl.program_id(1) == 0)
    def _():
        _gather_layer_rows(w_ref, wb_ref, layer, n_layers)

    o_ref[...] = _dot_nt(a_ref[...], wb_ref[...]).astype(o_ref.dtype)


def column_rows_view(w):
    L, K, N = w.shape
    return w.transpose(2, 0, 1).reshape(N, L, K // LANES, LANES).transpose(0, 2, 1, 3).reshape(
        N * (K // LANES) * L, LANES)


def _column_window(n_layers, K, tn, start_col):
    rows_per_col = (K // LANES) * n_layers
    return pl.BlockSpec((pl.Element(tn * rows_per_col), pl.Element(LANES)),
                        lambda *ids: (start_col(*ids) * rows_per_col, 0))


def matmul_wt(a, w_rows, out_dtype, n_layers, layer, col0, n_cols, tm=512, tn=256):
    M, K = a.shape
    assert M % tm == 0 and n_cols % tn == 0
    return pl.pallas_call(
        functools.partial(_mm_wt_kernel, layer=layer, n_layers=n_layers),
        grid=(n_cols // tn, M // tm),
        in_specs=[pl.BlockSpec((tm, K), lambda j, i: (i, 0)),
                  _column_window(n_layers, K, tn, lambda j, i: col0 + j * tn)],
        out_specs=pl.BlockSpec((tm, tn), lambda j, i: (i, j)),
        out_shape=jax.ShapeDtypeStruct((M, n_cols), out_dtype),
        scratch_shapes=[pltpu.VMEM((tn, K), BF16)],
        compiler_params=_cp("parallel", "arbitrary"),
        name="matmul_wt",
    )(a, w_rows)


def _small_proj_kernel(a_ref, wg_ref, wf_ref, o_ref, wbg_ref, wbf_ref, *, layer, n_layers):
    @pl.when(pl.program_id(0) == 0)
    def _():
        _gather_layer_rows(wg_ref, wbg_ref, layer, n_layers)
        _gather_layer_rows(wf_ref, wbf_ref, layer, n_layers)

    a = a_ref[...]
    lane = lax.broadcasted_iota(jnp.int32, o_ref.shape, 1)
    o_ref[...] = jnp.where(lane < FORGET_LANE0, _dot_nt(a, wbg_ref[...]), _dot_nt(a, wbf_ref[...]))


def small_projection(a, w_rows, n_layers, layer, tm=1024):
    M, K = a.shape
    return pl.pallas_call(
        functools.partial(_small_proj_kernel, layer=layer, n_layers=n_layers),
        grid=(M // tm,),
        in_specs=[pl.BlockSpec((tm, K), lambda i: (i, 0)),
                  _column_window(n_layers, K, LANES, lambda i: OFF_NSA_GATE - GATE_LANE0),
                  _column_window(n_layers, K, LANES, lambda i: OFF_FOX_F - FORGET_LANE0)],
        out_specs=pl.BlockSpec((tm, LANES), lambda i: (i, 0)),
        out_shape=jax.ShapeDtypeStruct((M, LANES), F32),
        scratch_shapes=[pltpu.VMEM((LANES, K), BF16), pltpu.VMEM((LANES, K), BF16)],
        compiler_params=_cp("arbitrary"),
        name="small_projection",
    )(a, w_rows, w_rows)


HEADS_PER_STEP = 4


def _head_cols(i):
    return slice(i * HEAD_DIM, (i + 1) * HEAD_DIM)


def _flash(qs, load_kv, lo, hi, chunk_fn, score_fns):
    tq = qs[0].shape[0]

    def body(c, carry):
        shared = chunk_fn(c)
        kvs = [load_kv(c, i) for i in range(len(qs))]
        raws = [_dot_nt(q, kvs[i][0]) for i, q in enumerate(qs)]
        mids = []
        for i, raw in enumerate(raws):
            m, l, _ = carry[i]
            s = score_fns[i](raw, c, shared)
            m_new = jnp.maximum(m, jnp.max(s, axis=1, keepdims=True))
            alpha = jnp.exp(m - m_new)
            p = jnp.exp(s - m_new)
            mids.append((m_new, alpha, alpha * l + jnp.sum(p, axis=1, keepdims=True), p))
        out = []
        for i, (m_new, alpha, l, p) in enumerate(mids):
            v = kvs[i][1]
            acc = alpha * carry[i][2] + jnp.dot(p.astype(v.dtype), v, preferred_element_type=F32)
            out.append((m_new, l, acc))
        return tuple(out)

    one = (jnp.full((tq, 1), NEG, F32), jnp.zeros((tq, 1), F32), jnp.zeros((tq, HEAD_DIM), F32))
    return lax.fori_loop(lo, hi, body, tuple(one for _ in qs))


def _flash_t(qs, load_kv, lo, hi, chunk_fn, score_fns):
    tq = qs[0].shape[0]

    def body(c, stats):
        shared = chunk_fn(c)
        raws = [_dot_nt(load_kv(c, i)[0], q) for i, q in enumerate(qs)]
        mids = []
        for i, raw in enumerate(raws):
            m, l, _ = stats[i]
            s = score_fns[i](raw, c, shared)
            m_new = jnp.maximum(m, jnp.max(s, axis=0, keepdims=True))
            alpha = jnp.exp(m - m_new)
            p = jnp.exp(s - m_new)
            mids.append((m_new, alpha, alpha * l + jnp.sum(p, axis=0, keepdims=True), p))
        out = []
        for i, (m_new, alpha, l, p) in enumerate(mids):
            v = load_kv(c, i)[1]
            pv = lax.dot_general(v, p.astype(v.dtype), (((0,), (0,)), ((), ())), preferred_element_type=F32)
            out.append((m_new, l, alpha * stats[i][2] + pv))
        return tuple(out)

    one = (jnp.full((1, tq), NEG, F32), jnp.zeros((1, tq), F32), jnp.zeros((HEAD_DIM, tq), F32))
    return lax.fori_loop(lo, hi, body, tuple(one for _ in qs))


def _alibi_kernel(slope_ref, q_ref, k_ref, v_ref, *rest, tq, tk, max_dist, shared_kv, with_sel, with_lse):
    rest = list(rest)
    sel_ref = rest.pop(0) if with_sel else None
    o_ref = rest.pop(0)
    lse_ref = rest.pop(0) if with_lse else None
    hb = pl.program_id(1)
    q0 = pl.program_id(3) * tq
    nh = HEADS_PER_STEP
    rc = lax.broadcasted_iota(jnp.int32, (tk, tq), 1) - lax.broadcasted_iota(jnp.int32, (tk, tq), 0)
    if with_sel:
        sel_t = sel_ref[...].astype(F32).T.astype(BF16)
        lane_blk = lax.broadcasted_iota(jnp.int32, (tk, LANES), 1)
        key_blk = lax.broadcasted_iota(jnp.int32, (tk, LANES), 0) // SLC_BLK

    def valid_fn(c):
        dist = rc + (q0 - c * tk)
        valid = dist >= 0
        if max_dist is not None:
            valid = valid & (dist <= max_dist)
        if with_sel:
            expand = (lane_blk == key_blk + (hb * (LANES // 4) + c * (tk // SLC_BLK))).astype(BF16)
            valid = valid & (jnp.dot(expand, sel_t, preferred_element_type=F32) > 0.5)
        return dist.astype(F32), valid

    def make_score(i):
        slope = slope_ref[hb * nh + i]

        def score_fn(s, c, shared):
            dist_f, valid = shared
            return jnp.where(valid, s * SCALE - slope * dist_f, NEG)
        return score_fn

    def load_kv(c, i):
        off = pl.multiple_of(c * tk, tk)
        cols = _head_cols(0 if shared_kv else i)
        return k_ref[pl.ds(off, tk), cols], v_ref[pl.ds(off, tk), cols]

    hi = (q0 + tq - 1) // tk + 1
    lo = 0 if max_dist is None else jnp.maximum(q0 - max_dist, 0) // tk
    res = _flash_t([q_ref[:, _head_cols(i)] for i in range(nh)], load_kv, lo, hi, valid_fn,
                   [make_score(i) for i in range(nh)])
    for i, (m, l, acc_t) in enumerate(res):
        o_ref[:, _head_cols(i)] = (acc_t / l).T.astype(o_ref.dtype)
        if with_lse:
            lse_ref[:, _head_cols(i)] = jnp.broadcast_to(m + jnp.log(l), (HEAD_DIM, tq)).T


def alibi_attention(qa, ka, va, slopes, *, B, Ts, n_sub, n_head_blocks, shared_kv, q_col, k_col, v_col,
                    max_dist, tq, tk, sel=None, with_lse=False):
    tq = min(tq, Ts)
    tk = min(tk, Ts)
    nq = Ts // tq
    rows = B * Ts
    wq = HEADS_PER_STEP * HEAD_DIM
    wkv = HEAD_DIM if shared_kv else wq
    in_specs = [pl.BlockSpec((tq, wq), lambda b, h, r, i, s: (b * nq + i, q_col(h, r))),
                pl.BlockSpec((Ts, wkv), lambda b, h, r, i, s: (b, k_col(h, r))),
                pl.BlockSpec((Ts, wkv), lambda b, h, r, i, s: (b, v_col(h, r)))]
    args = [qa, ka, va]
    if sel is not None:
        in_specs.append(pl.BlockSpec((tq, LANES), lambda b, h, r, i, s: (b * nq + i, 0)))
        args.append(sel)
    o_spec = pl.BlockSpec((tq, wq), lambda b, h, r, i, s: (b * nq + i, r * n_head_blocks + h))
    o_shape = jax.ShapeDtypeStruct((rows, n_sub * n_head_blocks * wq), F32)
    kern = functools.partial(_alibi_kernel, tq=tq, tk=tk, max_dist=max_dist, shared_kv=shared_kv,
                             with_sel=sel is not None, with_lse=with_lse)
    return pl.pallas_call(
        kern,
        grid_spec=pltpu.PrefetchScalarGridSpec(
            num_scalar_prefetch=1, grid=(B, n_head_blocks, n_sub, nq),
            in_specs=in_specs,
            out_specs=[o_spec, o_spec] if with_lse else o_spec),
        out_shape=[o_shape, o_shape] if with_lse else o_shape,
        compiler_params=_cp("parallel", "parallel", "parallel", "parallel"),
        name="alibi_attention",
    )(jnp.asarray(slopes, F32), *args)


def _fox_kernel(q_ref, k_ref, v_ref, fq_ref, fk_ref, o_ref, *, tq, tk):
    hb = pl.program_id(1)
    q0 = pl.program_id(2) * tq
    nh = HEADS_PER_STEP
    lane = lax.broadcasted_iota(jnp.int32, (tq, LANES), 1)
    f_tok = fq_ref[...]
    rc = lax.broadcasted_iota(jnp.int32, (tq, tk), 0) - lax.broadcasted_iota(jnp.int32, (tq, tk), 1)

    def make_score(i):
        fq = jnp.sum(jnp.where(lane == FORGET_LANE0 + hb * nh + i, f_tok, 0.0), axis=1, keepdims=True)

        def score_fn(s, c, causal):
            return jnp.where(causal, (s * SCALE + fq) - fk_ref[i, pl.ds(c, 1), :], NEG)
        return score_fn

    def load_kv(c, i):
        off = pl.multiple_of(c * tk, tk)
        return k_ref[pl.ds(off, tk), _head_cols(i)], v_ref[pl.ds(off, tk), _head_cols(i)]

    hi = (q0 + tq - 1) // tk + 1
    res = _flash([q_ref[:, _head_cols(i)] for i in range(nh)], load_kv, 0, hi,
                 lambda c: rc + (q0 - c * tk) >= 0, [make_score(i) for i in range(nh)])
    for i, (_, l, acc) in enumerate(res):
        o_ref[:, _head_cols(i)] = (acc / l).astype(o_ref.dtype)


def fox_attention(proj_fox, f_tok, f_rows, *, B, T, tq=256, tk=256):
    nq = T // tq
    nh = HEADS_PER_STEP
    nhb = FOX_HEADS // nh
    wq = nh * HEAD_DIM
    return pl.pallas_call(
        functools.partial(_fox_kernel, tq=tq, tk=tk),
        grid=(B, nhb, nq),
        in_specs=[pl.BlockSpec((tq, wq), lambda b, h, i: (b * nq + i, h)),
                  pl.BlockSpec((T, wq), lambda b, h, i: (b, nhb + h)),
                  pl.BlockSpec((T, wq), lambda b, h, i: (b, 2 * nhb + h)),
                  pl.BlockSpec((tq, LANES), lambda b, h, i: (b * nq + i, 0)),
                  pl.BlockSpec((nh, T // tk, tk), lambda b, h, i: (b * nhb + h, 0, 0))],
        out_specs=pl.BlockSpec((tq, wq), lambda b, h, i: (b * nq + i, h)),
        out_shape=jax.ShapeDtypeStruct((B * T, FOX_W), BF16),
        compiler_params=_cp("parallel", "parallel", "parallel"),
        name="fox_attention",
    )(proj_fox, proj_fox, proj_fox, f_tok, f_rows)


def _cumsum_kernel(x_ref, b_ref, o_ref, *, T, tr):
    i = pl.program_id(1)
    z = x_ref[...] + b_ref[...]
    logf = jnp.minimum(z, 0.0) - jnp.log1p(jnp.exp(-jnp.abs(z)))
    row = lax.broadcasted_iota(jnp.int32, (tr, T), 0) + i * tr
    col = lax.broadcasted_iota(jnp.int32, (tr, T), 1)
    o_ref[...] = _dot3_lhs((row >= col).astype(BF16), logf)


def _dot3_lhs(m, x):
    hi, mid, lo = _split3(x)
    d = functools.partial(jnp.dot, preferred_element_type=F32)
    return (d(m, lo) + d(m, mid)) + d(m, hi)


def forget_cumsum(small, bias_row, *, B, T, tr=256):
    return pl.pallas_call(
        functools.partial(_cumsum_kernel, T=T, tr=tr),
        grid=(B, T // tr),
        in_specs=[pl.BlockSpec((T, LANES), lambda b, i: (b, 0)),
                  pl.BlockSpec((1, LANES), lambda b, i: (0, 0))],
        out_specs=pl.BlockSpec((tr, LANES), lambda b, i: (b * (T // tr) + i, 0)),
        out_shape=jax.ShapeDtypeStruct((B * T, LANES), F32),
        compiler_params=_cp("parallel", "parallel"),
        name="forget_cumsum",
    )(small, bias_row)


def _compress_kernel(x_ref, pe_ref, w1_ref, w2_ref, o_ref):
    x = x_ref[...].astype(F32)
    y0 = jnp.dot((x + pe_ref[0]).astype(BF16), w1_ref[0], preferred_element_type=F32)
    y1 = jnp.dot((x + pe_ref[1]).astype(BF16), w1_ref[1], preferred_element_type=F32)
    n = y1.shape[0]
    hmid = _gelu(y0 + pltpu.roll(y1, n - 1, 0))
    o_ref[...] = jnp.dot(hmid.astype(BF16), w2_ref[...], preferred_element_type=F32).astype(o_ref.dtype)


def compress_blocks(chunks, pe, w1, w2):
    _, G, nch, cw = chunks.shape
    return pl.pallas_call(
        _compress_kernel,
        grid=(2, G),
        in_specs=[pl.BlockSpec((None, None, nch, cw), lambda a, g: (a, g, 0, 0)),
                  pl.BlockSpec((None, 2, 1, cw), lambda a, g: (a, 0, 0, 0)),
                  pl.BlockSpec((None, 2, cw, HEAD_DIM), lambda a, g: (a, 0, 0, 0)),
                  pl.BlockSpec((None, HEAD_DIM, HEAD_DIM), lambda a, g: (a, 0, 0))],
        out_specs=pl.BlockSpec((None, None, nch, HEAD_DIM), lambda a, g: (a, g, 0, 0)),
        out_shape=jax.ShapeDtypeStruct((2, G, nch, HEAD_DIM), BF16),
        compiler_params=_cp("parallel", "parallel"),
        name="compress_blocks",
    )(chunks, pe, w1, w2)


def _cmp_kernel(slope_ref, q_ref, kc_ref, vc_ref, smat_ref, o_ref, sel_ref, *, tq, n_cmp, n_slc):
    q0 = pl.program_id(1) * tq
    nk = kc_ref.shape[1]
    pos = lax.broadcasted_iota(jnp.int32, (tq, nk), 0) + q0
    n_idx = lax.broadcasted_iota(jnp.int32, (tq, nk), 1)
    dist_c = pos - (n_idx * CMP_STRIDE + (CMP_LEN - 1))
    valid_c = (dist_c >= 0) & (n_idx < n_cmp)
    dist_f = dist_c.astype(F32)
    p_slc = jnp.zeros((tq, LANES), F32)
    for kv in range(NSA_KV_HEADS):
        kc = kc_ref[kv]
        vc = vc_ref[kv]
        imp = jnp.zeros((tq, nk), F32)
        for g in range(NSA_GROUP):
            hh = kv * NSA_GROUP + g
            q = q_ref[:, hh * HEAD_DIM:(hh + 1) * HEAD_DIM]
            s = _dot_nt(q, kc) * SCALE - slope_ref[hh] * dist_f
            s = jnp.where(valid_c, s, NEG)
            e = jnp.where(valid_c, jnp.exp(s - jnp.max(s, axis=1, keepdims=True)), 0.0)
            p = e / jnp.maximum(jnp.sum(e, axis=1, keepdims=True), 1e-30)
            o_ref[:, hh * HEAD_DIM:(hh + 1) * HEAD_DIM] = jnp.dot(
                p.astype(BF16), vc, preferred_element_type=F32)
            imp = imp + p
        p_slc = p_slc + _dot3(imp, smat_ref[kv])
    grp = LANES // 4
    p_t = p_slc.T
    j = lax.broadcasted_iota(jnp.int32, (grp, tq), 0)
    qblk = (lax.broadcasted_iota(jnp.int32, (grp, tq), 1) + q0) // SLC_BLK
    forced = (j == 0) | (j == qblk) | (j == qblk - 1)
    causal = j <= qblk
    chosen = []
    for kv in range(NSA_KV_HEADS):
        score = jnp.where(forced, BIG, jnp.where(causal, p_t[kv * grp:(kv + 1) * grp], NEG))
        rank = jnp.zeros((grp, tq), jnp.int32)
        for k in range(n_slc):
            row = score[k:k + 1]
            ahead = (row > score) | ((row == score) & (k < j))
            rank = rank + ahead.astype(jnp.int32)
        chosen.append(jnp.where(causal & (rank < min(N_SEL, n_slc)), 1.0, 0.0))
    chosen.append(jnp.zeros((LANES - NSA_KV_HEADS * grp, tq), F32))
    sel_ref[...] = jnp.concatenate(chosen, axis=0).T.astype(sel_ref.dtype)


def cmp_attention(proj_a, kvc, smat, slopes, *, B, T, tq=256):
    nq = T // tq
    nch = kvc.shape[2]
    n_cmp = (T - CMP_LEN) // CMP_STRIDE + 1
    n_slc = T // SLC_BLK
    kvc4 = kvc.reshape(2, B, NSA_KV_HEADS, nch, HEAD_DIM)
    return pl.pallas_call(
        functools.partial(_cmp_kernel, tq=tq, n_cmp=n_cmp, n_slc=n_slc),
        grid_spec=pltpu.PrefetchScalarGridSpec(
            num_scalar_prefetch=1, grid=(B, nq),
            in_specs=[pl.BlockSpec((tq, NSA_Q_W), lambda b, i, s: (b * nq + i, 0)),
                      pl.BlockSpec((None, None, NSA_KV_HEADS, nch, HEAD_DIM), lambda b, i, s: (0, b, 0, 0, 0)),
                      pl.BlockSpec((None, None, NSA_KV_HEADS, nch, HEAD_DIM), lambda b, i, s: (1, b, 0, 0, 0)),
                      pl.BlockSpec((NSA_KV_HEADS, nch, LANES), lambda b, i, s: (0, 0, 0))],
            out_specs=[pl.BlockSpec((tq, NSA_Q_W), lambda b, i, s: (b * nq + i, 0)),
                       pl.BlockSpec((tq, LANES), lambda b, i, s: (b * nq + i, 0))]),
        out_shape=[jax.ShapeDtypeStruct((B * T, NSA_Q_W), F32),
                   jax.ShapeDtypeStruct((B * T, LANES), BF16)],
        compiler_params=_cp("parallel", "parallel"),
        name="cmp_attention",
    )(jnp.asarray(slopes, F32), proj_a, kvc4, kvc4, smat)


def _pool_matrix(nch, n_slc):
    r = SLC_BLK // CMP_STRIDE
    c_over = CMP_LEN // CMP_STRIDE
    n_cmp_max = nch
    m = np.zeros((NSA_KV_HEADS, nch, LANES), np.float32)
    for kv in range(NSA_KV_HEADS):
        for j in range(n_slc):
            for a in range(r):
                for b in range(c_over):
                    i = r * j + a + b
                    if i < n_cmp_max:
                        m[kv, i, kv * (LANES // 4) + j] += 1.0
    return m


def _nsa_merge_kernel(oc_ref, os_ref, ow_ref, g_ref, e_ref, o_ref):
    sig = jax.nn.sigmoid(g_ref[...])
    out = _dot3(sig, e_ref[0]) * oc_ref[...]
    out = out + _dot3(sig, e_ref[1]) * os_ref[...]
    out = out + _dot3(sig, e_ref[2]) * ow_ref[...]
    o_ref[...] = out.astype(o_ref.dtype)


def nsa_merge(o_cmp, o_slc, o_win, small, tr=256):
    N = o_cmp.shape[0]
    e = np.zeros((3, LANES, NSA_Q_W), np.float32)
    for br in range(3):
        for h in range(NSA_HEADS):
            e[br, GATE_LANE0 + h * 3 + br, h * HEAD_DIM:(h + 1) * HEAD_DIM] = 1.0
    row = pl.BlockSpec((tr, NSA_Q_W), lambda i: (i, 0))
    return pl.pallas_call(
        _nsa_merge_kernel,
        grid=(N // tr,),
        in_specs=[row, row, row,
                  pl.BlockSpec((tr, LANES), lambda i: (i, 0)),
                  pl.BlockSpec((3, LANES, NSA_Q_W), lambda i: (0, 0, 0))],
        out_specs=row,
        out_shape=jax.ShapeDtypeStruct((N, NSA_Q_W), BF16),
        compiler_params=_cp("parallel"),
        name="nsa_merge",
    )(o_cmp, o_slc, o_win, small, jnp.asarray(e, BF16))


def _dil_merge_kernel(o0, o1, o2, l0, l1, l2, o_ref):
    a, b, c = l0[...], l1[...], l2[...]
    m = jnp.maximum(jnp.maximum(a, b), c)
    ea, eb, ec = jnp.exp(a - m), jnp.exp(b - m), jnp.exp(c - m)
    z = ea + eb + ec
    o_ref[...] = ((ea / z) * o0[...] + (eb / z) * o1[...] + (ec / z) * o2[...]).astype(o_ref.dtype)


def dil_merge(outs, lses, tr=512):
    N, W = outs[0].shape
    row = pl.BlockSpec((tr, W), lambda i: (i, 0))
    return pl.pallas_call(
        _dil_merge_kernel,
        grid=(N // tr,),
        in_specs=[row] * 6,
        out_specs=row,
        out_shape=jax.ShapeDtypeStruct((N, W), BF16),
        compiler_params=_cp("parallel"),
        name="dil_merge",
    )(*outs, *lses)


def _branch_kernel(u_ref, wga, wgb, wgc, bga, bgb, bgc, oa, ob, oc, wa, wb, wc, z_ref):
    d = functools.partial(jnp.dot, preferred_element_type=F32)
    u = u_ref[...]
    z = jax.nn.sigmoid(d(u, wga[...]) + bga[...]) * d(oa[...], wa[...])
    z = z + jax.nn.sigmoid(d(u, wgb[...]) + bgb[...]) * d(ob[...], wb[...])
    z = z + jax.nn.sigmoid(d(u, wgc[...]) + bgc[...]) * d(oc[...], wc[...])
    z_ref[...] = z.astype(z_ref.dtype)


def gated_branches(u, w_gate, b_gate, layer, o_a, o_b, o_c, wb_a, wb_b, wb_c, tm=512, tn=512):
    N, D = u.shape
    nj = D // tn

    def wg(k):
        return pl.BlockSpec((None, D, tn), lambda j, i: (layer, 0, k * nj + j))

    def bg(k):
        return pl.BlockSpec((None, 1, tn), lambda j, i: (layer, 0, k * nj + j))

    def act(w):
        return pl.BlockSpec((tm, w), lambda j, i: (i, 0))

    def wup(w):
        return pl.BlockSpec((w, tn), lambda j, i: (0, j))

    return pl.pallas_call(
        _branch_kernel,
        grid=(nj, N // tm),
        in_specs=[act(D), wg(0), wg(1), wg(2), bg(0), bg(1), bg(2),
                  act(OUT_A), act(OUT_B), act(OUT_C), wup(OUT_A), wup(OUT_B), wup(OUT_C)],
        out_specs=pl.BlockSpec((tm, tn), lambda j, i: (i, j)),
        out_shape=jax.ShapeDtypeStruct((N, D), BF16),
        compiler_params=_cp("parallel", "parallel"),
        name="gated_branches",
    )(u, w_gate, w_gate, w_gate, b_gate, b_gate, b_gate, o_a, o_b, o_c, wb_a, wb_b, wb_c)


def _postnorm_kernel(x_ref, y_ref, g_ref, gam_ref, bet_ref, sc_ref, sh_ref, xo_ref, u_ref, *ut_ref):
    h = ALPHA * x_ref[...] + (1.0 + g_ref[...]) * y_ref[...]
    mu = jnp.mean(h, axis=-1, keepdims=True)
    hc = h - mu
    var = jnp.mean(hc * hc, axis=-1, keepdims=True)
    xn = hc * lax.rsqrt(var + LN_EPS) * gam_ref[...] + bet_ref[...]
    xo_ref[...] = xn
    u = xn * (1.0 + sc_ref[...]) + sh_ref[...]
    u_ref[...] = u.astype(u_ref.dtype)
    if ut_ref:
        ut_ref[0][...] = u.T.astype(u_ref.dtype)


def postnorm(x, y, mod_g, g_blk, gamma, beta, mod_n, sc_blk, sh_blk, T, tr=256, transposed=False):
    N, D = x.shape
    row = pl.BlockSpec((tr, D), lambda i: (i, 0))
    vec = pl.BlockSpec((1, D), lambda i: (0, 0))

    def mod(blk):
        return pl.BlockSpec((None, 1, D), lambda i: (i * tr // T, 0, blk))

    out_specs = [row, row]
    out_shape = [jax.ShapeDtypeStruct((N, D), F32), jax.ShapeDtypeStruct((N, D), BF16)]
    if transposed:
        out_specs.append(pl.BlockSpec((D, tr), lambda i: (0, i)))
        out_shape.append(jax.ShapeDtypeStruct((D, N), BF16))
    return pl.pallas_call(
        _postnorm_kernel,
        grid=(N // tr,),
        in_specs=[row, row, mod(g_blk), vec, vec, mod(sc_blk), mod(sh_blk)],
        out_specs=out_specs,
        out_shape=out_shape,
        compiler_params=_cp("parallel"),
        name="postnorm",
    )(x, y, mod_g, gamma, beta, mod_n, mod_n)


def _topk_rows(s, k, break_ties):
    rows, tn = s.shape
    iota = lax.broadcasted_iota(jnp.int32, (rows, tn), 0)
    rank = jnp.full((rows, tn), k, jnp.int32)
    cur = s
    tops = []
    for r in range(k):
        m = jnp.max(cur, axis=0, keepdims=True)
        hit = cur == m
        if break_ties:
            hit = iota == jnp.min(jnp.where(hit, iota, rows), axis=0, keepdims=True)
        rank = jnp.where(hit, r, rank)
        cur = jnp.where(hit, -jnp.inf, cur)
        tops.append(m)
    return rank, tops


def _merge_topk(top1, top2, k):
    t1 = _stack_rows(top1)
    tn = t1.shape[1]
    iota = lax.broadcasted_iota(jnp.int32, (k, tn), 0)
    taken = jnp.zeros((k, tn), jnp.int32)
    front = t1 + top2[0]
    m0 = top1[0] + top2[0]
    z = jnp.zeros((1, tn), F32)
    for _ in range(k):
        m = jnp.max(front, axis=0, keepdims=True)
        hit = iota == jnp.min(jnp.where(front == m, iota, k), axis=0, keepdims=True)
        z = z + jnp.exp(m - m0)
        nxt = jnp.sum(jnp.where(hit, taken, 0), axis=0, keepdims=True) + 1
        follower = jnp.full((1, tn), -jnp.inf, F32)
        for rb in range(1, k):
            follower = jnp.where(nxt == rb, top2[rb], follower)
        t1_hit = jnp.sum(jnp.where(hit, t1, 0.0), axis=0, keepdims=True)
        taken = taken + hit.astype(jnp.int32)
        front = jnp.where(hit, t1_hit + follower, front)
    return taken.astype(F32), z


def _stack_rows(rows_list):
    n = len(rows_list)
    tn = rows_list[0].shape[1]
    iota = lax.broadcasted_iota(jnp.int32, (n, tn), 0)
    out = jnp.zeros((n, tn), rows_list[0].dtype)
    for r, row in enumerate(rows_list):
        out = jnp.where(iota == r, row, out)
    return out


def _peer_select_kernel(q_ref, sk_ref, e1_ref, l1_ref, e2_ref, r2_ref, *, cpt):
    K = PEER_TOPK
    half = PEER_QDIM // 2
    s1 = _dot_nt(sk_ref[0], q_ref[:, :half])
    s2 = _dot_nt(sk_ref[1], q_ref[:, half:])

    def emit(break_ties):
        rank1, top1 = _topk_rows(s1, K, break_ties)
        rank2, top2 = _topk_rows(s2, K, break_ties)
        row_len, z = _merge_topk(top1, top2, K)
        l1 = jnp.zeros(s1.shape, F32)
        for ra in range(K):
            l1 = jnp.where(rank1 == ra, row_len[ra:ra + 1], l1)
        e1 = jnp.where(rank1 < K, jnp.exp(s1 - top1[0]) / z, 0.0)
        e2 = jnp.where(rank2 < K, jnp.exp(s2 - top2[0]), 0.0)
        for j in range(PEER_NKEYS // cpt):
            e1_ref[j] = e1[j * cpt:(j + 1) * cpt]
            l1_ref[j] = l1[j * cpt:(j + 1) * cpt]
        e2_ref[...] = e2.astype(BF16)
        r2_ref[...] = rank2.astype(BF16)
        members = (rank1 < K).astype(F32) + (rank2 < K).astype(F32)
        return jnp.sum(jnp.abs(jnp.sum(members, axis=0, keepdims=True) - 2.0 * K))

    tied = emit(False)

    @pl.when(tied != 0.0)
    def _():
        emit(True)


def peer_select(q, subkeys, *, cpt, tn=256):
    N = q.shape[0]
    H = PEER_HEADS
    nt = PEER_NKEYS // cpt
    s1 = pl.BlockSpec((nt, None, cpt, tn), lambda i, h: (0, h, 0, i))
    s2 = pl.BlockSpec((None, PEER_NKEYS, tn), lambda i, h: (h, 0, i))
    sh1 = jax.ShapeDtypeStruct((nt, H, cpt, N), F32)
    sh2 = jax.ShapeDtypeStruct((H, PEER_NKEYS, N), BF16)
    return pl.pallas_call(
        functools.partial(_peer_select_kernel, cpt=cpt),
        grid=(N // tn, H),
        in_specs=[pl.BlockSpec((tn, PEER_QDIM), lambda i, h: (i, h)),
                  pl.BlockSpec((None, 2, PEER_NKEYS, PEER_QDIM // 2), lambda i, h: (h, 0, 0, 0))],
        out_specs=[s1, s1, s2, s2],
        out_shape=[sh1, sh1, sh2, sh2],
        compiler_params=_cp("parallel", "parallel"),
        name="peer_select",
    )(q, subkeys)


def _peer_dense_kernel(ut_ref, tu_ref, tv_ref, e1_ref, l1_ref, e2_ref, r2_ref, o_ref, a_s, *, cpt, tnc):
    @pl.when(pl.program_id(1) == 0)
    def _():
        o_ref[...] = jnp.zeros_like(o_ref)

    nk = PEER_NKEYS
    cph = cpt // 2
    tm = ut_ref.shape[1]
    for k in range(2):
        rows = slice(k * cph * nk, (k + 1) * cph * nk)
        for hc in range(2):
            tc = slice(hc * (tm // 2), (hc + 1) * (tm // 2))
            ht = jnp.dot(tu_ref[rows, :], ut_ref[:, tc], preferred_element_type=F32)
            for c2 in range(cph):
                cc = k * cph + c2
                w = None
                for h in range(PEER_HEADS):
                    hit = r2_ref[h, :, tc] < l1_ref[h, cc:cc + 1, tc].astype(BF16)
                    wh = jnp.where(hit, e2_ref[h, :, tc], jnp.zeros((), BF16)) * e1_ref[h, cc:cc + 1, tc].astype(BF16)
                    w = wh if w is None else w + wh
                a_s[k, tc, c2 * nk:(c2 + 1) * nk] = (w.astype(F32) * _gelu(ht[c2 * nk:(c2 + 1) * nk])).T.astype(BF16)
    for k in range(2):
        rows = slice(k * cph * nk, (k + 1) * cph * nk)
        for nt in range(o_ref.shape[1] // tnc):
            cs = slice(nt * tnc, (nt + 1) * tnc)
            o_ref[:, cs] += jnp.dot(a_s[k], tv_ref[rows, cs], preferred_element_type=F32)


def peer_dense(ut, tu, tv, layer, e1, l1, e2, r2, *, cpt, tm=512):
    D, N = ut.shape
    E = tu.shape[1]
    te = cpt * PEER_NKEYS
    H = PEER_HEADS
    once = pl.Buffered(1)
    s1 = pl.BlockSpec((None, H, cpt, tm), lambda i, j: (j, 0, 0, i))
    s2 = pl.BlockSpec((H, PEER_NKEYS, tm), lambda i, j: (0, 0, i), pipeline_mode=once)
    tab = pl.BlockSpec((None, te, D), lambda i, j: (layer, j, 0))
    return pl.pallas_call(
        functools.partial(_peer_dense_kernel, cpt=cpt, tnc=512),
        grid=(N // tm, E // te),
        in_specs=[pl.BlockSpec((D, tm), lambda i, j: (0, i), pipeline_mode=once), tab, tab, s1, s1, s2, s2],
        out_specs=pl.BlockSpec((tm, D), lambda i, j: (i, 0)),
        out_shape=jax.ShapeDtypeStruct((N, D), F32),
        scratch_shapes=[pltpu.VMEM((2, tm, te // 2), BF16)],
        compiler_params=_cp("parallel", "arbitrary"),
        name="peer_dense",
    )(ut, tu, tv, e1, l1, e2, r2)


def _hybrid_mixer(u, p, *, B, T):
    N = B * T
    sl = _alibi_slopes(ALIBI_HEADS)
    nsa_slopes = sl[0::2]
    dil_slopes = sl[1::2].reshape(DIL_GROUPS, DIL_HPG)

    lyr = p["layer"]
    nl = p["n_layers"]
    proj_a = matmul_wt(u, p["w_in_rows"], BF16, nl, lyr, 0, NSA_A_W)
    small = small_projection(u, p["w_in_rows"], nl, lyr)
    proj_fox = matmul_wt(u, p["w_in_rows"], BF16, nl, lyr, OFF_FOX, 3 * FOX_W, tn=512)
    proj_dil = matmul_wt(u, p["w_in_rows"], BF16, nl, lyr, OFF_DIL, 3 * DIL_W, tn=512)

    nch = T // CMP_STRIDE
    kv_cmp = proj_a[:, OFF_NSA_KV:OFF_NSA_KV + 2 * NSA_KV_W].reshape(B, T, 2, NSA_KV_HEADS, HEAD_DIM)
    chunks = kv_cmp.transpose(2, 0, 3, 1, 4).reshape(2, B * NSA_KV_HEADS, nch, CMP_STRIDE * HEAD_DIM)
    kvc = compress_blocks(chunks, p["cmp_pe"], p["cmp_w1"], p["cmp_w2"])
    smat = jnp.asarray(_pool_matrix(nch, T // SLC_BLK), BF16)
    o_cmp, sel = cmp_attention(proj_a, kvc, smat, nsa_slopes, B=B, T=T)
    qh = NSA_HEADS
    kvh = NSA_KV_HEADS
    assert NSA_GROUP == HEADS_PER_STEP and DIL_HPG == HEADS_PER_STEP
    nsa_kw = dict(B=B, Ts=T, n_sub=1, n_head_blocks=kvh, shared_kv=True,
                  q_col=lambda h, r: h, tq=128, tk=256)
    o_slc = alibi_attention(proj_a, proj_a, proj_a, nsa_slopes, max_dist=None, sel=sel,
                            k_col=lambda h, r: qh + 2 * kvh + h,
                            v_col=lambda h, r: qh + 3 * kvh + h, **nsa_kw)
    o_win = alibi_attention(proj_a, proj_a, proj_a, nsa_slopes, max_dist=WIN - 1,
                            k_col=lambda h, r: qh + 4 * kvh + h,
                            v_col=lambda h, r: qh + 5 * kvh + h, **nsa_kw)
    o_a = nsa_merge(o_cmp, o_slc, o_win, small)

    f_tok = forget_cumsum(small, p["b_forget_row"], B=B, T=T)
    tkf = min(256, T)
    f_rows = f_tok[:, FORGET_LANE0:FORGET_LANE0 + FOX_HEADS].reshape(B, T, FOX_HEADS)
    f_rows = f_rows.transpose(0, 2, 1).reshape(B * FOX_HEADS, T // tkf, tkf)
    o_b = fox_attention(proj_fox, f_tok, f_rows, B=B, T=T, tq=min(128, T), tk=tkf)

    outs, lses = [], []
    ncb = 3 * DIL_GROUPS
    for g, (window, dil) in enumerate(DIL_PAIRS):
        Ts = T // dil
        view = proj_dil.reshape(B * Ts, dil * 3 * DIL_W)
        o, lse = alibi_attention(
            view, view, view, dil_slopes[g] * np.float32(dil), B=B, Ts=Ts, n_sub=dil, n_head_blocks=1,
            shared_kv=False, max_dist=window // dil, tq=128, tk=128, with_lse=True,
            q_col=lambda h, r, g=g: r * ncb + g,
            k_col=lambda h, r, g=g: r * ncb + DIL_GROUPS + g,
            v_col=lambda h, r, g=g: r * ncb + 2 * DIL_GROUPS + g)
        outs.append(o.reshape(N, OUT_C))
        lses.append(lse.reshape(N, OUT_C))
    o_c = dil_merge(outs, lses)

    z = gated_branches(u, p["w_gate"], p["b_gate"], lyr, o_a, o_b, o_c, p["wb_a"], p["wb_b"], p["wb_c"])
    return matmul(z, p["w_out"], F32, layer=lyr)


def _peer_ffn(u, ut, p):
    cpt = 4
    q = matmul(u, p["peer_wq"], BF16, layer=p["layer"])
    e1, l1, e2, r2 = peer_select(q, p["peer_subkeys"], cpt=cpt)
    return peer_dense(ut, p["peer_u"], p["peer_v"], p["layer"], e1, l1, e2, r2, cpt=cpt)


def _layer_params(l, shared, b_forget, cmp_pe, cmp_w1, cmp_w2, w_branch, peer_subkeys):
    b_row = jnp.zeros((1, LANES), F32).at[0, FORGET_LANE0:FORGET_LANE0 + FOX_HEADS].set(b_forget[l])
    half = CMP_LEN // 2
    return dict(
        shared,
        layer=l,
        b_forget_row=b_row,
        cmp_pe=cmp_pe[l].reshape(2, 2, 1, half * HEAD_DIM),
        cmp_w1=cmp_w1[l].astype(BF16).reshape(2, 2, half * HEAD_DIM, HEAD_DIM),
        cmp_w2=cmp_w2[l].astype(BF16),
        wb_a=w_branch[l, :OUT_A].astype(BF16),
        wb_b=w_branch[l, OUT_A:OUT_A + OUT_B].astype(BF16),
        wb_c=w_branch[l, OUT_A + OUT_B:].astype(BF16),
        peer_subkeys=peer_subkeys[l].astype(BF16),
    )


def kernel(x, c, w_ada, b_ada, w_in, b_forget, cmp_pe, cmp_w1, cmp_w2, w_branch, w_gate, b_gate, w_out,
           ln1_g, ln1_b, peer_wq, peer_subkeys, peer_u, peer_v, ln2_g, ln2_b):
    B, T, D = x.shape
    L = w_ada.shape[0]
    N = B * T
    c_pad = jnp.zeros((8, D), BF16).at[:B].set(c.astype(BF16))
    mod = ada_project(c_pad, w_ada, b_ada.reshape(L, 1, 6 * D))
    mods = [mod[l].reshape(8, 1, 6 * D) for l in range(L)]
    xf = x.reshape(N, D)
    u = modulate(xf, mods[0], 1, 0, T)
    shared = dict(w_in_rows=column_rows_view(w_in), n_layers=L, w_out=w_out, peer_wq=peer_wq,
                  w_gate=w_gate.astype(BF16), b_gate=b_gate.reshape(L, 1, 3 * D),
                  peer_u=peer_u.astype(BF16), peer_v=peer_v.astype(BF16))
    for l in range(L):
        p = _layer_params(l, shared, b_forget, cmp_pe, cmp_w1, cmp_w2, w_branch, peer_subkeys)
        y = _hybrid_mixer(u, p, B=B, T=T)
        xf, u, ut = postnorm(xf, y, mods[l], 2, ln1_g[l].reshape(1, D), ln1_b[l].reshape(1, D),
                             mods[l], 4, 3, T, transposed=True)
        y = _peer_ffn(u, ut, p)
        nxt = mods[min(l + 1, L - 1)]
        xf, u = postnorm(xf, y, mods[l], 5, ln2_g[l].reshape(1, D), ln2_b[l].reshape(1, D),
                         nxt, 1, 0, T)
    return xf.reshape(B, T, D)
```

```python
import functools

import numpy as np
import jax
import jax.numpy as jnp
from jax import lax
from jax.experimental import pallas as pl
from jax.experimental.pallas import tpu as pltpu

F32 = jnp.float32
BF16 = jnp.bfloat16

D_MODEL = 4096
HEAD_DIM = 128
NSA_HEADS = 12
NSA_KV_HEADS = 3
NSA_GROUP = NSA_HEADS // NSA_KV_HEADS
CMP_LEN = 32
CMP_STRIDE = 16
SLC_BLK = 64
N_SEL = 16
WIN = 512
FOX_HEADS = 8
DIL_PAIRS = ((128, 1), (512, 4), (2048, 16))
DIL_GROUPS = 3
DIL_HPG = 4
DIL_HEADS = DIL_GROUPS * DIL_HPG
ALIBI_HEADS = NSA_HEADS + DIL_HEADS
NSA_Q_W = NSA_HEADS * HEAD_DIM
NSA_KV_W = NSA_KV_HEADS * HEAD_DIM
FOX_W = FOX_HEADS * HEAD_DIM
DIL_W = DIL_HEADS * HEAD_DIM
OFF_NSA_KV = NSA_Q_W
OFF_NSA_GATE = OFF_NSA_KV + 6 * NSA_KV_W
OFF_FOX = OFF_NSA_GATE + 3 * NSA_HEADS
OFF_FOX_F = OFF_FOX + 3 * FOX_W
OFF_DIL = OFF_FOX_F + FOX_HEADS
IN_COLS = OFF_DIL + 3 * DIL_W
NSA_A_W = NSA_Q_W + 6 * NSA_KV_W
OUT_A = NSA_Q_W
OUT_B = FOX_W
OUT_C = DIL_HPG * HEAD_DIM
PEER_HEADS = 8
PEER_NKEYS = 128
PEER_EXPERTS = PEER_NKEYS ** 2
PEER_TOPK = 16
PEER_QDIM = 256
DEPTH = 2
ALPHA = (2.0 * DEPTH) ** 0.25
LN_EPS = 1e-5
NEG = -1e30
BIG = 1e30
SCALE = HEAD_DIM ** -0.5

LANES = 128
GATE_LANE0 = 0
FORGET_LANE0 = 3 * NSA_HEADS
VMEM_LIMIT = 56 * 1024 * 1024


def _cp(*sem):
    return pltpu.CompilerParams(dimension_semantics=sem, vmem_limit_bytes=VMEM_LIMIT)


def _alibi_slopes(n):
    return (np.float32(2.0) ** (-8.0 * np.arange(1, n + 1, dtype=np.float32) / n)).astype(np.float32)


def _gelu(x):
    return 0.5 * x * (1.0 + jnp.tanh(np.float32(np.sqrt(2.0 / np.pi)) * (x + np.float32(0.044715) * (x * x * x))))


def _split3(x):
    hi = x.astype(BF16)
    r1 = x - hi.astype(F32)
    mid = r1.astype(BF16)
    lo = (r1 - mid.astype(F32)).astype(BF16)
    return hi, mid, lo


def _dot3(x, m):
    hi, mid, lo = _split3(x)
    d = functools.partial(jnp.dot, preferred_element_type=F32)
    return (d(lo, m) + d(mid, m)) + d(hi, m)


def _dot_nt(a, b):
    return lax.dot_general(a, b, (((1,), (1,)), ((), ())), preferred_element_type=F32)


def _ada_kernel(c_ref, w_ref, b_ref, o_ref):
    o_ref[...] = jnp.dot(c_ref[...], w_ref[...].astype(BF16), preferred_element_type=F32) + b_ref[...]


def ada_project(c_pad, w_ada, b_ada, tn=512):
    L, D, N6 = w_ada.shape
    return pl.pallas_call(
        _ada_kernel,
        grid=(L, N6 // tn),
        in_specs=[pl.BlockSpec((8, D), lambda l, j: (0, 0)),
                  pl.BlockSpec((None, D, tn), lambda l, j: (l, 0, j)),
                  pl.BlockSpec((None, 1, tn), lambda l, j: (l, 0, j))],
        out_specs=pl.BlockSpec((None, 8, tn), lambda l, j: (l, 0, j)),
        out_shape=jax.ShapeDtypeStruct((L, 8, N6), F32),
        compiler_params=_cp("parallel", "parallel"),
        name="ada_project",
    )(c_pad, w_ada, b_ada)


def _modulate_kernel(x_ref, sc_ref, sh_ref, u_ref):
    u_ref[...] = (x_ref[...] * (1.0 + sc_ref[...]) + sh_ref[...]).astype(u_ref.dtype)


def modulate(x, mod3, sc_blk, sh_blk, T, tr=256):
    N, D = x.shape
    return pl.pallas_call(
        _modulate_kernel,
        grid=(N // tr,),
        in_specs=[pl.BlockSpec((tr, D), lambda i: (i, 0)),
                  pl.BlockSpec((None, 1, D), lambda i: (i * tr // T, 0, sc_blk)),
                  pl.BlockSpec((None, 1, D), lambda i: (i * tr // T, 0, sh_blk))],
        out_specs=pl.BlockSpec((tr, D), lambda i: (i, 0)),
        out_shape=jax.ShapeDtypeStruct((N, D), BF16),
        compiler_params=_cp("parallel"),
        name="modulate",
    )(x, mod3, mod3)


def _mm_kernel(a_ref, w_ref, o_ref, wb_ref):
    @pl.when(pl.program_id(1) == 0)
    def _():
        wb_ref[...] = w_ref[...].astype(BF16)

    o_ref[...] = jnp.dot(a_ref[...], wb_ref[...], preferred_element_type=F32).astype(o_ref.dtype)


def matmul(a, w, out_dtype, layer, tm=1024, tn=512):
    M, K = a.shape
    N = w.shape[2]
    assert M % tm == 0 and N % tn == 0
    return pl.pallas_call(
        _mm_kernel,
        grid=(N // tn, M // tm),
        in_specs=[pl.BlockSpec((tm, K), lambda j, i: (i, 0)),
                  pl.BlockSpec((None, K, tn), lambda j, i: (layer, 0, j))],
        out_specs=pl.BlockSpec((tm, tn), lambda j, i: (i, j)),
        out_shape=jax.ShapeDtypeStruct((M, N), out_dtype),
        scratch_shapes=[pltpu.VMEM((K, tn), BF16)],
        compiler_params=_cp("parallel", "arbitrary"),
        name="matmul",
    )(a, w)


def _gather_layer_rows(w_ref, wb_ref, layer, n_layers):
    tn, K = wb_ref.shape
    kt_n = K // LANES
    stride = kt_n * n_layers
    sub = 16
    for g in range(tn // sub):
        for kt in range(kt_n):
            first = (g * sub * kt_n + kt) * n_layers + layer
            wb_ref[g * sub:(g + 1) * sub, kt * LANES:(kt + 1) * LANES] = (
                w_ref[pl.ds(first, sub, stride=stride), :].astype(BF16))


def _mm_wt_kernel(a_ref, w_ref, o_ref, wb_ref, *, layer, n_layers):
    @pl.when(pl.program_id(1) == 0)
    def _():
        _gather_layer_rows(w_ref, wb_ref, layer, n_layers)

    o_ref[...] = _dot_nt(a_ref[...], wb_ref[...]).astype(o_ref.dtype)


def column_rows_view(w):
    L, K, N = w.shape
    return w.transpose(2, 0, 1).reshape(N, L, K // LANES, LANES).transpose(0, 2, 1, 3).reshape(
        N * (K // LANES) * L, LANES)


def _column_window(n_layers, K, tn, start_col):
    rows_per_col = (K // LANES) * n_layers
    return pl.BlockSpec((pl.Element(tn * rows_per_col), pl.Element(LANES)),
                        lambda *ids: (start_col(*ids) * rows_per_col, 0), pipeline_mode=pl.Buffered(1))


def matmul_wt(a, w_rows, out_dtype, n_layers, layer, col0, n_cols, tm=512, tn=768):
    M, K = a.shape
    assert M % tm == 0 and n_cols % tn == 0
    return pl.pallas_call(
        functools.partial(_mm_wt_kernel, layer=layer, n_layers=n_layers),
        grid=(n_cols // tn, M // tm),
        in_specs=[pl.BlockSpec((tm, K), lambda j, i: (i, 0)),
                  _column_window(n_layers, K, tn, lambda j, i: col0 + j * tn)],
        out_specs=pl.BlockSpec((tm, tn), lambda j, i: (i, j)),
        out_shape=jax.ShapeDtypeStruct((M, n_cols), out_dtype),
        scratch_shapes=[pltpu.VMEM((tn, K), BF16)],
        compiler_params=_cp("parallel", "arbitrary"),
        name="matmul_wt",
    )(a, w_rows)


def _small_proj_kernel(a_ref, wg_ref, wf_ref, o_ref, wbg_ref, wbf_ref, *, layer, n_layers):
    @pl.when(pl.program_id(0) == 0)
    def _():
        _gather_layer_rows(wg_ref, wbg_ref, layer, n_layers)
        _gather_layer_rows(wf_ref, wbf_ref, layer, n_layers)

    a = a_ref[...]
    lane = lax.broadcasted_iota(jnp.int32, o_ref.shape, 1)
    o_ref[...] = jnp.where(lane < FORGET_LANE0, _dot_nt(a, wbg_ref[...]), _dot_nt(a, wbf_ref[...]))


def small_projection(a, w_rows, n_layers, layer, tm=1024):
    M, K = a.shape
    return pl.pallas_call(
        functools.partial(_small_proj_kernel, layer=layer, n_layers=n_layers),
        grid=(M // tm,),
        in_specs=[pl.BlockSpec((tm, K), lambda i: (i, 0)),
                  _column_window(n_layers, K, LANES, lambda i: OFF_NSA_GATE - GATE_LANE0),
                  _column_window(n_layers, K, LANES, lambda i: OFF_FOX_F - FORGET_LANE0)],
        out_specs=pl.BlockSpec((tm, LANES), lambda i: (i, 0)),
        out_shape=jax.ShapeDtypeStruct((M, LANES), F32),
        scratch_shapes=[pltpu.VMEM((LANES, K), BF16), pltpu.VMEM((LANES, K), BF16)],
        compiler_params=_cp("arbitrary"),
        name="small_projection",
    )(a, w_rows, w_rows)


HEADS_PER_STEP = 4


def _head_cols(i):
    return slice(i * HEAD_DIM, (i + 1) * HEAD_DIM)


def _flash(qs, load_kv, lo, hi, chunk_fn, score_fns):
    tq = qs[0].shape[0]

    def body(c, carry):
        shared = chunk_fn(c)
        kvs = [load_kv(c, i) for i in range(len(qs))]
        raws = [_dot_nt(q, kvs[i][0]) for i, q in enumerate(qs)]
        mids = []
        for i, raw in enumerate(raws):
            m, l, _ = carry[i]
            s = score_fns[i](raw, c, shared)
            m_new = jnp.maximum(m, jnp.max(s, axis=1, keepdims=True))
            alpha = jnp.exp(m - m_new)
            p = jnp.exp(s - m_new)
            mids.append((m_new, alpha, alpha * l + jnp.sum(p, axis=1, keepdims=True), p))
        out = []
        for i, (m_new, alpha, l, p) in enumerate(mids):
            v = kvs[i][1]
            acc = alpha * carry[i][2] + jnp.dot(p.astype(v.dtype), v, preferred_element_type=F32)
            out.append((m_new, l, acc))
        return tuple(out)

    one = (jnp.full((tq, 1), NEG, F32), jnp.zeros((tq, 1), F32), jnp.zeros((tq, HEAD_DIM), F32))
    return lax.fori_loop(lo, hi, body, tuple(one for _ in qs))


def _flash_t(qs, load_kv, lo, hi, chunk_fn, score_fns):
    tq = qs[0].shape[0]

    def body(c, stats):
        shared = chunk_fn(c)
        raws = [_dot_nt(load_kv(c, i)[0], q) for i, q in enumerate(qs)]
        mids = []
        for i, raw in enumerate(raws):
            m, l, _ = stats[i]
            s = score_fns[i](raw, c, shared)
            m_new = jnp.maximum(m, jnp.max(s, axis=0, keepdims=True))
            alpha = jnp.exp(m - m_new)
            p = jnp.exp(s - m_new)
            mids.append((m_new, alpha, alpha * l + jnp.sum(p, axis=0, keepdims=True), p))
        out = []
        for i, (m_new, alpha, l, p) in enumerate(mids):
            v = load_kv(c, i)[1]
            pv = lax.dot_general(v, p.astype(v.dtype), (((0,), (0,)), ((), ())), preferred_element_type=F32)
            out.append((m_new, l, alpha * stats[i][2] + pv))
        return tuple(out)

    one = (jnp.full((1, tq), NEG, F32), jnp.zeros((1, tq), F32), jnp.zeros((HEAD_DIM, tq), F32))
    return lax.fori_loop(lo, hi, body, tuple(one for _ in qs))


def _alibi_kernel(slope_ref, q_ref, k_ref, v_ref, *rest, tq, tk, max_dist, shared_kv, with_sel, with_lse):
    rest = list(rest)
    sel_ref = rest.pop(0) if with_sel else None
    o_ref = rest.pop(0)
    lse_ref = rest.pop(0) if with_lse else None
    hb = pl.program_id(1)
    q0 = pl.program_id(3) * tq
    nh = HEADS_PER_STEP
    rc = lax.broadcasted_iota(jnp.int32, (tk, tq), 1) - lax.broadcasted_iota(jnp.int32, (tk, tq), 0)
    if with_sel:
        sel_t = sel_ref[...].astype(F32).T.astype(BF16)
        lane_blk = lax.broadcasted_iota(jnp.int32, (tk, LANES), 1)
        key_blk = lax.broadcasted_iota(jnp.int32, (tk, LANES), 0) // SLC_BLK

    def valid_fn(c):
        dist = rc + (q0 - c * tk)
        valid = dist >= 0
        if max_dist is not None:
            valid = valid & (dist <= max_dist)
        if with_sel:
            expand = (lane_blk == key_blk + (hb * (LANES // 4) + c * (tk // SLC_BLK))).astype(BF16)
            valid = valid & (jnp.dot(expand, sel_t, preferred_element_type=F32) > 0.5)
        return dist.astype(F32), valid

    def make_score(i):
        slope = slope_ref[hb * nh + i]

        def score_fn(s, c, shared):
            dist_f, valid = shared
            return jnp.where(valid, s * SCALE - slope * dist_f, NEG)
        return score_fn

    def load_kv(c, i):
        off = pl.multiple_of(c * tk, tk)
        cols = _head_cols(0 if shared_kv else i)
        return k_ref[pl.ds(off, tk), cols], v_ref[pl.ds(off, tk), cols]

    hi = (q0 + tq - 1) // tk + 1
    lo = 0 if max_dist is None else jnp.maximum(q0 - max_dist, 0) // tk
    res = _flash_t([q_ref[:, _head_cols(i)] for i in range(nh)], load_kv, lo, hi, valid_fn,
                   [make_score(i) for i in range(nh)])
    for i, (m, l, acc_t) in enumerate(res):
        o_ref[:, _head_cols(i)] = (acc_t / l).T.astype(o_ref.dtype)
        if with_lse:
            lse_ref[:, _head_cols(i)] = jnp.broadcast_to(m + jnp.log(l), (HEAD_DIM, tq)).T


def alibi_attention(qa, ka, va, slopes, *, B, Ts, n_sub, n_head_blocks, shared_kv, q_col, k_col, v_col,
                    max_dist, tq, tk, sel=None, with_lse=False):
    tq = min(tq, Ts)
    tk = min(tk, Ts)
    nq = Ts // tq
    rows = B * Ts
    wq = HEADS_PER_STEP * HEAD_DIM
    wkv = HEAD_DIM if shared_kv else wq
    in_specs = [pl.BlockSpec((tq, wq), lambda b, h, r, i, s: (b * nq + i, q_col(h, r))),
                pl.BlockSpec((Ts, wkv), lambda b, h, r, i, s: (b, k_col(h, r))),
                pl.BlockSpec((Ts, wkv), lambda b, h, r, i, s: (b, v_col(h, r)))]
    args = [qa, ka, va]
    if sel is not None:
        in_specs.append(pl.BlockSpec((tq, LANES), lambda b, h, r, i, s: (b * nq + i, 0)))
        args.append(sel)
    o_spec = pl.BlockSpec((tq, wq), lambda b, h, r, i, s: (b * nq + i, r * n_head_blocks + h))
    o_shape = jax.ShapeDtypeStruct((rows, n_sub * n_head_blocks * wq), F32)
    kern = functools.partial(_alibi_kernel, tq=tq, tk=tk, max_dist=max_dist, shared_kv=shared_kv,
                             with_sel=sel is not None, with_lse=with_lse)
    return pl.pallas_call(
        kern,
        grid_spec=pltpu.PrefetchScalarGridSpec(
            num_scalar_prefetch=1, grid=(B, n_head_blocks, n_sub, nq),
            in_specs=in_specs,
            out_specs=[o_spec, o_spec] if with_lse else o_spec),
        out_shape=[o_shape, o_shape] if with_lse else o_shape,
        compiler_params=_cp("parallel", "parallel", "parallel", "parallel"),
        name="alibi_attention",
    )(jnp.asarray(slopes, F32), *args)


def _fox_kernel(q_ref, k_ref, v_ref, fq_ref, fk_ref, o_ref, *, tq, tk):
    hb = pl.program_id(1)
    q0 = pl.program_id(2) * tq
    nh = HEADS_PER_STEP
    lane = lax.broadcasted_iota(jnp.int32, (tq, LANES), 1)
    f_tok = fq_ref[...]
    rc = lax.broadcasted_iota(jnp.int32, (tq, tk), 0) - lax.broadcasted_iota(jnp.int32, (tq, tk), 1)

    def make_score(i):
        fq = jnp.sum(jnp.where(lane == FORGET_LANE0 + hb * nh + i, f_tok, 0.0), axis=1, keepdims=True)

        def score_fn(s, c, causal):
            return jnp.where(causal, (s * SCALE + fq) - fk_ref[i, pl.ds(c, 1), :], NEG)
        return score_fn

    def load_kv(c, i):
        off = pl.multiple_of(c * tk, tk)
        return k_ref[pl.ds(off, tk), _head_cols(i)], v_ref[pl.ds(off, tk), _head_cols(i)]

    hi = (q0 + tq - 1) // tk + 1
    res = _flash([q_ref[:, _head_cols(i)] for i in range(nh)], load_kv, 0, hi,
                 lambda c: rc + (q0 - c * tk) >= 0, [make_score(i) for i in range(nh)])
    for i, (_, l, acc) in enumerate(res):
        o_ref[:, _head_cols(i)] = (acc / l).astype(o_ref.dtype)


def fox_attention(proj_fox, f_tok, f_rows, *, B, T, tq=256, tk=256):
    nq = T // tq
    nh = HEADS_PER_STEP
    nhb = FOX_HEADS // nh
    wq = nh * HEAD_DIM
    return pl.pallas_call(
        functools.partial(_fox_kernel, tq=tq, tk=tk),
        grid=(B, nhb, nq),
        in_specs=[pl.BlockSpec((tq, wq), lambda b, h, i: (b * nq + i, h)),
                  pl.BlockSpec((T, wq), lambda b, h, i: (b, nhb + h)),
                  pl.BlockSpec((T, wq), lambda b, h, i: (b, 2 * nhb + h)),
                  pl.BlockSpec((tq, LANES), lambda b, h, i: (b * nq + i, 0)),
                  pl.BlockSpec((nh, T // tk, tk), lambda b, h, i: (b * nhb + h, 0, 0))],
        out_specs=pl.BlockSpec((tq, wq), lambda b, h, i: (b * nq + i, h)),
        out_shape=jax.ShapeDtypeStruct((B * T, FOX_W), BF16),
        compiler_params=_cp("parallel", "parallel", "parallel"),
        name="fox_attention",
    )(proj_fox, proj_fox, proj_fox, f_tok, f_rows)


def _cumsum_kernel(x_ref, b_ref, o_ref, *, T, tr):
    i = pl.program_id(1)
    z = x_ref[...] + b_ref[...]
    logf = jnp.minimum(z, 0.0) - jnp.log1p(jnp.exp(-jnp.abs(z)))
    row = lax.broadcasted_iota(jnp.int32, (tr, T), 0) + i * tr
    col = lax.broadcasted_iota(jnp.int32, (tr, T), 1)
    o_ref[...] = _dot3_lhs((row >= col).astype(BF16), logf)


def _dot3_lhs(m, x):
    hi, mid, lo = _split3(x)
    d = functools.partial(jnp.dot, preferred_element_type=F32)
    return (d(m, lo) + d(m, mid)) + d(m, hi)


def forget_cumsum(small, bias_row, *, B, T, tr=256):
    return pl.pallas_call(
        functools.partial(_cumsum_kernel, T=T, tr=tr),
        grid=(B, T // tr),
        in_specs=[pl.BlockSpec((T, LANES), lambda b, i: (b, 0)),
                  pl.BlockSpec((1, LANES), lambda b, i: (0, 0))],
        out_specs=pl.BlockSpec((tr, LANES), lambda b, i: (b * (T // tr) + i, 0)),
        out_shape=jax.ShapeDtypeStruct((B * T, LANES), F32),
        compiler_params=_cp("parallel", "parallel"),
        name="forget_cumsum",
    )(small, bias_row)


def _compress_kernel(x_ref, pe_ref, w1_ref, w2_ref, o_ref):
    x = x_ref[...].astype(F32)
    y0 = jnp.dot((x + pe_ref[0]).astype(BF16), w1_ref[0], preferred_element_type=F32)
    y1 = jnp.dot((x + pe_ref[1]).astype(BF16), w1_ref[1], preferred_element_type=F32)
    n = y1.shape[0]
    hmid = _gelu(y0 + pltpu.roll(y1, n - 1, 0))
    o_ref[...] = jnp.dot(hmid.astype(BF16), w2_ref[...], preferred_element_type=F32).astype(o_ref.dtype)


def compress_blocks(chunks, pe, w1, w2):
    _, G, nch, cw = chunks.shape
    return pl.pallas_call(
        _compress_kernel,
        grid=(2, G),
        in_specs=[pl.BlockSpec((None, None, nch, cw), lambda a, g: (a, g, 0, 0)),
                  pl.BlockSpec((None, 2, 1, cw), lambda a, g: (a, 0, 0, 0)),
                  pl.BlockSpec((None, 2, cw, HEAD_DIM), lambda a, g: (a, 0, 0, 0)),
                  pl.BlockSpec((None, HEAD_DIM, HEAD_DIM), lambda a, g: (a, 0, 0))],
        out_specs=pl.BlockSpec((None, None, nch, HEAD_DIM), lambda a, g: (a, g, 0, 0)),
        out_shape=jax.ShapeDtypeStruct((2, G, nch, HEAD_DIM), BF16),
        compiler_params=_cp("parallel", "parallel"),
        name="compress_blocks",
    )(chunks, pe, w1, w2)


def _cmp_kernel(slope_ref, q_ref, kc_ref, vc_ref, smat_ref, o_ref, sel_ref, *, tq, n_cmp, n_slc):
    q0 = pl.program_id(1) * tq
    nk = kc_ref.shape[1]
    pos = lax.broadcasted_iota(jnp.int32, (tq, nk), 0) + q0
    n_idx = lax.broadcasted_iota(jnp.int32, (tq, nk), 1)
    dist_c = pos - (n_idx * CMP_STRIDE + (CMP_LEN - 1))
    valid_c = (dist_c >= 0) & (n_idx < n_cmp)
    dist_f = dist_c.astype(F32)
    p_slc = jnp.zeros((tq, LANES), F32)
    for kv in range(NSA_KV_HEADS):
        kc = kc_ref[kv]
        vc = vc_ref[kv]
        imp = jnp.zeros((tq, nk), F32)
        for g in range(NSA_GROUP):
            hh = kv * NSA_GROUP + g
            q = q_ref[:, hh * HEAD_DIM:(hh + 1) * HEAD_DIM]
            s = _dot_nt(q, kc) * SCALE - slope_ref[hh] * dist_f
            s = jnp.where(valid_c, s, NEG)
            e = jnp.where(valid_c, jnp.exp(s - jnp.max(s, axis=1, keepdims=True)), 0.0)
            p = e / jnp.maximum(jnp.sum(e, axis=1, keepdims=True), 1e-30)
            o_ref[:, hh * HEAD_DIM:(hh + 1) * HEAD_DIM] = jnp.dot(
                p.astype(BF16), vc, preferred_element_type=F32)
            imp = imp + p
        p_slc = p_slc + _dot3(imp, smat_ref[kv])
    grp = LANES // 4
    p_t = p_slc.T
    j = lax.broadcasted_iota(jnp.int32, (grp, tq), 0)
    qblk = (lax.broadcasted_iota(jnp.int32, (grp, tq), 1) + q0) // SLC_BLK
    forced = (j == 0) | (j == qblk) | (j == qblk - 1)
    causal = j <= qblk
    chosen = []
    for kv in range(NSA_KV_HEADS):
        score = jnp.where(forced, BIG, jnp.where(causal, p_t[kv * grp:(kv + 1) * grp], NEG))
        rank = jnp.zeros((grp, tq), jnp.int32)
        for k in range(n_slc):
            row = score[k:k + 1]
            ahead = (row > score) | ((row == score) & (k < j))
            rank = rank + ahead.astype(jnp.int32)
        chosen.append(jnp.where(causal & (rank < min(N_SEL, n_slc)), 1.0, 0.0))
    chosen.append(jnp.zeros((LANES - NSA_KV_HEADS * grp, tq), F32))
    sel_ref[...] = jnp.concatenate(chosen, axis=0).T.astype(sel_ref.dtype)


def cmp_attention(proj_a, kvc, smat, slopes, *, B, T, tq=256):
    nq = T // tq
    nch = kvc.shape[2]
    n_cmp = (T - CMP_LEN) // CMP_STRIDE + 1
    n_slc = T // SLC_BLK
    kvc4 = kvc.reshape(2, B, NSA_KV_HEADS, nch, HEAD_DIM)
    return pl.pallas_call(
        functools.partial(_cmp_kernel, tq=tq, n_cmp=n_cmp, n_slc=n_slc),
        grid_spec=pltpu.PrefetchScalarGridSpec(
            num_scalar_prefetch=1, grid=(B, nq),
            in_specs=[pl.BlockSpec((tq, NSA_Q_W), lambda b, i, s: (b * nq + i, 0)),
                      pl.BlockSpec((None, None, NSA_KV_HEADS, nch, HEAD_DIM), lambda b, i, s: (0, b, 0, 0, 0)),
                      pl.BlockSpec((None, None, NSA_KV_HEADS, nch, HEAD_DIM), lambda b, i, s: (1, b, 0, 0, 0)),
                      pl.BlockSpec((NSA_KV_HEADS, nch, LANES), lambda b, i, s: (0, 0, 0))],
            out_specs=[pl.BlockSpec((tq, NSA_Q_W), lambda b, i, s: (b * nq + i, 0)),
                       pl.BlockSpec((tq, LANES), lambda b, i, s: (b * nq + i, 0))]),
        out_shape=[jax.ShapeDtypeStruct((B * T, NSA_Q_W), F32),
                   jax.ShapeDtypeStruct((B * T, LANES), BF16)],
        compiler_params=_cp("parallel", "parallel"),
        name="cmp_attention",
    )(jnp.asarray(slopes, F32), proj_a, kvc4, kvc4, smat)


def _pool_matrix(nch, n_slc):
    r = SLC_BLK // CMP_STRIDE
    c_over = CMP_LEN // CMP_STRIDE
    n_cmp_max = nch
    m = np.zeros((NSA_KV_HEADS, nch, LANES), np.float32)
    for kv in range(NSA_KV_HEADS):
        for j in range(n_slc):
            for a in range(r):
                for b in range(c_over):
                    i = r * j + a + b
                    if i < n_cmp_max:
                        m[kv, i, kv * (LANES // 4) + j] += 1.0
    return m


def _nsa_merge_kernel(oc_ref, os_ref, ow_ref, g_ref, e_ref, o_ref):
    sig = jax.nn.sigmoid(g_ref[...])
    out = _dot3(sig, e_ref[0]) * oc_ref[...]
    out = out + _dot3(sig, e_ref[1]) * os_ref[...]
    out = out + _dot3(sig, e_ref[2]) * ow_ref[...]
    o_ref[...] = out.astype(o_ref.dtype)


def nsa_merge(o_cmp, o_slc, o_win, small, tr=256):
    N = o_cmp.shape[0]
    e = np.zeros((3, LANES, NSA_Q_W), np.float32)
    for br in range(3):
        for h in range(NSA_HEADS):
            e[br, GATE_LANE0 + h * 3 + br, h * HEAD_DIM:(h + 1) * HEAD_DIM] = 1.0
    row = pl.BlockSpec((tr, NSA_Q_W), lambda i: (i, 0))
    return pl.pallas_call(
        _nsa_merge_kernel,
        grid=(N // tr,),
        in_specs=[row, row, row,
                  pl.BlockSpec((tr, LANES), lambda i: (i, 0)),
                  pl.BlockSpec((3, LANES, NSA_Q_W), lambda i: (0, 0, 0))],
        out_specs=row,
        out_shape=jax.ShapeDtypeStruct((N, NSA_Q_W), BF16),
        compiler_params=_cp("parallel"),
        name="nsa_merge",
    )(o_cmp, o_slc, o_win, small, jnp.asarray(e, BF16))


def _dil_merge_kernel(o0, o1, o2, l0, l1, l2, o_ref):
    a, b, c = l0[...], l1[...], l2[...]
    m = jnp.maximum(jnp.maximum(a, b), c)
    ea, eb, ec = jnp.exp(a - m), jnp.exp(b - m), jnp.exp(c - m)
    z = ea + eb + ec
    o_ref[...] = ((ea / z) * o0[...] + (eb / z) * o1[...] + (ec / z) * o2[...]).astype(o_ref.dtype)


def dil_merge(outs, lses, tr=512):
    N, W = outs[0].shape
    row = pl.BlockSpec((tr, W), lambda i: (i, 0))
    return pl.pallas_call(
        _dil_merge_kernel,
        grid=(N // tr,),
        in_specs=[row] * 6,
        out_specs=row,
        out_shape=jax.ShapeDtypeStruct((N, W), BF16),
        compiler_params=_cp("parallel"),
        name="dil_merge",
    )(*outs, *lses)


def _branch_kernel(u_ref, wga, wgb, wgc, bga, bgb, bgc, oa, ob, oc, wa, wb, wc, z_ref):
    d = functools.partial(jnp.dot, preferred_element_type=F32)
    u = u_ref[...]
    z = jax.nn.sigmoid(d(u, wga[...]) + bga[...]) * d(oa[...], wa[...])
    z = z + jax.nn.sigmoid(d(u, wgb[...]) + bgb[...]) * d(ob[...], wb[...])
    z = z + jax.nn.sigmoid(d(u, wgc[...]) + bgc[...]) * d(oc[...], wc[...])
    z_ref[...] = z.astype(z_ref.dtype)


def gated_branches(u, w_gate, b_gate, layer, o_a, o_b, o_c, wb_a, wb_b, wb_c, tm=512, tn=512):
    N, D = u.shape
    nj = D // tn

    def wg(k):
        return pl.BlockSpec((None, D, tn), lambda j, i: (layer, 0, k * nj + j))

    def bg(k):
        return pl.BlockSpec((None, 1, tn), lambda j, i: (layer, 0, k * nj + j))

    def act(w):
        return pl.BlockSpec((tm, w), lambda j, i: (i, 0))

    def wup(w):
        return pl.BlockSpec((w, tn), lambda j, i: (0, j))

    return pl.pallas_call(
        _branch_kernel,
        grid=(nj, N // tm),
        in_specs=[act(D), wg(0), wg(1), wg(2), bg(0), bg(1), bg(2),
                  act(OUT_A), act(OUT_B), act(OUT_C), wup(OUT_A), wup(OUT_B), wup(OUT_C)],
        out_specs=pl.BlockSpec((tm, tn), lambda j, i: (i, j)),
        out_shape=jax.ShapeDtypeStruct((N, D), BF16),
        compiler_params=_cp("parallel", "parallel"),
        name="gated_branches",
    )(u, w_gate, w_gate, w_gate, b_gate, b_gate, b_gate, o_a, o_b, o_c, wb_a, wb_b, wb_c)


def _postnorm_kernel(x_ref, y_ref, g_ref, gam_ref, bet_ref, sc_ref, sh_ref, xo_ref, u_ref, *ut_ref):
    h = ALPHA * x_ref[...] + (1.0 + g_ref[...]) * y_ref[...]
    mu = jnp.mean(h, axis=-1, keepdims=True)
    hc = h - mu
    var = jnp.mean(hc * hc, axis=-1, keepdims=True)
    xn = hc * lax.rsqrt(var + LN_EPS) * gam_ref[...] + bet_ref[...]
    xo_ref[...] = xn
    u = xn * (1.0 + sc_ref[...]) + sh_ref[...]
    u_ref[...] = u.astype(u_ref.dtype)
    if ut_ref:
        ut_ref[0][...] = u.T.astype(u_ref.dtype)


def postnorm(x, y, mod_g, g_blk, gamma, beta, mod_n, sc_blk, sh_blk, T, tr=256, transposed=False):
    N, D = x.shape
    row = pl.BlockSpec((tr, D), lambda i: (i, 0))
    vec = pl.BlockSpec((1, D), lambda i: (0, 0))

    def mod(blk):
        return pl.BlockSpec((None, 1, D), lambda i: (i * tr // T, 0, blk))

    out_specs = [row, row]
    out_shape = [jax.ShapeDtypeStruct((N, D), F32), jax.ShapeDtypeStruct((N, D), BF16)]
    if transposed:
        out_specs.append(pl.BlockSpec((D, tr), lambda i: (0, i)))
        out_shape.append(jax.ShapeDtypeStruct((D, N), BF16))
    return pl.pallas_call(
        _postnorm_kernel,
        grid=(N // tr,),
        in_specs=[row, row, mod(g_blk), vec, vec, mod(sc_blk), mod(sh_blk)],
        out_specs=out_specs,
        out_shape=out_shape,
        compiler_params=_cp("parallel"),
        name="postnorm",
    )(x, y, mod_g, gamma, beta, mod_n, mod_n)


def _topk_rows(s, k, break_ties):
    rows, tn = s.shape
    iota = lax.broadcasted_iota(jnp.int32, (rows, tn), 0)
    rank = jnp.full((rows, tn), k, jnp.int32)
    cur = s
    tops = []
    for r in range(k):
        m = jnp.max(cur, axis=0, keepdims=True)
        hit = cur == m
        if break_ties:
            hit = iota == jnp.min(jnp.where(hit, iota, rows), axis=0, keepdims=True)
        rank = jnp.where(hit, r, rank)
        cur = jnp.where(hit, -jnp.inf, cur)
        tops.append(m)
    return rank, tops


def _merge_topk(top1, top2, k):
    t1 = _stack_rows(top1)
    tn = t1.shape[1]
    iota = lax.broadcasted_iota(jnp.int32, (k, tn), 0)
    taken = jnp.zeros((k, tn), jnp.int32)
    front = t1 + top2[0]
    m0 = top1[0] + top2[0]
    z = jnp.zeros((1, tn), F32)
    for _ in range(k):
        m = jnp.max(front, axis=0, keepdims=True)
        hit = iota == jnp.min(jnp.where(front == m, iota, k), axis=0, keepdims=True)
        z = z + jnp.exp(m - m0)
        nxt = jnp.sum(jnp.where(hit, taken, 0), axis=0, keepdims=True) + 1
        follower = jnp.full((1, tn), -jnp.inf, F32)
        for rb in range(1, k):
            follower = jnp.where(nxt == rb, top2[rb], follower)
        t1_hit = jnp.sum(jnp.where(hit, t1, 0.0), axis=0, keepdims=True)
        taken = taken + hit.astype(jnp.int32)
        front = jnp.where(hit, t1_hit + follower, front)
    return taken.astype(F32), z


def _stack_rows(rows_list):
    n = len(rows_list)
    tn = rows_list[0].shape[1]
    iota = lax.broadcasted_iota(jnp.int32, (n, tn), 0)
    out = jnp.zeros((n, tn), rows_list[0].dtype)
    for r, row in enumerate(rows_list):
        out = jnp.where(iota == r, row, out)
    return out


def _peer_select_kernel(q_ref, sk_ref, e1_ref, l1_ref, e2_ref, r2_ref, *, cpt):
    K = PEER_TOPK
    half = PEER_QDIM // 2
    s1 = _dot_nt(sk_ref[0], q_ref[:, :half])
    s2 = _dot_nt(sk_ref[1], q_ref[:, half:])

    def emit(break_ties):
        rank1, top1 = _topk_rows(s1, K, break_ties)
        rank2, top2 = _topk_rows(s2, K, break_ties)
        row_len, z = _merge_topk(top1, top2, K)
        l1 = jnp.zeros(s1.shape, F32)
        for ra in range(K):
            l1 = jnp.where(rank1 == ra, row_len[ra:ra + 1], l1)
        e1 = jnp.where(rank1 < K, jnp.exp(s1 - top1[0]) / z, 0.0)
        e2 = jnp.where(rank2 < K, jnp.exp(s2 - top2[0]), 0.0)
        for j in range(PEER_NKEYS // cpt):
            e1_ref[j] = e1[j * cpt:(j + 1) * cpt]
            l1_ref[j] = l1[j * cpt:(j + 1) * cpt]
        e2_ref[...] = e2.astype(BF16)
        r2_ref[...] = rank2.astype(BF16)
        members = (rank1 < K).astype(F32) + (rank2 < K).astype(F32)
        return jnp.sum(jnp.abs(jnp.sum(members, axis=0, keepdims=True) - 2.0 * K))

    tied = emit(False)

    @pl.when(tied != 0.0)
    def _():
        emit(True)


def peer_select(q, subkeys, *, cpt, tn=256):
    N = q.shape[0]
    H = PEER_HEADS
    nt = PEER_NKEYS // cpt
    s1 = pl.BlockSpec((nt, None, cpt, tn), lambda i, h: (0, h, 0, i))
    s2 = pl.BlockSpec((None, PEER_NKEYS, tn), lambda i, h: (h, 0, i))
    sh1 = jax.ShapeDtypeStruct((nt, H, cpt, N), F32)
    sh2 = jax.ShapeDtypeStruct((H, PEER_NKEYS, N), BF16)
    return pl.pallas_call(
        functools.partial(_peer_select_kernel, cpt=cpt),
        grid=(N // tn, H),
        in_specs=[pl.BlockSpec((tn, PEER_QDIM), lambda i, h: (i, h)),
                  pl.BlockSpec((None, 2, PEER_NKEYS, PEER_QDIM // 2), lambda i, h: (h, 0, 0, 0))],
        out_specs=[s1, s1, s2, s2],
        out_shape=[sh1, sh1, sh2, sh2],
        compiler_params=_cp("parallel", "parallel"),
        name="peer_select",
    )(q, subkeys)


def _peer_dense_kernel(ut_ref, tu_ref, tv_ref, e1_ref, l1_ref, e2_ref, r2_ref, o_ref, a_s, *, cpt, tnc):
    @pl.when(pl.program_id(1) == 0)
    def _():
        o_ref[...] = jnp.zeros_like(o_ref)

    nk = PEER_NKEYS
    cph = cpt // 2
    tm = ut_ref.shape[1]
    for k in range(2):
        rows = slice(k * cph * nk, (k + 1) * cph * nk)
        for hc in range(2):
            tc = slice(hc * (tm // 2), (hc + 1) * (tm // 2))
            ht = jnp.dot(tu_ref[rows, :], ut_ref[:, tc], preferred_element_type=F32)
            for c2 in range(cph):
                cc = k * cph + c2
                w = None
                for h in range(PEER_HEADS):
                    hit = r2_ref[h, :, tc] < l1_ref[h, cc:cc + 1, tc].astype(BF16)
                    wh = jnp.where(hit, e2_ref[h, :, tc], jnp.zeros((), BF16)) * e1_ref[h, cc:cc + 1, tc].astype(BF16)
                    w = wh if w is None else w + wh
                a_s[k, tc, c2 * nk:(c2 + 1) * nk] = (w.astype(F32) * _gelu(ht[c2 * nk:(c2 + 1) * nk])).T.astype(BF16)
    for k in range(2):
        rows = slice(k * cph * nk, (k + 1) * cph * nk)
        for nt in range(o_ref.shape[1] // tnc):
            cs = slice(nt * tnc, (nt + 1) * tnc)
            o_ref[:, cs] += jnp.dot(a_s[k], tv_ref[rows, cs], preferred_element_type=F32)


def peer_dense(ut, tu, tv, layer, e1, l1, e2, r2, *, cpt, tm=512):
    D, N = ut.shape
    E = tu.shape[1]
    te = cpt * PEER_NKEYS
    H = PEER_HEADS
    once = pl.Buffered(1)
    s1 = pl.BlockSpec((None, H, cpt, tm), lambda i, j: (j, 0, 0, i))
    s2 = pl.BlockSpec((H, PEER_NKEYS, tm), lambda i, j: (0, 0, i), pipeline_mode=once)
    tab = pl.BlockSpec((None, te, D), lambda i, j: (layer, j, 0))
    return pl.pallas_call(
        functools.partial(_peer_dense_kernel, cpt=cpt, tnc=512),
        grid=(N // tm, E // te),
        in_specs=[pl.BlockSpec((D, tm), lambda i, j: (0, i), pipeline_mode=once), tab, tab, s1, s1, s2, s2],
        out_specs=pl.BlockSpec((tm, D), lambda i, j: (i, 0)),
        out_shape=jax.ShapeDtypeStruct((N, D), F32),
        scratch_shapes=[pltpu.VMEM((2, tm, te // 2), BF16)],
        compiler_params=_cp("parallel", "arbitrary"),
        name="peer_dense",
    )(ut, tu, tv, e1, l1, e2, r2)


def _hybrid_mixer(u, p, *, B, T):
    N = B * T
    sl = _alibi_slopes(ALIBI_HEADS)
    nsa_slopes = sl[0::2]
    dil_slopes = sl[1::2].reshape(DIL_GROUPS, DIL_HPG)

    lyr = p["layer"]
    nl = p["n_layers"]
    proj_a = matmul_wt(u, p["w_in_rows"], BF16, nl, lyr, 0, NSA_A_W)
    small = small_projection(u, p["w_in_rows"], nl, lyr)
    proj_fox = matmul_wt(u, p["w_in_rows"], BF16, nl, lyr, OFF_FOX, 3 * FOX_W)
    proj_dil = matmul_wt(u, p["w_in_rows"], BF16, nl, lyr, OFF_DIL, 3 * DIL_W)

    nch = T // CMP_STRIDE
    kv_cmp = proj_a[:, OFF_NSA_KV:OFF_NSA_KV + 2 * NSA_KV_W].reshape(B, T, 2, NSA_KV_HEADS, HEAD_DIM)
    chunks = kv_cmp.transpose(2, 0, 3, 1, 4).reshape(2, B * NSA_KV_HEADS, nch, CMP_STRIDE * HEAD_DIM)
    kvc = compress_blocks(chunks, p["cmp_pe"], p["cmp_w1"], p["cmp_w2"])
    smat = jnp.asarray(_pool_matrix(nch, T // SLC_BLK), BF16)
    o_cmp, sel = cmp_attention(proj_a, kvc, smat, nsa_slopes, B=B, T=T)
    qh = NSA_HEADS
    kvh = NSA_KV_HEADS
    assert NSA_GROUP == HEADS_PER_STEP and DIL_HPG == HEADS_PER_STEP
    nsa_kw = dict(B=B, Ts=T, n_sub=1, n_head_blocks=kvh, shared_kv=True,
                  q_col=lambda h, r: h, tq=128, tk=256)
    o_slc = alibi_attention(proj_a, proj_a, proj_a, nsa_slopes, max_dist=None, sel=sel,
                            k_col=lambda h, r: qh + 2 * kvh + h,
                            v_col=lambda h, r: qh + 3 * kvh + h, **nsa_kw)
    o_win = alibi_attention(proj_a, proj_a, proj_a, nsa_slopes, max_dist=WIN - 1,
                            k_col=lambda h, r: qh + 4 * kvh + h,
                            v_col=lambda h, r: qh + 5 * kvh + h, **nsa_kw)
    o_a = nsa_merge(o_cmp, o_slc, o_win, small)

    f_tok = forget_cumsum(small, p["b_forget_row"], B=B, T=T)
    tkf = min(256, T)
    f_rows = f_tok[:, FORGET_LANE0:FORGET_LANE0 + FOX_HEADS].reshape(B, T, FOX_HEADS)
    f_rows = f_rows.transpose(0, 2, 1).reshape(B * FOX_HEADS, T // tkf, tkf)
    o_b = fox_attention(proj_fox, f_tok, f_rows, B=B, T=T, tq=min(128, T), tk=tkf)

    outs, lses = [], []
    for g, (window, dil) in enumerate(DIL_PAIRS):
        Ts = T // dil
        blk = [t * DIL_GROUPS + g for t in range(3)]
        if dil == 1:
            views = [proj_dil] * 3
            cols = [lambda h, r, c=c: c for c in blk]
        else:
            views = [proj_dil[:, c * OUT_C:(c + 1) * OUT_C].reshape(B * Ts, dil * OUT_C) for c in blk]
            cols = [lambda h, r: r] * 3
        o, lse = alibi_attention(
            *views, dil_slopes[g] * np.float32(dil), B=B, Ts=Ts, n_sub=dil, n_head_blocks=1,
            shared_kv=False, max_dist=window // dil, tq=128, tk=128, with_lse=True,
            q_col=cols[0], k_col=cols[1], v_col=cols[2])
        outs.append(o.reshape(N, OUT_C))
        lses.append(lse.reshape(N, OUT_C))
    o_c = dil_merge(outs, lses)

    z = gated_branches(u, p["w_gate"], p["b_gate"], lyr, o_a, o_b, o_c, p["wb_a"], p["wb_b"], p["wb_c"])
    return matmul(z, p["w_out"], F32, layer=lyr)


def _peer_ffn(u, ut, p):
    cpt = 4
    q = matmul(u, p["peer_wq"], BF16, layer=p["layer"])
    e1, l1, e2, r2 = peer_select(q, p["peer_subkeys"], cpt=cpt)
    return peer_dense(ut, p["peer_u"], p["peer_v"], p["layer"], e1, l1, e2, r2, cpt=cpt)


def _layer_params(l, shared, b_forget, cmp_pe, cmp_w1, cmp_w2, w_branch, peer_subkeys):
    b_row = jnp.zeros((1, LANES), F32).at[0, FORGET_LANE0:FORGET_LANE0 + FOX_HEADS].set(b_forget[l])
    half = CMP_LEN // 2
    return dict(
        shared,
        layer=l,
        b_forget_row=b_row,
        cmp_pe=cmp_pe[l].reshape(2, 2, 1, half * HEAD_DIM),
        cmp_w1=cmp_w1[l].astype(BF16).reshape(2, 2, half * HEAD_DIM, HEAD_DIM),
        cmp_w2=cmp_w2[l].astype(BF16),
        wb_a=w_branch[l, :OUT_A].astype(BF16),
        wb_b=w_branch[l, OUT_A:OUT_A + OUT_B].astype(BF16),
        wb_c=w_branch[l, OUT_A + OUT_B:].astype(BF16),
        peer_subkeys=peer_subkeys[l].astype(BF16),
    )


def kernel(x, c, w_ada, b_ada, w_in, b_forget, cmp_pe, cmp_w1, cmp_w2, w_branch, w_gate, b_gate, w_out,
           ln1_g, ln1_b, peer_wq, peer_subkeys, peer_u, peer_v, ln2_g, ln2_b):
    B, T, D = x.shape
    L = w_ada.shape[0]
    N = B * T
    c_pad = jnp.zeros((8, D), BF16).at[:B].set(c.astype(BF16))
    mod = ada_project(c_pad, w_ada, b_ada.reshape(L, 1, 6 * D))
    mods = [mod[l].reshape(8, 1, 6 * D) for l in range(L)]
    xf = x.reshape(N, D)
    u = modulate(xf, mods[0], 1, 0, T)
    shared = dict(w_in_rows=column_rows_view(w_in), n_layers=L, w_out=w_out, peer_wq=peer_wq,
                  w_gate=w_gate.astype(BF16), b_gate=b_gate.reshape(L, 1, 3 * D),
                  peer_u=peer_u.astype(BF16), peer_v=peer_v.astype(BF16))
    for l in range(L):
        p = _layer_params(l, shared, b_forget, cmp_pe, cmp_w1, cmp_w2, w_branch, peer_subkeys)
        y = _hybrid_mixer(u, p, B=B, T=T)
        xf, u, ut = postnorm(xf, y, mods[l], 2, ln1_g[l].reshape(1, D), ln1_b[l].reshape(1, D),
                             mods[l], 4, 3, T, transposed=True)
        y = _peer_ffn(u, ut, p)
        nxt = mods[min(l + 1, L - 1)]
        xf, u = postnorm(xf, y, mods[l], 5, ln2_g[l].reshape(1, D), ln2_b[l].reshape(1, D),
                         nxt, 1, 0, T)
    return xf.reshape(B, T, D)
```

```python
import functools

import numpy as np
import jax
import jax.numpy as jnp
from jax import lax
from jax.experimental import pallas as pl
from jax.experimental.pallas import tpu as pltpu

F32 = jnp.float32
BF16 = jnp.bfloat16

D_MODEL = 4096
HEAD_DIM = 128
NSA_HEADS = 12
NSA_KV_HEADS = 3
NSA_GROUP = NSA_HEADS // NSA_KV_HEADS
CMP_LEN = 32
CMP_STRIDE = 16
SLC_BLK = 64
N_SEL = 16
WIN = 512
FOX_HEADS = 8
DIL_PAIRS = ((128, 1), (512, 4), (2048, 16))
DIL_GROUPS = 3
DIL_HPG = 4
DIL_HEADS = DIL_GROUPS * DIL_HPG
ALIBI_HEADS = NSA_HEADS + DIL_HEADS
NSA_Q_W = NSA_HEADS * HEAD_DIM
NSA_KV_W = NSA_KV_HEADS * HEAD_DIM
FOX_W = FOX_HEADS * HEAD_DIM
DIL_W = DIL_HEADS * HEAD_DIM
OFF_NSA_KV = NSA_Q_W
OFF_NSA_GATE = OFF_NSA_KV + 6 * NSA_KV_W
OFF_FOX = OFF_NSA_GATE + 3 * NSA_HEADS
OFF_FOX_F = OFF_FOX + 3 * FOX_W
OFF_DIL = OFF_FOX_F + FOX_HEADS
IN_COLS = OFF_DIL + 3 * DIL_W
NSA_A_W = NSA_Q_W + 6 * NSA_KV_W
OUT_A = NSA_Q_W
OUT_B = FOX_W
OUT_C = DIL_HPG * HEAD_DIM
PEER_HEADS = 8
PEER_NKEYS = 128
PEER_EXPERTS = PEER_NKEYS ** 2
PEER_TOPK = 16
PEER_QDIM = 256
DEPTH = 2
ALPHA = (2.0 * DEPTH) ** 0.25
LN_EPS = 1e-5
NEG = -1e30
BIG = 1e30
SCALE = HEAD_DIM ** -0.5

LANES = 128
GATE_LANE0 = 0
FORGET_LANE0 = 3 * NSA_HEADS
VMEM_LIMIT = 56 * 1024 * 1024


def _cp(*sem):
    return pltpu.CompilerParams(dimension_semantics=sem, vmem_limit_bytes=VMEM_LIMIT)


def _alibi_slopes(n):
    return (np.float32(2.0) ** (-8.0 * np.arange(1, n + 1, dtype=np.float32) / n)).astype(np.float32)


def _gelu(x):
    return 0.5 * x * (1.0 + jnp.tanh(np.float32(np.sqrt(2.0 / np.pi)) * (x + np.float32(0.044715) * (x * x * x))))


def _split3(x):
    hi = x.astype(BF16)
    r1 = x - hi.astype(F32)
    mid = r1.astype(BF16)
    lo = (r1 - mid.astype(F32)).astype(BF16)
    return hi, mid, lo


def _dot3(x, m):
    hi, mid, lo = _split3(x)
    d = functools.partial(jnp.dot, preferred_element_type=F32)
    return (d(lo, m) + d(mid, m)) + d(hi, m)


def _dot_nt(a, b):
    return lax.dot_general(a, b, (((1,), (1,)), ((), ())), preferred_element_type=F32)


def _ada_kernel(c_ref, w_ref, b_ref, o_ref):
    o_ref[...] = jnp.dot(c_ref[...], w_ref[...].astype(BF16), preferred_element_type=F32) + b_ref[...]


def ada_project(c_pad, w_ada, b_ada, tn=512):
    L, D, N6 = w_ada.shape
    return pl.pallas_call(
        _ada_kernel,
        grid=(L, N6 // tn),
        in_specs=[pl.BlockSpec((8, D), lambda l, j: (0, 0)),
                  pl.BlockSpec((None, D, tn), lambda l, j: (l, 0, j)),
                  pl.BlockSpec((None, 1, tn), lambda l, j: (l, 0, j))],
        out_specs=pl.BlockSpec((None, 8, tn), lambda l, j: (l, 0, j)),
        out_shape=jax.ShapeDtypeStruct((L, 8, N6), F32),
        compiler_params=_cp("parallel", "parallel"),
        name="ada_project",
    )(c_pad, w_ada, b_ada)


def _cast_kernel(x_ref, o_ref):
    o_ref[...] = x_ref[...].astype(o_ref.dtype)


def cast_bf16(x, tr=512, tc=4096):
    L, R, C = x.shape
    spec = pl.BlockSpec((None, tr, tc), lambda l, i, j: (l, i, j))
    return pl.pallas_call(
        _cast_kernel,
        grid=(L, R // tr, C // tc),
        in_specs=[spec],
        out_specs=spec,
        out_shape=jax.ShapeDtypeStruct(x.shape, BF16),
        compiler_params=_cp("parallel", "parallel", "parallel"),
        name="cast_bf16",
    )(x)


def _modulate_kernel(x_ref, sc_ref, sh_ref, u_ref):
    u_ref[...] = (x_ref[...] * (1.0 + sc_ref[...]) + sh_ref[...]).astype(u_ref.dtype)


def modulate(x, mod3, sc_blk, sh_blk, T, tr=256):
    N, D = x.shape
    return pl.pallas_call(
        _modulate_kernel,
        grid=(N // tr,),
        in_specs=[pl.BlockSpec((tr, D), lambda i: (i, 0)),
                  pl.BlockSpec((None, 1, D), lambda i: (i * tr // T, 0, sc_blk)),
                  pl.BlockSpec((None, 1, D), lambda i: (i * tr // T, 0, sh_blk))],
        out_specs=pl.BlockSpec((tr, D), lambda i: (i, 0)),
        out_shape=jax.ShapeDtypeStruct((N, D), BF16),
        compiler_params=_cp("parallel"),
        name="modulate",
    )(x, mod3, mod3)


def _mm_kernel(a_ref, w_ref, o_ref, wb_ref):
    @pl.when(pl.program_id(1) == 0)
    def _():
        wb_ref[...] = w_ref[...].astype(BF16)

    o_ref[...] = jnp.dot(a_ref[...], wb_ref[...], preferred_element_type=F32).astype(o_ref.dtype)


def matmul(a, w, out_dtype, layer, tm=1024, tn=512):
    M, K = a.shape
    N = w.shape[2]
    assert M % tm == 0 and N % tn == 0
    return pl.pallas_call(
        _mm_kernel,
        grid=(N // tn, M // tm),
        in_specs=[pl.BlockSpec((tm, K), lambda j, i: (i, 0)),
                  pl.BlockSpec((None, K, tn), lambda j, i: (layer, 0, j))],
        out_specs=pl.BlockSpec((tm, tn), lambda j, i: (i, j)),
        out_shape=jax.ShapeDtypeStruct((M, N), out_dtype),
        scratch_shapes=[pltpu.VMEM((K, tn), BF16)],
        compiler_params=_cp("parallel", "arbitrary"),
        name="matmul",
    )(a, w)


def _gather_layer_rows(w_ref, wb_ref, layer, n_layers):
    tn, K = wb_ref.shape
    kt_n = K // LANES
    stride = kt_n * n_layers
    sub = 16
    for g in range(tn // sub):
        for kt in range(kt_n):
            first = (g * sub * kt_n + kt) * n_layers + layer
            wb_ref[g * sub:(g + 1) * sub, kt * LANES:(kt + 1) * LANES] = (
                w_ref[pl.ds(first, sub, stride=stride), :].astype(BF16))


def _mm_wt_kernel(a_ref, w_ref, o_ref, wb_ref, *, layer, n_layers):
    @pl.when(pl.program_id(1) == 0)
    def _():
        _gather_layer_rows(w_ref, wb_ref, layer, n_layers)

    o_ref[...] = _dot_nt(a_ref[...], wb_ref[...]).astype(o_ref.dtype)


def column_rows_view(w):
    L, K, N = w.shape
    return w.transpose(2, 0, 1).reshape(N, L, K // LANES, LANES).transpose(0, 2, 1, 3).reshape(
        N * (K // LANES) * L, LANES)


def _column_window(n_layers, K, tn, start_col):
    rows_per_col = (K // LANES) * n_layers
    return pl.BlockSpec((pl.Element(tn * rows_per_col), pl.Element(LANES)),
                        lambda *ids: (start_col(*ids) * rows_per_col, 0), pipeline_mode=pl.Buffered(1))


def matmul_wt(a, w_rows, out_dtype, n_layers, layer, col0, n_cols, tm=512, tn=768):
    M, K = a.shape
    assert M % tm == 0 and n_cols % tn == 0
    return pl.pallas_call(
        functools.partial(_mm_wt_kernel, layer=layer, n_layers=n_layers),
        grid=(n_cols // tn, M // tm),
        in_specs=[pl.BlockSpec((tm, K), lambda j, i: (i, 0)),
                  _column_window(n_layers, K, tn, lambda j, i: col0 + j * tn)],
        out_specs=pl.BlockSpec((tm, tn), lambda j, i: (i, j)),
        out_shape=jax.ShapeDtypeStruct((M, n_cols), out_dtype),
        scratch_shapes=[pltpu.VMEM((tn, K), BF16)],
        compiler_params=_cp("parallel", "arbitrary"),
        name="matmul_wt",
    )(a, w_rows)


def _small_proj_kernel(a_ref, wg_ref, wf_ref, o_ref, wbg_ref, wbf_ref, *, layer, n_layers):
    @pl.when(pl.program_id(0) == 0)
    def _():
        _gather_layer_rows(wg_ref, wbg_ref, layer, n_layers)
        _gather_layer_rows(wf_ref, wbf_ref, layer, n_layers)

    a = a_ref[...]
    lane = lax.broadcasted_iota(jnp.int32, o_ref.shape, 1)
    o_ref[...] = jnp.where(lane < FORGET_LANE0, _dot_nt(a, wbg_ref[...]), _dot_nt(a, wbf_ref[...]))


def small_projection(a, w_rows, n_layers, layer, tm=1024):
    M, K = a.shape
    return pl.pallas_call(
        functools.partial(_small_proj_kernel, layer=layer, n_layers=n_layers),
        grid=(M // tm,),
        in_specs=[pl.BlockSpec((tm, K), lambda i: (i, 0)),
                  _column_window(n_layers, K, LANES, lambda i: OFF_NSA_GATE - GATE_LANE0),
                  _column_window(n_layers, K, LANES, lambda i: OFF_FOX_F - FORGET_LANE0)],
        out_specs=pl.BlockSpec((tm, LANES), lambda i: (i, 0)),
        out_shape=jax.ShapeDtypeStruct((M, LANES), F32),
        scratch_shapes=[pltpu.VMEM((LANES, K), BF16), pltpu.VMEM((LANES, K), BF16)],
        compiler_params=_cp("arbitrary"),
        name="small_projection",
    )(a, w_rows, w_rows)


HEADS_PER_STEP = 4


def _head_cols(i):
    return slice(i * HEAD_DIM, (i + 1) * HEAD_DIM)


def _flash(qs, load_kv, lo, hi, chunk_fn, score_fns):
    tq = qs[0].shape[0]

    def body(c, carry):
        shared = chunk_fn(c)
        kvs = [load_kv(c, i) for i in range(len(qs))]
        raws = [_dot_nt(q, kvs[i][0]) for i, q in enumerate(qs)]
        mids = []
        for i, raw in enumerate(raws):
            m, l, _ = carry[i]
            s = score_fns[i](raw, c, shared)
            m_new = jnp.maximum(m, jnp.max(s, axis=1, keepdims=True))
            alpha = jnp.exp(m - m_new)
            p = jnp.exp(s - m_new)
            mids.append((m_new, alpha, alpha * l + jnp.sum(p, axis=1, keepdims=True), p))
        out = []
        for i, (m_new, alpha, l, p) in enumerate(mids):
            v = kvs[i][1]
            acc = alpha * carry[i][2] + jnp.dot(p.astype(v.dtype), v, preferred_element_type=F32)
            out.append((m_new, l, acc))
        return tuple(out)

    one = (jnp.full((tq, 1), NEG, F32), jnp.zeros((tq, 1), F32), jnp.zeros((tq, HEAD_DIM), F32))
    return lax.fori_loop(lo, hi, body, tuple(one for _ in qs))


def _flash_t(qs, load_kv, lo, hi, chunk_fn, score_fns):
    tq = qs[0].shape[0]

    def body(c, stats):
        shared = chunk_fn(c)
        raws = [_dot_nt(load_kv(c, i)[0], q) for i, q in enumerate(qs)]
        mids = []
        for i, raw in enumerate(raws):
            m, l, _ = stats[i]
            s = score_fns[i](raw, c, shared)
            m_new = jnp.maximum(m, jnp.max(s, axis=0, keepdims=True))
            alpha = jnp.exp(m - m_new)
            p = jnp.exp(s - m_new)
            mids.append((m_new, alpha, alpha * l + jnp.sum(p, axis=0, keepdims=True), p))
        out = []
        for i, (m_new, alpha, l, p) in enumerate(mids):
            v = load_kv(c, i)[1]
            pv = lax.dot_general(v, p.astype(v.dtype), (((0,), (0,)), ((), ())), preferred_element_type=F32)
            out.append((m_new, l, alpha * stats[i][2] + pv))
        return tuple(out)

    one = (jnp.full((1, tq), NEG, F32), jnp.zeros((1, tq), F32), jnp.zeros((HEAD_DIM, tq), F32))
    return lax.fori_loop(lo, hi, body, tuple(one for _ in qs))


def _alibi_kernel(slope_ref, q_ref, k_ref, v_ref, *rest, tq, tk, max_dist, shared_kv, with_sel, with_lse):
    rest = list(rest)
    sel_ref = rest.pop(0) if with_sel else None
    o_ref = rest.pop(0)
    lse_ref = rest.pop(0) if with_lse else None
    hb = pl.program_id(1)
    q0 = pl.program_id(3) * tq
    nh = HEADS_PER_STEP
    rc = lax.broadcasted_iota(jnp.int32, (tk, tq), 1) - lax.broadcasted_iota(jnp.int32, (tk, tq), 0)
    if with_sel:
        sel_t = sel_ref[...].astype(F32).T.astype(BF16)
        lane_blk = lax.broadcasted_iota(jnp.int32, (tk, LANES), 1)
        key_blk = lax.broadcasted_iota(jnp.int32, (tk, LANES), 0) // SLC_BLK

    def valid_fn(c):
        dist = rc + (q0 - c * tk)
        valid = dist >= 0
        if max_dist is not None:
            valid = valid & (dist <= max_dist)
        if with_sel:
            expand = (lane_blk == key_blk + (hb * (LANES // 4) + c * (tk // SLC_BLK))).astype(BF16)
            valid = valid & (jnp.dot(expand, sel_t, preferred_element_type=F32) > 0.5)
        return dist.astype(F32), valid

    def make_score(i):
        slope = slope_ref[hb * nh + i]

        def score_fn(s, c, shared):
            dist_f, valid = shared
            return jnp.where(valid, s * SCALE - slope * dist_f, NEG)
        return score_fn

    def load_kv(c, i):
        off = pl.multiple_of(c * tk, tk)
        cols = _head_cols(0 if shared_kv else i)
        return k_ref[pl.ds(off, tk), cols], v_ref[pl.ds(off, tk), cols]

    hi = (q0 + tq - 1) // tk + 1
    lo = 0 if max_dist is None else jnp.maximum(q0 - max_dist, 0) // tk
    res = _flash_t([q_ref[:, _head_cols(i)] for i in range(nh)], load_kv, lo, hi, valid_fn,
                   [make_score(i) for i in range(nh)])
    for i, (m, l, acc_t) in enumerate(res):
        o_ref[:, _head_cols(i)] = (acc_t / l).T.astype(o_ref.dtype)
        if with_lse:
            lse_ref[:, _head_cols(i)] = jnp.broadcast_to(m + jnp.log(l), (HEAD_DIM, tq)).T


def alibi_attention(qa, ka, va, slopes, *, B, Ts, n_sub, n_head_blocks, shared_kv, q_col, k_col, v_col,
                    max_dist, tq, tk, sel=None, with_lse=False):
    tq = min(tq, Ts)
    tk = min(tk, Ts)
    nq = Ts // tq
    rows = B * Ts
    wq = HEADS_PER_STEP * HEAD_DIM
    wkv = HEAD_DIM if shared_kv else wq
    in_specs = [pl.BlockSpec((tq, wq), lambda b, h, r, i, s: (b * nq + i, q_col(h, r))),
                pl.BlockSpec((Ts, wkv), lambda b, h, r, i, s: (b, k_col(h, r))),
                pl.BlockSpec((Ts, wkv), lambda b, h, r, i, s: (b, v_col(h, r)))]
    args = [qa, ka, va]
    if sel is not None:
        in_specs.append(pl.BlockSpec((tq, LANES), lambda b, h, r, i, s: (b * nq + i, 0)))
        args.append(sel)
    o_spec = pl.BlockSpec((tq, wq), lambda b, h, r, i, s: (b * nq + i, r * n_head_blocks + h))
    o_shape = jax.ShapeDtypeStruct((rows, n_sub * n_head_blocks * wq), F32)
    kern = functools.partial(_alibi_kernel, tq=tq, tk=tk, max_dist=max_dist, shared_kv=shared_kv,
                             with_sel=sel is not None, with_lse=with_lse)
    return pl.pallas_call(
        kern,
        grid_spec=pltpu.PrefetchScalarGridSpec(
            num_scalar_prefetch=1, grid=(B, n_head_blocks, n_sub, nq),
            in_specs=in_specs,
            out_specs=[o_spec, o_spec] if with_lse else o_spec),
        out_shape=[o_shape, o_shape] if with_lse else o_shape,
        compiler_params=_cp("parallel", "parallel", "parallel", "parallel"),
        name="alibi_attention",
    )(jnp.asarray(slopes, F32), *args)


def _fox_kernel(q_ref, k_ref, v_ref, fq_ref, fk_ref, o_ref, *, tq, tk):
    hb = pl.program_id(1)
    q0 = pl.program_id(2) * tq
    nh = HEADS_PER_STEP
    lane = lax.broadcasted_iota(jnp.int32, (tq, LANES), 1)
    f_tok = fq_ref[...]
    rc = lax.broadcasted_iota(jnp.int32, (tq, tk), 0) - lax.broadcasted_iota(jnp.int32, (tq, tk), 1)

    def make_score(i):
        fq = jnp.sum(jnp.where(lane == FORGET_LANE0 + hb * nh + i, f_tok, 0.0), axis=1, keepdims=True)

        def score_fn(s, c, causal):
            return jnp.where(causal, (s * SCALE + fq) - fk_ref[i, pl.ds(c, 1), :], NEG)
        return score_fn

    def load_kv(c, i):
        off = pl.multiple_of(c * tk, tk)
        return k_ref[pl.ds(off, tk), _head_cols(i)], v_ref[pl.ds(off, tk), _head_cols(i)]

    hi = (q0 + tq - 1) // tk + 1
    res = _flash([q_ref[:, _head_cols(i)] for i in range(nh)], load_kv, 0, hi,
                 lambda c: rc + (q0 - c * tk) >= 0, [make_score(i) for i in range(nh)])
    for i, (_, l, acc) in enumerate(res):
        o_ref[:, _head_cols(i)] = (acc / l).astype(o_ref.dtype)


def fox_attention(proj_fox, f_tok, f_rows, *, B, T, tq=256, tk=256):
    nq = T // tq
    nh = HEADS_PER_STEP
    nhb = FOX_HEADS // nh
    wq = nh * HEAD_DIM
    return pl.pallas_call(
        functools.partial(_fox_kernel, tq=tq, tk=tk),
        grid=(B, nhb, nq),
        in_specs=[pl.BlockSpec((tq, wq), lambda b, h, i: (b * nq + i, h)),
                  pl.BlockSpec((T, wq), lambda b, h, i: (b, nhb + h)),
                  pl.BlockSpec((T, wq), lambda b, h, i: (b, 2 * nhb + h)),
                  pl.BlockSpec((tq, LANES), lambda b, h, i: (b * nq + i, 0)),
                  pl.BlockSpec((nh, T // tk, tk), lambda b, h, i: (b * nhb + h, 0, 0))],
        out_specs=pl.BlockSpec((tq, wq), lambda b, h, i: (b * nq + i, h)),
        out_shape=jax.ShapeDtypeStruct((B * T, FOX_W), BF16),
        compiler_params=_cp("parallel", "parallel", "parallel"),
        name="fox_attention",
    )(proj_fox, proj_fox, proj_fox, f_tok, f_rows)


def _cumsum_kernel(x_ref, b_ref, o_ref, *, T, tr):
    i = pl.program_id(1)
    z = x_ref[...] + b_ref[...]
    logf = jnp.minimum(z, 0.0) - jnp.log1p(jnp.exp(-jnp.abs(z)))
    row = lax.broadcasted_iota(jnp.int32, (tr, T), 0) + i * tr
    col = lax.broadcasted_iota(jnp.int32, (tr, T), 1)
    o_ref[...] = _dot3_lhs((row >= col).astype(BF16), logf)


def _dot3_lhs(m, x):
    hi, mid, lo = _split3(x)
    d = functools.partial(jnp.dot, preferred_element_type=F32)
    return (d(m, lo) + d(m, mid)) + d(m, hi)


def forget_cumsum(small, bias_row, *, B, T, tr=256):
    return pl.pallas_call(
        functools.partial(_cumsum_kernel, T=T, tr=tr),
        grid=(B, T // tr),
        in_specs=[pl.BlockSpec((T, LANES), lambda b, i: (b, 0)),
                  pl.BlockSpec((1, LANES), lambda b, i: (0, 0))],
        out_specs=pl.BlockSpec((tr, LANES), lambda b, i: (b * (T // tr) + i, 0)),
        out_shape=jax.ShapeDtypeStruct((B * T, LANES), F32),
        compiler_params=_cp("parallel", "parallel"),
        name="forget_cumsum",
    )(small, bias_row)


def _compress_kernel(x_ref, pe_ref, w1_ref, w2_ref, o_ref):
    x = x_ref[...].astype(F32)
    y0 = jnp.dot((x + pe_ref[0]).astype(BF16), w1_ref[0], preferred_element_type=F32)
    y1 = jnp.dot((x + pe_ref[1]).astype(BF16), w1_ref[1], preferred_element_type=F32)
    n = y1.shape[0]
    hmid = _gelu(y0 + pltpu.roll(y1, n - 1, 0))
    o_ref[...] = jnp.dot(hmid.astype(BF16), w2_ref[...], preferred_element_type=F32).astype(o_ref.dtype)


def compress_blocks(chunks, pe, w1, w2):
    _, G, nch, cw = chunks.shape
    return pl.pallas_call(
        _compress_kernel,
        grid=(2, G),
        in_specs=[pl.BlockSpec((None, None, nch, cw), lambda a, g: (a, g, 0, 0)),
                  pl.BlockSpec((None, 2, 1, cw), lambda a, g: (a, 0, 0, 0)),
                  pl.BlockSpec((None, 2, cw, HEAD_DIM), lambda a, g: (a, 0, 0, 0)),
                  pl.BlockSpec((None, HEAD_DIM, HEAD_DIM), lambda a, g: (a, 0, 0))],
        out_specs=pl.BlockSpec((None, None, nch, HEAD_DIM), lambda a, g: (a, g, 0, 0)),
        out_shape=jax.ShapeDtypeStruct((2, G, nch, HEAD_DIM), BF16),
        compiler_params=_cp("parallel", "parallel"),
        name="compress_blocks",
    )(chunks, pe, w1, w2)


def _cmp_kernel(slope_ref, q_ref, kc_ref, vc_ref, smat_ref, o_ref, sel_ref, *, tq, n_cmp, n_slc):
    q0 = pl.program_id(1) * tq
    nk = kc_ref.shape[1]
    pos = lax.broadcasted_iota(jnp.int32, (tq, nk), 0) + q0
    n_idx = lax.broadcasted_iota(jnp.int32, (tq, nk), 1)
    dist_c = pos - (n_idx * CMP_STRIDE + (CMP_LEN - 1))
    valid_c = (dist_c >= 0) & (n_idx < n_cmp)
    dist_f = dist_c.astype(F32)
    p_slc = jnp.zeros((tq, LANES), F32)
    for kv in range(NSA_KV_HEADS):
        kc = kc_ref[kv]
        vc = vc_ref[kv]
        imp = jnp.zeros((tq, nk), F32)
        for g in range(NSA_GROUP):
            hh = kv * NSA_GROUP + g
            q = q_ref[:, hh * HEAD_DIM:(hh + 1) * HEAD_DIM]
            s = _dot_nt(q, kc) * SCALE - slope_ref[hh] * dist_f
            s = jnp.where(valid_c, s, NEG)
            e = jnp.where(valid_c, jnp.exp(s - jnp.max(s, axis=1, keepdims=True)), 0.0)
            p = e / jnp.maximum(jnp.sum(e, axis=1, keepdims=True), 1e-30)
            o_ref[:, hh * HEAD_DIM:(hh + 1) * HEAD_DIM] = jnp.dot(
                p.astype(BF16), vc, preferred_element_type=F32)
            imp = imp + p
        p_slc = p_slc + _dot3(imp, smat_ref[kv])
    grp = LANES // 4
    p_t = p_slc.T
    j = lax.broadcasted_iota(jnp.int32, (grp, tq), 0)
    qblk = (lax.broadcasted_iota(jnp.int32, (grp, tq), 1) + q0) // SLC_BLK
    forced = (j == 0) | (j == qblk) | (j == qblk - 1)
    causal = j <= qblk
    chosen = []
    for kv in range(NSA_KV_HEADS):
        score = jnp.where(forced, BIG, jnp.where(causal, p_t[kv * grp:(kv + 1) * grp], NEG))
        rank = jnp.zeros((grp, tq), jnp.int32)
        for k in range(n_slc):
            row = score[k:k + 1]
            ahead = (row > score) | ((row == score) & (k < j))
            rank = rank + ahead.astype(jnp.int32)
        chosen.append(jnp.where(causal & (rank < min(N_SEL, n_slc)), 1.0, 0.0))
    chosen.append(jnp.zeros((LANES - NSA_KV_HEADS * grp, tq), F32))
    sel_ref[...] = jnp.concatenate(chosen, axis=0).T.astype(sel_ref.dtype)


def cmp_attention(proj_a, kvc, smat, slopes, *, B, T, tq=256):
    nq = T // tq
    nch = kvc.shape[2]
    n_cmp = (T - CMP_LEN) // CMP_STRIDE + 1
    n_slc = T // SLC_BLK
    kvc4 = kvc.reshape(2, B, NSA_KV_HEADS, nch, HEAD_DIM)
    return pl.pallas_call(
        functools.partial(_cmp_kernel, tq=tq, n_cmp=n_cmp, n_slc=n_slc),
        grid_spec=pltpu.PrefetchScalarGridSpec(
            num_scalar_prefetch=1, grid=(B, nq),
            in_specs=[pl.BlockSpec((tq, NSA_Q_W), lambda b, i, s: (b * nq + i, 0)),
                      pl.BlockSpec((None, None, NSA_KV_HEADS, nch, HEAD_DIM), lambda b, i, s: (0, b, 0, 0, 0)),
                      pl.BlockSpec((None, None, NSA_KV_HEADS, nch, HEAD_DIM), lambda b, i, s: (1, b, 0, 0, 0)),
                      pl.BlockSpec((NSA_KV_HEADS, nch, LANES), lambda b, i, s: (0, 0, 0))],
            out_specs=[pl.BlockSpec((tq, NSA_Q_W), lambda b, i, s: (b * nq + i, 0)),
                       pl.BlockSpec((tq, LANES), lambda b, i, s: (b * nq + i, 0))]),
        out_shape=[jax.ShapeDtypeStruct((B * T, NSA_Q_W), F32),
                   jax.ShapeDtypeStruct((B * T, LANES), BF16)],
        compiler_params=_cp("parallel", "parallel"),
        name="cmp_attention",
    )(jnp.asarray(slopes, F32), proj_a, kvc4, kvc4, smat)


def _pool_matrix(nch, n_slc):
    r = SLC_BLK // CMP_STRIDE
    c_over = CMP_LEN // CMP_STRIDE
    n_cmp_max = nch
    m = np.zeros((NSA_KV_HEADS, nch, LANES), np.float32)
    for kv in range(NSA_KV_HEADS):
        for j in range(n_slc):
            for a in range(r):
                for b in range(c_over):
                    i = r * j + a + b
                    if i < n_cmp_max:
                        m[kv, i, kv * (LANES // 4) + j] += 1.0
    return m


def _nsa_merge_kernel(oc_ref, os_ref, ow_ref, g_ref, e_ref, o_ref):
    sig = jax.nn.sigmoid(g_ref[...])
    out = _dot3(sig, e_ref[0]) * oc_ref[...]
    out = out + _dot3(sig, e_ref[1]) * os_ref[...]
    out = out + _dot3(sig, e_ref[2]) * ow_ref[...]
    o_ref[...] = out.astype(o_ref.dtype)


def nsa_merge(o_cmp, o_slc, o_win, small, tr=256):
    N = o_cmp.shape[0]
    e = np.zeros((3, LANES, NSA_Q_W), np.float32)
    for br in range(3):
        for h in range(NSA_HEADS):
            e[br, GATE_LANE0 + h * 3 + br, h * HEAD_DIM:(h + 1) * HEAD_DIM] = 1.0
    row = pl.BlockSpec((tr, NSA_Q_W), lambda i: (i, 0))
    return pl.pallas_call(
        _nsa_merge_kernel,
        grid=(N // tr,),
        in_specs=[row, row, row,
                  pl.BlockSpec((tr, LANES), lambda i: (i, 0)),
                  pl.BlockSpec((3, LANES, NSA_Q_W), lambda i: (0, 0, 0))],
        out_specs=row,
        out_shape=jax.ShapeDtypeStruct((N, NSA_Q_W), BF16),
        compiler_params=_cp("parallel"),
        name="nsa_merge",
    )(o_cmp, o_slc, o_win, small, jnp.asarray(e, BF16))


def _dil_merge_kernel(o0, o1, o2, l0, l1, l2, o_ref):
    a, b, c = l0[...], l1[...], l2[...]
    m = jnp.maximum(jnp.maximum(a, b), c)
    ea, eb, ec = jnp.exp(a - m), jnp.exp(b - m), jnp.exp(c - m)
    z = ea + eb + ec
    o_ref[...] = ((ea / z) * o0[...] + (eb / z) * o1[...] + (ec / z) * o2[...]).astype(o_ref.dtype)


def dil_merge(outs, lses, tr=512):
    N, W = outs[0].shape
    row = pl.BlockSpec((tr, W), lambda i: (i, 0))
    return pl.pallas_call(
        _dil_merge_kernel,
        grid=(N // tr,),
        in_specs=[row] * 6,
        out_specs=row,
        out_shape=jax.ShapeDtypeStruct((N, W), BF16),
        compiler_params=_cp("parallel"),
        name="dil_merge",
    )(*outs, *lses)


def _branch_kernel(u_ref, wga, wgb, wgc, bga, bgb, bgc, oa, ob, oc, wa, wb, wc, z_ref):
    d = functools.partial(jnp.dot, preferred_element_type=F32)
    u = u_ref[...]
    z = jax.nn.sigmoid(d(u, wga[...]) + bga[...]) * d(oa[...], wa[...])
    z = z + jax.nn.sigmoid(d(u, wgb[...]) + bgb[...]) * d(ob[...], wb[...])
    z = z + jax.nn.sigmoid(d(u, wgc[...]) + bgc[...]) * d(oc[...], wc[...])
    z_ref[...] = z.astype(z_ref.dtype)


def gated_branches(u, w_gate, b_gate, layer, o_a, o_b, o_c, wb_a, wb_b, wb_c, tm=512, tn=512):
    N, D = u.shape
    nj = D // tn

    def wg(k):
        return pl.BlockSpec((None, D, tn), lambda j, i: (layer, 0, k * nj + j))

    def bg(k):
        return pl.BlockSpec((None, 1, tn), lambda j, i: (layer, 0, k * nj + j))

    def act(w):
        return pl.BlockSpec((tm, w), lambda j, i: (i, 0))

    def wup(w):
        return pl.BlockSpec((w, tn), lambda j, i: (0, j))

    return pl.pallas_call(
        _branch_kernel,
        grid=(nj, N // tm),
        in_specs=[act(D), wg(0), wg(1), wg(2), bg(0), bg(1), bg(2),
                  act(OUT_A), act(OUT_B), act(OUT_C), wup(OUT_A), wup(OUT_B), wup(OUT_C)],
        out_specs=pl.BlockSpec((tm, tn), lambda j, i: (i, j)),
        out_shape=jax.ShapeDtypeStruct((N, D), BF16),
        compiler_params=_cp("parallel", "parallel"),
        name="gated_branches",
    )(u, w_gate, w_gate, w_gate, b_gate, b_gate, b_gate, o_a, o_b, o_c, wb_a, wb_b, wb_c)


def _postnorm_kernel(x_ref, y_ref, g_ref, gam_ref, bet_ref, sc_ref, sh_ref, xo_ref, u_ref, *ut_ref):
    h = ALPHA * x_ref[...] + (1.0 + g_ref[...]) * y_ref[...]
    mu = jnp.mean(h, axis=-1, keepdims=True)
    hc = h - mu
    var = jnp.mean(hc * hc, axis=-1, keepdims=True)
    xn = hc * lax.rsqrt(var + LN_EPS) * gam_ref[...] + bet_ref[...]
    xo_ref[...] = xn
    u = xn * (1.0 + sc_ref[...]) + sh_ref[...]
    u_ref[...] = u.astype(u_ref.dtype)
    if ut_ref:
        ut_ref[0][...] = u.T.astype(u_ref.dtype)


def postnorm(x, y, mod_g, g_blk, gamma, beta, mod_n, sc_blk, sh_blk, T, tr=256, transposed=False):
    N, D = x.shape
    row = pl.BlockSpec((tr, D), lambda i: (i, 0))
    vec = pl.BlockSpec((1, D), lambda i: (0, 0))

    def mod(blk):
        return pl.BlockSpec((None, 1, D), lambda i: (i * tr // T, 0, blk))

    out_specs = [row, row]
    out_shape = [jax.ShapeDtypeStruct((N, D), F32), jax.ShapeDtypeStruct((N, D), BF16)]
    if transposed:
        out_specs.append(pl.BlockSpec((D, tr), lambda i: (0, i)))
        out_shape.append(jax.ShapeDtypeStruct((D, N), BF16))
    return pl.pallas_call(
        _postnorm_kernel,
        grid=(N // tr,),
        in_specs=[row, row, mod(g_blk), vec, vec, mod(sc_blk), mod(sh_blk)],
        out_specs=out_specs,
        out_shape=out_shape,
        compiler_params=_cp("parallel"),
        name="postnorm",
    )(x, y, mod_g, gamma, beta, mod_n, mod_n)


def _topk_rows(s, k, break_ties):
    rows, tn = s.shape
    iota = lax.broadcasted_iota(jnp.int32, (rows, tn), 0)
    rank = jnp.full((rows, tn), k, jnp.int32)
    cur = s
    tops = []
    for r in range(k):
        m = jnp.max(cur, axis=0, keepdims=True)
        hit = cur == m
        if break_ties:
            hit = iota == jnp.min(jnp.where(hit, iota, rows), axis=0, keepdims=True)
        rank = jnp.where(hit, r, rank)
        cur = jnp.where(hit, -jnp.inf, cur)
        tops.append(m)
    return rank, tops


def _merge_topk(top1, top2, k):
    t1 = _stack_rows(top1)
    tn = t1.shape[1]
    iota = lax.broadcasted_iota(jnp.int32, (k, tn), 0)
    taken = jnp.zeros((k, tn), jnp.int32)
    front = t1 + top2[0]
    m0 = top1[0] + top2[0]
    z = jnp.zeros((1, tn), F32)
    for _ in range(k):
        m = jnp.max(front, axis=0, keepdims=True)
        hit = iota == jnp.min(jnp.where(front == m, iota, k), axis=0, keepdims=True)
        z = z + jnp.exp(m - m0)
        nxt = jnp.sum(jnp.where(hit, taken, 0), axis=0, keepdims=True) + 1
        follower = jnp.full((1, tn), -jnp.inf, F32)
        for rb in range(1, k):
            follower = jnp.where(nxt == rb, top2[rb], follower)
        t1_hit = jnp.sum(jnp.where(hit, t1, 0.0), axis=0, keepdims=True)
        taken = taken + hit.astype(jnp.int32)
        front = jnp.where(hit, t1_hit + follower, front)
    return taken.astype(F32), z


def _stack_rows(rows_list):
    n = len(rows_list)
    tn = rows_list[0].shape[1]
    iota = lax.broadcasted_iota(jnp.int32, (n, tn), 0)
    out = jnp.zeros((n, tn), rows_list[0].dtype)
    for r, row in enumerate(rows_list):
        out = jnp.where(iota == r, row, out)
    return out


def _peer_select_kernel(q_ref, sk_ref, e1_ref, l1_ref, e2_ref, r2_ref, *, cpt):
    K = PEER_TOPK
    half = PEER_QDIM // 2
    s1 = _dot_nt(sk_ref[0], q_ref[:, :half])
    s2 = _dot_nt(sk_ref[1], q_ref[:, half:])

    def emit(break_ties):
        rank1, top1 = _topk_rows(s1, K, break_ties)
        rank2, top2 = _topk_rows(s2, K, break_ties)
        row_len, z = _merge_topk(top1, top2, K)
        l1 = jnp.zeros(s1.shape, F32)
        for ra in range(K):
            l1 = jnp.where(rank1 == ra, row_len[ra:ra + 1], l1)
        e1 = jnp.where(rank1 < K, jnp.exp(s1 - top1[0]) / z, 0.0)
        e2 = jnp.where(rank2 < K, jnp.exp(s2 - top2[0]), 0.0)
        for j in range(PEER_NKEYS // cpt):
            e1_ref[j] = e1[j * cpt:(j + 1) * cpt]
            l1_ref[j] = l1[j * cpt:(j + 1) * cpt]
        e2_ref[...] = e2.astype(BF16)
        r2_ref[...] = rank2.astype(BF16)
        members = (rank1 < K).astype(F32) + (rank2 < K).astype(F32)
        return jnp.sum(jnp.abs(jnp.sum(members, axis=0, keepdims=True) - 2.0 * K))

    tied = emit(False)

    @pl.when(tied != 0.0)
    def _():
        emit(True)


def peer_select(q, subkeys, *, cpt, tn=256):
    N = q.shape[0]
    H = PEER_HEADS
    nt = PEER_NKEYS // cpt
    s1 = pl.BlockSpec((nt, None, cpt, tn), lambda i, h: (0, h, 0, i))
    s2 = pl.BlockSpec((None, PEER_NKEYS, tn), lambda i, h: (h, 0, i))
    sh1 = jax.ShapeDtypeStruct((nt, H, cpt, N), F32)
    sh2 = jax.ShapeDtypeStruct((H, PEER_NKEYS, N), BF16)
    return pl.pallas_call(
        functools.partial(_peer_select_kernel, cpt=cpt),
        grid=(N // tn, H),
        in_specs=[pl.BlockSpec((tn, PEER_QDIM), lambda i, h: (i, h)),
                  pl.BlockSpec((None, 2, PEER_NKEYS, PEER_QDIM // 2), lambda i, h: (h, 0, 0, 0))],
        out_specs=[s1, s1, s2, s2],
        out_shape=[sh1, sh1, sh2, sh2],
        compiler_params=_cp("parallel", "parallel"),
        name="peer_select",
    )(q, subkeys)


def _peer_dense_kernel(ut_ref, tu_ref, tv_ref, e1_ref, l1_ref, e2_ref, r2_ref, o_ref, a_s, *, cpt, tnc):
    @pl.when(pl.program_id(1) == 0)
    def _():
        o_ref[...] = jnp.zeros_like(o_ref)

    nk = PEER_NKEYS
    cph = cpt // 2
    tm = ut_ref.shape[1]
    for k in range(2):
        rows = slice(k * cph * nk, (k + 1) * cph * nk)
        for hc in range(2):
            tc = slice(hc * (tm // 2), (hc + 1) * (tm // 2))
            ht = jnp.dot(tu_ref[rows, :], ut_ref[:, tc], preferred_element_type=F32)
            for c2 in range(cph):
                cc = k * cph + c2
                w = None
                for h in range(PEER_HEADS):
                    hit = r2_ref[h, :, tc] < l1_ref[h, cc:cc + 1, tc].astype(BF16)
                    wh = jnp.where(hit, e2_ref[h, :, tc], jnp.zeros((), BF16)) * e1_ref[h, cc:cc + 1, tc].astype(BF16)
                    w = wh if w is None else w + wh
                a_s[k, tc, c2 * nk:(c2 + 1) * nk] = (w.astype(F32) * _gelu(ht[c2 * nk:(c2 + 1) * nk])).T.astype(BF16)
    for k in range(2):
        rows = slice(k * cph * nk, (k + 1) * cph * nk)
        for nt in range(o_ref.shape[1] // tnc):
            cs = slice(nt * tnc, (nt + 1) * tnc)
            o_ref[:, cs] += jnp.dot(a_s[k], tv_ref[rows, cs], preferred_element_type=F32)


def peer_dense(ut, tu, tv, layer, e1, l1, e2, r2, *, cpt, tm=512):
    D, N = ut.shape
    E = tu.shape[1]
    te = cpt * PEER_NKEYS
    H = PEER_HEADS
    once = pl.Buffered(1)
    s1 = pl.BlockSpec((None, H, cpt, tm), lambda i, j: (j, 0, 0, i))
    s2 = pl.BlockSpec((H, PEER_NKEYS, tm), lambda i, j: (0, 0, i), pipeline_mode=once)
    tab = pl.BlockSpec((None, te, D), lambda i, j: (layer, j, 0))
    return pl.pallas_call(
        functools.partial(_peer_dense_kernel, cpt=cpt, tnc=512),
        grid=(N // tm, E // te),
        in_specs=[pl.BlockSpec((D, tm), lambda i, j: (0, i), pipeline_mode=once), tab, tab, s1, s1, s2, s2],
        out_specs=pl.BlockSpec((tm, D), lambda i, j: (i, 0)),
        out_shape=jax.ShapeDtypeStruct((N, D), F32),
        scratch_shapes=[pltpu.VMEM((2, tm, te // 2), BF16)],
        compiler_params=_cp("parallel", "arbitrary"),
        name="peer_dense",
    )(ut, tu, tv, e1, l1, e2, r2)


def _hybrid_mixer(u, p, *, B, T):
    N = B * T
    sl = _alibi_slopes(ALIBI_HEADS)
    nsa_slopes = sl[0::2]
    dil_slopes = sl[1::2].reshape(DIL_GROUPS, DIL_HPG)

    lyr = p["layer"]
    nl = p["n_layers"]
    proj_a = matmul_wt(u, p["w_in_rows"], BF16, nl, lyr, 0, NSA_A_W)
    small = small_projection(u, p["w_in_rows"], nl, lyr)
    proj_fox = matmul_wt(u, p["w_in_rows"], BF16, nl, lyr, OFF_FOX, 3 * FOX_W)
    proj_dil = matmul_wt(u, p["w_in_rows"], BF16, nl, lyr, OFF_DIL, 3 * DIL_W)

    nch = T // CMP_STRIDE
    kv_cmp = proj_a[:, OFF_NSA_KV:OFF_NSA_KV + 2 * NSA_KV_W].reshape(B, T, 2, NSA_KV_HEADS, HEAD_DIM)
    chunks = kv_cmp.transpose(2, 0, 3, 1, 4).reshape(2, B * NSA_KV_HEADS, nch, CMP_STRIDE * HEAD_DIM)
    kvc = compress_blocks(chunks, p["cmp_pe"], p["cmp_w1"], p["cmp_w2"])
    smat = jnp.asarray(_pool_matrix(nch, T // SLC_BLK), BF16)
    o_cmp, sel = cmp_attention(proj_a, kvc, smat, nsa_slopes, B=B, T=T)
    qh = NSA_HEADS
    kvh = NSA_KV_HEADS
    assert NSA_GROUP == HEADS_PER_STEP and DIL_HPG == HEADS_PER_STEP
    nsa_kw = dict(B=B, Ts=T, n_sub=1, n_head_blocks=kvh, shared_kv=True,
                  q_col=lambda h, r: h, tq=128, tk=256)
    o_slc = alibi_attention(proj_a, proj_a, proj_a, nsa_slopes, max_dist=None, sel=sel,
                            k_col=lambda h, r: qh + 2 * kvh + h,
                            v_col=lambda h, r: qh + 3 * kvh + h, **nsa_kw)
    o_win = alibi_attention(proj_a, proj_a, proj_a, nsa_slopes, max_dist=WIN - 1,
                            k_col=lambda h, r: qh + 4 * kvh + h,
                            v_col=lambda h, r: qh + 5 * kvh + h, **nsa_kw)
    o_a = nsa_merge(o_cmp, o_slc, o_win, small)

    f_tok = forget_cumsum(small, p["b_forget_row"], B=B, T=T)
    tkf = min(256, T)
    f_rows = f_tok[:, FORGET_LANE0:FORGET_LANE0 + FOX_HEADS].reshape(B, T, FOX_HEADS)
    f_rows = f_rows.transpose(0, 2, 1).reshape(B * FOX_HEADS, T // tkf, tkf)
    o_b = fox_attention(proj_fox, f_tok, f_rows, B=B, T=T, tq=min(128, T), tk=tkf)

    outs, lses = [], []
    for g, (window, dil) in enumerate(DIL_PAIRS):
        Ts = T // dil
        blk = [t * DIL_GROUPS + g for t in range(3)]
        if dil == 1:
            views = [proj_dil] * 3
            cols = [lambda h, r, c=c: c for c in blk]
        else:
            views = [proj_dil[:, c * OUT_C:(c + 1) * OUT_C].reshape(B * Ts, dil * OUT_C) for c in blk]
            cols = [lambda h, r: r] * 3
        o, lse = alibi_attention(
            *views, dil_slopes[g] * np.float32(dil), B=B, Ts=Ts, n_sub=dil, n_head_blocks=1,
            shared_kv=False, max_dist=window // dil, tq=128, tk=128, with_lse=True,
            q_col=cols[0], k_col=cols[1], v_col=cols[2])
        outs.append(o.reshape(N, OUT_C))
        lses.append(lse.reshape(N, OUT_C))
    o_c = dil_merge(outs, lses)

    z = gated_branches(u, p["w_gate"], p["b_gate"], lyr, o_a, o_b, o_c, p["wb_a"], p["wb_b"], p["wb_c"])
    return matmul(z, p["w_out"], F32, layer=lyr)


def _peer_ffn(u, ut, p):
    cpt = 4
    q = matmul(u, p["peer_wq"], BF16, layer=p["layer"])
    e1, l1, e2, r2 = peer_select(q, p["peer_subkeys"], cpt=cpt)
    return peer_dense(ut, p["peer_u"], p["peer_v"], p["layer"], e1, l1, e2, r2, cpt=cpt)


def _layer_params(l, shared, b_forget, cmp_pe, cmp_w1, cmp_w2, w_branch, peer_subkeys):
    b_row = jnp.zeros((1, LANES), F32).at[0, FORGET_LANE0:FORGET_LANE0 + FOX_HEADS].set(b_forget[l])
    half = CMP_LEN // 2
    return dict(
        shared,
        layer=l,
        b_forget_row=b_row,
        cmp_pe=cmp_pe[l].reshape(2, 2, 1, half * HEAD_DIM),
        cmp_w1=cmp_w1[l].astype(BF16).reshape(2, 2, half * HEAD_DIM, HEAD_DIM),
        cmp_w2=cmp_w2[l].astype(BF16),
        wb_a=w_branch[l, :OUT_A].astype(BF16),
        wb_b=w_branch[l, OUT_A:OUT_A + OUT_B].astype(BF16),
        wb_c=w_branch[l, OUT_A + OUT_B:].astype(BF16),
        peer_subkeys=peer_subkeys[l].astype(BF16),
    )


def kernel(x, c, w_ada, b_ada, w_in, b_forget, cmp_pe, cmp_w1, cmp_w2, w_branch, w_gate, b_gate, w_out,
           ln1_g, ln1_b, peer_wq, peer_subkeys, peer_u, peer_v, ln2_g, ln2_b):
    B, T, D = x.shape
    L = w_ada.shape[0]
    N = B * T
    c_pad = jnp.zeros((8, D), BF16).at[:B].set(c.astype(BF16))
    mod = ada_project(c_pad, w_ada, b_ada.reshape(L, 1, 6 * D))
    mods = [mod[l].reshape(8, 1, 6 * D) for l in range(L)]
    xf = x.reshape(N, D)
    u = modulate(xf, mods[0], 1, 0, T)
    shared = dict(w_in_rows=column_rows_view(w_in), n_layers=L, w_out=w_out, peer_wq=peer_wq,
                  w_gate=cast_bf16(w_gate), b_gate=b_gate.reshape(L, 1, 3 * D),
                  peer_u=cast_bf16(peer_u), peer_v=cast_bf16(peer_v))
    for l in range(L):
        p = _layer_params(l, shared, b_forget, cmp_pe, cmp_w1, cmp_w2, w_branch, peer_subkeys)
        y = _hybrid_mixer(u, p, B=B, T=T)
        xf, u, ut = postnorm(xf, y, mods[l], 2, ln1_g[l].reshape(1, D), ln1_b[l].reshape(1, D),
                             mods[l], 4, 3, T, transposed=True)
        y = _peer_ffn(u, ut, p)
        nxt = mods[min(l + 1, L - 1)]
        xf, u = postnorm(xf, y, mods[l], 5, ln2_g[l].reshape(1, D), ln2_b[l].reshape(1, D),
                         nxt, 1, 0, T)
    return xf.reshape(B, T, D)
```

```python
import functools

import numpy as np
import jax
import jax.numpy as jnp
from jax import lax
from jax.experimental import pallas as pl
from jax.experimental.pallas import tpu as pltpu

F32 = jnp.float32
BF16 = jnp.bfloat16

D_MODEL = 4096
HEAD_DIM = 128
NSA_HEADS = 12
NSA_KV_HEADS = 3
NSA_GROUP = NSA_HEADS // NSA_KV_HEADS
CMP_LEN = 32
CMP_STRIDE = 16
SLC_BLK = 64
N_SEL = 16
WIN = 512
FOX_HEADS = 8
DIL_PAIRS = ((128, 1), (512, 4), (2048, 16))
DIL_GROUPS = 3
DIL_HPG = 4
DIL_HEADS = DIL_GROUPS * DIL_HPG
ALIBI_HEADS = NSA_HEADS + DIL_HEADS
NSA_Q_W = NSA_HEADS * HEAD_DIM
NSA_KV_W = NSA_KV_HEADS * HEAD_DIM
FOX_W = FOX_HEADS * HEAD_DIM
DIL_W = DIL_HEADS * HEAD_DIM
OFF_NSA_KV = NSA_Q_W
OFF_NSA_GATE = OFF_NSA_KV + 6 * NSA_KV_W
OFF_FOX = OFF_NSA_GATE + 3 * NSA_HEADS
OFF_FOX_F = OFF_FOX + 3 * FOX_W
OFF_DIL = OFF_FOX_F + FOX_HEADS
IN_COLS = OFF_DIL + 3 * DIL_W
NSA_A_W = NSA_Q_W + 6 * NSA_KV_W
OUT_A = NSA_Q_W
OUT_B = FOX_W
OUT_C = DIL_HPG * HEAD_DIM
PEER_HEADS = 8
PEER_NKEYS = 128
PEER_EXPERTS = PEER_NKEYS ** 2
PEER_TOPK = 16
PEER_QDIM = 256
DEPTH = 2
ALPHA = (2.0 * DEPTH) ** 0.25
LN_EPS = 1e-5
NEG = -1e30
BIG = 1e30
SCALE = HEAD_DIM ** -0.5

LANES = 128
GATE_LANE0 = 0
FORGET_LANE0 = 3 * NSA_HEADS
VMEM_LIMIT = 56 * 1024 * 1024


def _cp(*sem):
    return pltpu.CompilerParams(dimension_semantics=sem, vmem_limit_bytes=VMEM_LIMIT)


def _alibi_slopes(n):
    return (np.float32(2.0) ** (-8.0 * np.arange(1, n + 1, dtype=np.float32) / n)).astype(np.float32)


def _gelu(x):
    return 0.5 * x * (1.0 + jnp.tanh(np.float32(np.sqrt(2.0 / np.pi)) * (x + np.float32(0.044715) * (x * x * x))))


def _split3(x):
    hi = x.astype(BF16)
    r1 = x - hi.astype(F32)
    mid = r1.astype(BF16)
    lo = (r1 - mid.astype(F32)).astype(BF16)
    return hi, mid, lo


def _dot3(x, m):
    hi, mid, lo = _split3(x)
    d = functools.partial(jnp.dot, preferred_element_type=F32)
    return (d(lo, m) + d(mid, m)) + d(hi, m)


def _dot_nt(a, b):
    return lax.dot_general(a, b, (((1,), (1,)), ((), ())), preferred_element_type=F32)


def _ada_kernel(c_ref, w_ref, b_ref, o_ref):
    o_ref[...] = jnp.dot(c_ref[...], w_ref[...].astype(BF16), preferred_element_type=F32) + b_ref[...]


def ada_project(c_pad, w_ada, b_ada, tn=512):
    L, D, N6 = w_ada.shape
    return pl.pallas_call(
        _ada_kernel,
        grid=(L, N6 // tn),
        in_specs=[pl.BlockSpec((8, D), lambda l, j: (0, 0)),
                  pl.BlockSpec((None, D, tn), lambda l, j: (l, 0, j)),
                  pl.BlockSpec((None, 1, tn), lambda l, j: (l, 0, j))],
        out_specs=pl.BlockSpec((None, 8, tn), lambda l, j: (l, 0, j)),
        out_shape=jax.ShapeDtypeStruct((L, 8, N6), F32),
        compiler_params=_cp("parallel", "parallel"),
        name="ada_project",
    )(c_pad, w_ada, b_ada)


def _modulate_kernel(x_ref, sc_ref, sh_ref, u_ref):
    u_ref[...] = (x_ref[...] * (1.0 + sc_ref[...]) + sh_ref[...]).astype(u_ref.dtype)


def modulate(x, mod3, sc_blk, sh_blk, T, tr=256):
    N, D = x.shape
    return pl.pallas_call(
        _modulate_kernel,
        grid=(N // tr,),
        in_specs=[pl.BlockSpec((tr, D), lambda i: (i, 0)),
                  pl.BlockSpec((None, 1, D), lambda i: (i * tr // T, 0, sc_blk)),
                  pl.BlockSpec((None, 1, D), lambda i: (i * tr // T, 0, sh_blk))],
        out_specs=pl.BlockSpec((tr, D), lambda i: (i, 0)),
        out_shape=jax.ShapeDtypeStruct((N, D), BF16),
        compiler_params=_cp("parallel"),
        name="modulate",
    )(x, mod3, mod3)


def _mm_kernel(a_ref, w_ref, o_ref, wb_ref):
    @pl.when(pl.program_id(1) == 0)
    def _():
        wb_ref[...] = w_ref[...].astype(BF16)

    o_ref[...] = jnp.dot(a_ref[...], wb_ref[...], preferred_element_type=F32).astype(o_ref.dtype)


def matmul(a, w, out_dtype, layer, tm=1024, tn=512):
    M, K = a.shape
    N = w.shape[2]
    assert M % tm == 0 and N % tn == 0
    return pl.pallas_call(
        _mm_kernel,
        grid=(N // tn, M // tm),
        in_specs=[pl.BlockSpec((tm, K), lambda j, i: (i, 0)),
                  pl.BlockSpec((None, K, tn), lambda j, i: (layer, 0, j))],
        out_specs=pl.BlockSpec((tm, tn), lambda j, i: (i, j)),
        out_shape=jax.ShapeDtypeStruct((M, N), out_dtype),
        scratch_shapes=[pltpu.VMEM((K, tn), BF16)],
        compiler_params=_cp("parallel", "arbitrary"),
        name="matmul",
    )(a, w)


def _gather_layer_rows(w_ref, wb_ref, layer, n_layers):
    tn, K = wb_ref.shape
    kt_n = K // LANES
    stride = kt_n * n_layers
    sub = 16
    for g in range(tn // sub):
        for kt in range(kt_n):
            first = (g * sub * kt_n + kt) * n_layers + layer
            wb_ref[g * sub:(g + 1) * sub, kt * LANES:(kt + 1) * LANES] = (
                w_ref[pl.ds(first, sub, stride=stride), :].astype(BF16))


def _mm_wt_kernel(a_ref, w_ref, o_ref, wb_ref, *, layer, n_layers):
    @pl.when(pl.program_id(1) == 0)
    def _():
        _gather_layer_rows(w_ref, wb_ref, layer, n_layers)

    o_ref[...] = _dot_nt(a_ref[...], wb_ref[...]).astype(o_ref.dtype)


def column_rows_view(w):
    L, K, N = w.shape
    return w.transpose(2, 0, 1).reshape(N, L, K // LANES, LANES).transpose(0, 2, 1, 3).reshape(
        N * (K // LANES) * L, LANES)


def _column_window(n_layers, K, tn, start_col):
    rows_per_col = (K // LANES) * n_layers
    return pl.BlockSpec((pl.Element(tn * rows_per_col), pl.Element(LANES)),
                        lambda *ids: (start_col(*ids) * rows_per_col, 0), pipeline_mode=pl.Buffered(1))


def matmul_wt(a, w_rows, out_dtype, n_layers, layer, col0, n_cols, tm=512, tn=768):
    M, K = a.shape
    assert M % tm == 0 and n_cols % tn == 0
    return pl.pallas_call(
        functools.partial(_mm_wt_kernel, layer=layer, n_layers=n_layers),
        grid=(n_cols // tn, M // tm),
        in_specs=[pl.BlockSpec((tm, K), lambda j, i: (i, 0)),
                  _column_window(n_layers, K, tn, lambda j, i: col0 + j * tn)],
        out_specs=pl.BlockSpec((tm, tn), lambda j, i: (i, j)),
        out_shape=jax.ShapeDtypeStruct((M, n_cols), out_dtype),
        scratch_shapes=[pltpu.VMEM((tn, K), BF16)],
        compiler_params=_cp("parallel", "arbitrary"),
        name="matmul_wt",
    )(a, w_rows)


def _small_proj_kernel(a_ref, wg_ref, wf_ref, o_ref, wbg_ref, wbf_ref, *, layer, n_layers):
    @pl.when(pl.program_id(0) == 0)
    def _():
        _gather_layer_rows(wg_ref, wbg_ref, layer, n_layers)
        _gather_layer_rows(wf_ref, wbf_ref, layer, n_layers)

    a = a_ref[...]
    lane = lax.broadcasted_iota(jnp.int32, o_ref.shape, 1)
    o_ref[...] = jnp.where(lane < FORGET_LANE0, _dot_nt(a, wbg_ref[...]), _dot_nt(a, wbf_ref[...]))


def small_projection(a, w_rows, n_layers, layer, tm=1024):
    M, K = a.shape
    return pl.pallas_call(
        functools.partial(_small_proj_kernel, layer=layer, n_layers=n_layers),
        grid=(M // tm,),
        in_specs=[pl.BlockSpec((tm, K), lambda i: (i, 0)),
                  _column_window(n_layers, K, LANES, lambda i: OFF_NSA_GATE - GATE_LANE0),
                  _column_window(n_layers, K, LANES, lambda i: OFF_FOX_F - FORGET_LANE0)],
        out_specs=pl.BlockSpec((tm, LANES), lambda i: (i, 0)),
        out_shape=jax.ShapeDtypeStruct((M, LANES), F32),
        scratch_shapes=[pltpu.VMEM((LANES, K), BF16), pltpu.VMEM((LANES, K), BF16)],
        compiler_params=_cp("arbitrary"),
        name="small_projection",
    )(a, w_rows, w_rows)


HEADS_PER_STEP = 4


def _head_cols(i):
    return slice(i * HEAD_DIM, (i + 1) * HEAD_DIM)


def _flash(qs, load_kv, lo, hi, chunk_fn, score_fns):
    tq = qs[0].shape[0]

    def body(c, carry):
        shared = chunk_fn(c)
        kvs = [load_kv(c, i) for i in range(len(qs))]
        raws = [_dot_nt(q, kvs[i][0]) for i, q in enumerate(qs)]
        mids = []
        for i, raw in enumerate(raws):
            m, l, _ = carry[i]
            s = score_fns[i](raw, c, shared)
            m_new = jnp.maximum(m, jnp.max(s, axis=1, keepdims=True))
            alpha = jnp.exp(m - m_new)
            p = jnp.exp(s - m_new)
            mids.append((m_new, alpha, alpha * l + jnp.sum(p, axis=1, keepdims=True), p))
        out = []
        for i, (m_new, alpha, l, p) in enumerate(mids):
            v = kvs[i][1]
            acc = alpha * carry[i][2] + jnp.dot(p.astype(v.dtype), v, preferred_element_type=F32)
            out.append((m_new, l, acc))
        return tuple(out)

    one = (jnp.full((tq, 1), NEG, F32), jnp.zeros((tq, 1), F32), jnp.zeros((tq, HEAD_DIM), F32))
    return lax.fori_loop(lo, hi, body, tuple(one for _ in qs))


def _flash_t(qs, load_kv, lo, hi, chunk_fn, score_fns):
    tq = qs[0].shape[0]

    def body(c, stats):
        shared = chunk_fn(c)
        raws = [_dot_nt(load_kv(c, i)[0], q) for i, q in enumerate(qs)]
        mids = []
        for i, raw in enumerate(raws):
            m, l, _ = stats[i]
            s = score_fns[i](raw, c, shared)
            m_new = jnp.maximum(m, jnp.max(s, axis=0, keepdims=True))
            alpha = jnp.exp(m - m_new)
            p = jnp.exp(s - m_new)
            mids.append((m_new, alpha, alpha * l + jnp.sum(p, axis=0, keepdims=True), p))
        out = []
        for i, (m_new, alpha, l, p) in enumerate(mids):
            v = load_kv(c, i)[1]
            pv = lax.dot_general(v, p.astype(v.dtype), (((0,), (0,)), ((), ())), preferred_element_type=F32)
            out.append((m_new, l, alpha * stats[i][2] + pv))
        return tuple(out)

    one = (jnp.full((1, tq), NEG, F32), jnp.zeros((1, tq), F32), jnp.zeros((HEAD_DIM, tq), F32))
    return lax.fori_loop(lo, hi, body, tuple(one for _ in qs))


def _alibi_kernel(slope_ref, q_ref, k_ref, v_ref, *rest, tq, tk, max_dist, shared_kv, with_sel, with_lse):
    rest = list(rest)
    sel_ref = rest.pop(0) if with_sel else None
    o_ref = rest.pop(0)
    lse_ref = rest.pop(0) if with_lse else None
    hb = pl.program_id(1)
    q0 = pl.program_id(3) * tq
    nh = HEADS_PER_STEP
    rc = lax.broadcasted_iota(jnp.int32, (tk, tq), 1) - lax.broadcasted_iota(jnp.int32, (tk, tq), 0)
    if with_sel:
        sel_t = sel_ref[...].astype(F32).T.astype(BF16)
        lane_blk = lax.broadcasted_iota(jnp.int32, (tk, LANES), 1)
        key_blk = lax.broadcasted_iota(jnp.int32, (tk, LANES), 0) // SLC_BLK

    def valid_fn(c):
        dist = rc + (q0 - c * tk)
        valid = dist >= 0
        if max_dist is not None:
            valid = valid & (dist <= max_dist)
        if with_sel:
            expand = (lane_blk == key_blk + (hb * (LANES // 4) + c * (tk // SLC_BLK))).astype(BF16)
            valid = valid & (jnp.dot(expand, sel_t, preferred_element_type=F32) > 0.5)
        return dist.astype(F32), valid

    def make_score(i):
        slope = slope_ref[hb * nh + i]

        def score_fn(s, c, shared):
            dist_f, valid = shared
            return jnp.where(valid, s * SCALE - slope * dist_f, NEG)
        return score_fn

    def load_kv(c, i):
        off = pl.multiple_of(c * tk, tk)
        cols = _head_cols(0 if shared_kv else i)
        return k_ref[pl.ds(off, tk), cols], v_ref[pl.ds(off, tk), cols]

    hi = (q0 + tq - 1) // tk + 1
    lo = 0 if max_dist is None else jnp.maximum(q0 - max_dist, 0) // tk
    res = _flash_t([q_ref[:, _head_cols(i)] for i in range(nh)], load_kv, lo, hi, valid_fn,
                   [make_score(i) for i in range(nh)])
    for i, (m, l, acc_t) in enumerate(res):
        o_ref[:, _head_cols(i)] = (acc_t / l).T.astype(o_ref.dtype)
        if with_lse:
            lse_ref[:, _head_cols(i)] = jnp.broadcast_to(m + jnp.log(l), (HEAD_DIM, tq)).T


def alibi_attention(qa, ka, va, slopes, *, B, Ts, n_sub, n_head_blocks, shared_kv, q_col, k_col, v_col,
                    max_dist, tq, tk, sel=None, with_lse=False):
    tq = min(tq, Ts)
    tk = min(tk, Ts)
    nq = Ts // tq
    rows = B * Ts
    wq = HEADS_PER_STEP * HEAD_DIM
    wkv = HEAD_DIM if shared_kv else wq
    in_specs = [pl.BlockSpec((tq, wq), lambda b, h, r, i, s: (b * nq + i, q_col(h, r))),
                pl.BlockSpec((Ts, wkv), lambda b, h, r, i, s: (b, k_col(h, r))),
                pl.BlockSpec((Ts, wkv), lambda b, h, r, i, s: (b, v_col(h, r)))]
    args = [qa, ka, va]
    if sel is not None:
        in_specs.append(pl.BlockSpec((tq, LANES), lambda b, h, r, i, s: (b * nq + i, 0)))
        args.append(sel)
    o_spec = pl.BlockSpec((tq, wq), lambda b, h, r, i, s: (b * nq + i, r * n_head_blocks + h))
    o_shape = jax.ShapeDtypeStruct((rows, n_sub * n_head_blocks * wq), F32)
    kern = functools.partial(_alibi_kernel, tq=tq, tk=tk, max_dist=max_dist, shared_kv=shared_kv,
                             with_sel=sel is not None, with_lse=with_lse)
    return pl.pallas_call(
        kern,
        grid_spec=pltpu.PrefetchScalarGridSpec(
            num_scalar_prefetch=1, grid=(B, n_head_blocks, n_sub, nq),
            in_specs=in_specs,
            out_specs=[o_spec, o_spec] if with_lse else o_spec),
        out_shape=[o_shape, o_shape] if with_lse else o_shape,
        compiler_params=_cp("parallel", "parallel", "parallel", "parallel"),
        name="alibi_attention",
    )(jnp.asarray(slopes, F32), *args)


def _fox_kernel(q_ref, k_ref, v_ref, fq_ref, fk_ref, o_ref, *, tq, tk):
    hb = pl.program_id(1)
    q0 = pl.program_id(2) * tq
    nh = HEADS_PER_STEP
    lane = lax.broadcasted_iota(jnp.int32, (tq, LANES), 1)
    f_tok = fq_ref[...]
    rc = lax.broadcasted_iota(jnp.int32, (tq, tk), 0) - lax.broadcasted_iota(jnp.int32, (tq, tk), 1)

    def make_score(i):
        fq = jnp.sum(jnp.where(lane == FORGET_LANE0 + hb * nh + i, f_tok, 0.0), axis=1, keepdims=True)

        def score_fn(s, c, causal):
            return jnp.where(causal, (s * SCALE + fq) - fk_ref[i, pl.ds(c, 1), :], NEG)
        return score_fn

    def load_kv(c, i):
        off = pl.multiple_of(c * tk, tk)
        return k_ref[pl.ds(off, tk), _head_cols(i)], v_ref[pl.ds(off, tk), _head_cols(i)]

    hi = (q0 + tq - 1) // tk + 1
    res = _flash([q_ref[:, _head_cols(i)] for i in range(nh)], load_kv, 0, hi,
                 lambda c: rc + (q0 - c * tk) >= 0, [make_score(i) for i in range(nh)])
    for i, (_, l, acc) in enumerate(res):
        o_ref[:, _head_cols(i)] = (acc / l).astype(o_ref.dtype)


def fox_attention(proj_fox, f_tok, f_rows, *, B, T, tq=256, tk=256):
    nq = T // tq
    nh = HEADS_PER_STEP
    nhb = FOX_HEADS // nh
    wq = nh * HEAD_DIM
    return pl.pallas_call(
        functools.partial(_fox_kernel, tq=tq, tk=tk),
        grid=(B, nhb, nq),
        in_specs=[pl.BlockSpec((tq, wq), lambda b, h, i: (b * nq + i, h)),
                  pl.BlockSpec((T, wq), lambda b, h, i: (b, nhb + h)),
                  pl.BlockSpec((T, wq), lambda b, h, i: (b, 2 * nhb + h)),
                  pl.BlockSpec((tq, LANES), lambda b, h, i: (b * nq + i, 0)),
                  pl.BlockSpec((nh, T // tk, tk), lambda b, h, i: (b * nhb + h, 0, 0))],
        out_specs=pl.BlockSpec((tq, wq), lambda b, h, i: (b * nq + i, h)),
        out_shape=jax.ShapeDtypeStruct((B * T, FOX_W), BF16),
        compiler_params=_cp("parallel", "parallel", "parallel"),
        name="fox_attention",
    )(proj_fox, proj_fox, proj_fox, f_tok, f_rows)


def _cumsum_kernel(x_ref, b_ref, o_ref, *, T, tr):
    i = pl.program_id(1)
    z = x_ref[...] + b_ref[...]
    logf = jnp.minimum(z, 0.0) - jnp.log1p(jnp.exp(-jnp.abs(z)))
    row = lax.broadcasted_iota(jnp.int32, (tr, T), 0) + i * tr
    col = lax.broadcasted_iota(jnp.int32, (tr, T), 1)
    o_ref[...] = _dot3_lhs((row >= col).astype(BF16), logf)


def _dot3_lhs(m, x):
    hi, mid, lo = _split3(x)
    d = functools.partial(jnp.dot, preferred_element_type=F32)
    return (d(m, lo) + d(m, mid)) + d(m, hi)


def forget_cumsum(small, bias_row, *, B, T, tr=256):
    return pl.pallas_call(
        functools.partial(_cumsum_kernel, T=T, tr=tr),
        grid=(B, T // tr),
        in_specs=[pl.BlockSpec((T, LANES), lambda b, i: (b, 0)),
                  pl.BlockSpec((1, LANES), lambda b, i: (0, 0))],
        out_specs=pl.BlockSpec((tr, LANES), lambda b, i: (b * (T // tr) + i, 0)),
        out_shape=jax.ShapeDtypeStruct((B * T, LANES), F32),
        compiler_params=_cp("parallel", "parallel"),
        name="forget_cumsum",
    )(small, bias_row)


def _compress_kernel(x_ref, pe_ref, w1_ref, w2_ref, o_ref):
    x = x_ref[...].astype(F32)
    y0 = jnp.dot((x + pe_ref[0]).astype(BF16), w1_ref[0], preferred_element_type=F32)
    y1 = jnp.dot((x + pe_ref[1]).astype(BF16), w1_ref[1], preferred_element_type=F32)
    n = y1.shape[0]
    hmid = _gelu(y0 + pltpu.roll(y1, n - 1, 0))
    o_ref[...] = jnp.dot(hmid.astype(BF16), w2_ref[...], preferred_element_type=F32).astype(o_ref.dtype)


def compress_blocks(chunks, pe, w1, w2):
    _, G, nch, cw = chunks.shape
    return pl.pallas_call(
        _compress_kernel,
        grid=(2, G),
        in_specs=[pl.BlockSpec((None, None, nch, cw), lambda a, g: (a, g, 0, 0)),
                  pl.BlockSpec((None, 2, 1, cw), lambda a, g: (a, 0, 0, 0)),
                  pl.BlockSpec((None, 2, cw, HEAD_DIM), lambda a, g: (a, 0, 0, 0)),
                  pl.BlockSpec((None, HEAD_DIM, HEAD_DIM), lambda a, g: (a, 0, 0))],
        out_specs=pl.BlockSpec((None, None, nch, HEAD_DIM), lambda a, g: (a, g, 0, 0)),
        out_shape=jax.ShapeDtypeStruct((2, G, nch, HEAD_DIM), BF16),
        compiler_params=_cp("parallel", "parallel"),
        name="compress_blocks",
    )(chunks, pe, w1, w2)


def _cmp_kernel(slope_ref, q_ref, kc_ref, vc_ref, smat_ref, o_ref, sel_ref, *, tq, n_cmp, n_slc):
    q0 = pl.program_id(1) * tq
    nk = kc_ref.shape[1]
    pos = lax.broadcasted_iota(jnp.int32, (tq, nk), 0) + q0
    n_idx = lax.broadcasted_iota(jnp.int32, (tq, nk), 1)
    dist_c = pos - (n_idx * CMP_STRIDE + (CMP_LEN - 1))
    valid_c = (dist_c >= 0) & (n_idx < n_cmp)
    dist_f = dist_c.astype(F32)
    p_slc = jnp.zeros((tq, LANES), F32)
    for kv in range(NSA_KV_HEADS):
        kc = kc_ref[kv]
        vc = vc_ref[kv]
        imp = jnp.zeros((tq, nk), F32)
        for g in range(NSA_GROUP):
            hh = kv * NSA_GROUP + g
            q = q_ref[:, hh * HEAD_DIM:(hh + 1) * HEAD_DIM]
            s = _dot_nt(q, kc) * SCALE - slope_ref[hh] * dist_f
            s = jnp.where(valid_c, s, NEG)
            e = jnp.where(valid_c, jnp.exp(s - jnp.max(s, axis=1, keepdims=True)), 0.0)
            p = e / jnp.maximum(jnp.sum(e, axis=1, keepdims=True), 1e-30)
            o_ref[:, hh * HEAD_DIM:(hh + 1) * HEAD_DIM] = jnp.dot(
                p.astype(BF16), vc, preferred_element_type=F32)
            imp = imp + p
        p_slc = p_slc + _dot3(imp, smat_ref[kv])
    grp = LANES // 4
    p_t = p_slc.T
    j = lax.broadcasted_iota(jnp.int32, (grp, tq), 0)
    qblk = (lax.broadcasted_iota(jnp.int32, (grp, tq), 1) + q0) // SLC_BLK
    forced = (j == 0) | (j == qblk) | (j == qblk - 1)
    causal = j <= qblk
    chosen = []
    for kv in range(NSA_KV_HEADS):
        score = jnp.where(forced, BIG, jnp.where(causal, p_t[kv * grp:(kv + 1) * grp], NEG))
        rank = jnp.zeros((grp, tq), jnp.int32)
        for k in range(n_slc):
            row = score[k:k + 1]
            ahead = (row > score) | ((row == score) & (k < j))
            rank = rank + ahead.astype(jnp.int32)
        chosen.append(jnp.where(causal & (rank < min(N_SEL, n_slc)), 1.0, 0.0))
    chosen.append(jnp.zeros((LANES - NSA_KV_HEADS * grp, tq), F32))
    sel_ref[...] = jnp.concatenate(chosen, axis=0).T.astype(sel_ref.dtype)


def cmp_attention(proj_a, kvc, smat, slopes, *, B, T, tq=256):
    nq = T // tq
    nch = kvc.shape[2]
    n_cmp = (T - CMP_LEN) // CMP_STRIDE + 1
    n_slc = T // SLC_BLK
    kvc4 = kvc.reshape(2, B, NSA_KV_HEADS, nch, HEAD_DIM)
    return pl.pallas_call(
        functools.partial(_cmp_kernel, tq=tq, n_cmp=n_cmp, n_slc=n_slc),
        grid_spec=pltpu.PrefetchScalarGridSpec(
            num_scalar_prefetch=1, grid=(B, nq),
            in_specs=[pl.BlockSpec((tq, NSA_Q_W), lambda b, i, s: (b * nq + i, 0)),
                      pl.BlockSpec((None, None, NSA_KV_HEADS, nch, HEAD_DIM), lambda b, i, s: (0, b, 0, 0, 0)),
                      pl.BlockSpec((None, None, NSA_KV_HEADS, nch, HEAD_DIM), lambda b, i, s: (1, b, 0, 0, 0)),
                      pl.BlockSpec((NSA_KV_HEADS, nch, LANES), lambda b, i, s: (0, 0, 0))],
            out_specs=[pl.BlockSpec((tq, NSA_Q_W), lambda b, i, s: (b * nq + i, 0)),
                       pl.BlockSpec((tq, LANES), lambda b, i, s: (b * nq + i, 0))]),
        out_shape=[jax.ShapeDtypeStruct((B * T, NSA_Q_W), F32),
                   jax.ShapeDtypeStruct((B * T, LANES), BF16)],
        compiler_params=_cp("parallel", "parallel"),
        name="cmp_attention",
    )(jnp.asarray(slopes, F32), proj_a, kvc4, kvc4, smat)


def _pool_matrix(nch, n_slc):
    r = SLC_BLK // CMP_STRIDE
    c_over = CMP_LEN // CMP_STRIDE
    n_cmp_max = nch
    m = np.zeros((NSA_KV_HEADS, nch, LANES), np.float32)
    for kv in range(NSA_KV_HEADS):
        for j in range(n_slc):
            for a in range(r):
                for b in range(c_over):
                    i = r * j + a + b
                    if i < n_cmp_max:
                        m[kv, i, kv * (LANES // 4) + j] += 1.0
    return m


def _nsa_merge_kernel(oc_ref, os_ref, ow_ref, g_ref, e_ref, o_ref):
    sig = jax.nn.sigmoid(g_ref[...])
    out = _dot3(sig, e_ref[0]) * oc_ref[...]
    out = out + _dot3(sig, e_ref[1]) * os_ref[...]
    out = out + _dot3(sig, e_ref[2]) * ow_ref[...]
    o_ref[...] = out.astype(o_ref.dtype)


def nsa_merge(o_cmp, o_slc, o_win, small, tr=256):
    N = o_cmp.shape[0]
    e = np.zeros((3, LANES, NSA_Q_W), np.float32)
    for br in range(3):
        for h in range(NSA_HEADS):
            e[br, GATE_LANE0 + h * 3 + br, h * HEAD_DIM:(h + 1) * HEAD_DIM] = 1.0
    row = pl.BlockSpec((tr, NSA_Q_W), lambda i: (i, 0))
    return pl.pallas_call(
        _nsa_merge_kernel,
        grid=(N // tr,),
        in_specs=[row, row, row,
                  pl.BlockSpec((tr, LANES), lambda i: (i, 0)),
                  pl.BlockSpec((3, LANES, NSA_Q_W), lambda i: (0, 0, 0))],
        out_specs=row,
        out_shape=jax.ShapeDtypeStruct((N, NSA_Q_W), BF16),
        compiler_params=_cp("parallel"),
        name="nsa_merge",
    )(o_cmp, o_slc, o_win, small, jnp.asarray(e, BF16))


def _dil_merge_kernel(o0, o1, o2, l0, l1, l2, o_ref):
    a, b, c = l0[...], l1[...], l2[...]
    m = jnp.maximum(jnp.maximum(a, b), c)
    ea, eb, ec = jnp.exp(a - m), jnp.exp(b - m), jnp.exp(c - m)
    z = ea + eb + ec
    o_ref[...] = ((ea / z) * o0[...] + (eb / z) * o1[...] + (ec / z) * o2[...]).astype(o_ref.dtype)


def dil_merge(outs, lses, tr=512):
    N, W = outs[0].shape
    row = pl.BlockSpec((tr, W), lambda i: (i, 0))
    return pl.pallas_call(
        _dil_merge_kernel,
        grid=(N // tr,),
        in_specs=[row] * 6,
        out_specs=row,
        out_shape=jax.ShapeDtypeStruct((N, W), BF16),
        compiler_params=_cp("parallel"),
        name="dil_merge",
    )(*outs, *lses)


def _branch_kernel(u_ref, wga, wgb, wgc, bga, bgb, bgc, oa, ob, oc, wa, wb, wc, z_ref):
    d = functools.partial(jnp.dot, preferred_element_type=F32)
    u = u_ref[...]
    z = jax.nn.sigmoid(d(u, wga[...]) + bga[...]) * d(oa[...], wa[...])
    z = z + jax.nn.sigmoid(d(u, wgb[...]) + bgb[...]) * d(ob[...], wb[...])
    z = z + jax.nn.sigmoid(d(u, wgc[...]) + bgc[...]) * d(oc[...], wc[...])
    z_ref[...] = z.astype(z_ref.dtype)


def gated_branches(u, w_gate, b_gate, layer, o_a, o_b, o_c, wb_a, wb_b, wb_c, tm=512, tn=512):
    N, D = u.shape
    nj = D // tn

    def wg(k):
        return pl.BlockSpec((None, D, tn), lambda j, i: (layer, 0, k * nj + j))

    def bg(k):
        return pl.BlockSpec((None, 1, tn), lambda j, i: (layer, 0, k * nj + j))

    def act(w):
        return pl.BlockSpec((tm, w), lambda j, i: (i, 0))

    def wup(w):
        return pl.BlockSpec((w, tn), lambda j, i: (0, j))

    return pl.pallas_call(
        _branch_kernel,
        grid=(nj, N // tm),
        in_specs=[act(D), wg(0), wg(1), wg(2), bg(0), bg(1), bg(2),
                  act(OUT_A), act(OUT_B), act(OUT_C), wup(OUT_A), wup(OUT_B), wup(OUT_C)],
        out_specs=pl.BlockSpec((tm, tn), lambda j, i: (i, j)),
        out_shape=jax.ShapeDtypeStruct((N, D), BF16),
        compiler_params=_cp("parallel", "parallel"),
        name="gated_branches",
    )(u, w_gate, w_gate, w_gate, b_gate, b_gate, b_gate, o_a, o_b, o_c, wb_a, wb_b, wb_c)


def _postnorm_kernel(x_ref, y_ref, g_ref, gam_ref, bet_ref, sc_ref, sh_ref, xo_ref, u_ref, *ut_ref):
    h = ALPHA * x_ref[...] + (1.0 + g_ref[...]) * y_ref[...]
    mu = jnp.mean(h, axis=-1, keepdims=True)
    hc = h - mu
    var = jnp.mean(hc * hc, axis=-1, keepdims=True)
    xn = hc * lax.rsqrt(var + LN_EPS) * gam_ref[...] + bet_ref[...]
    xo_ref[...] = xn
    u = xn * (1.0 + sc_ref[...]) + sh_ref[...]
    u_ref[...] = u.astype(u_ref.dtype)
    if ut_ref:
        ut_ref[0][...] = u.T.astype(u_ref.dtype)


def postnorm(x, y, mod_g, g_blk, gamma, beta, mod_n, sc_blk, sh_blk, T, tr=256, transposed=False):
    N, D = x.shape
    row = pl.BlockSpec((tr, D), lambda i: (i, 0))
    vec = pl.BlockSpec((1, D), lambda i: (0, 0))

    def mod(blk):
        return pl.BlockSpec((None, 1, D), lambda i: (i * tr // T, 0, blk))

    out_specs = [row, row]
    out_shape = [jax.ShapeDtypeStruct((N, D), F32), jax.ShapeDtypeStruct((N, D), BF16)]
    if transposed:
        out_specs.append(pl.BlockSpec((D, tr), lambda i: (0, i)))
        out_shape.append(jax.ShapeDtypeStruct((D, N), BF16))
    return pl.pallas_call(
        _postnorm_kernel,
        grid=(N // tr,),
        in_specs=[row, row, mod(g_blk), vec, vec, mod(sc_blk), mod(sh_blk)],
        out_specs=out_specs,
        out_shape=out_shape,
        compiler_params=_cp("parallel"),
        name="postnorm",
    )(x, y, mod_g, gamma, beta, mod_n, mod_n)


def _topk_rows(s, k, break_ties):
    rows, tn = s.shape
    iota = lax.broadcasted_iota(jnp.int32, (rows, tn), 0)
    rank = jnp.full((rows, tn), k, jnp.int32)
    cur = s
    tops = []
    for r in range(k):
        m = jnp.max(cur, axis=0, keepdims=True)
        hit = cur == m
        if break_ties:
            hit = iota == jnp.min(jnp.where(hit, iota, rows), axis=0, keepdims=True)
        rank = jnp.where(hit, r, rank)
        cur = jnp.where(hit, -jnp.inf, cur)
        tops.append(m)
    return rank, tops


def _merge_topk(top1, top2, k):
    t1 = _stack_rows(top1)
    tn = t1.shape[1]
    iota = lax.broadcasted_iota(jnp.int32, (k, tn), 0)
    taken = jnp.zeros((k, tn), jnp.int32)
    front = t1 + top2[0]
    m0 = top1[0] + top2[0]
    z = jnp.zeros((1, tn), F32)
    for _ in range(k):
        m = jnp.max(front, axis=0, keepdims=True)
        hit = iota == jnp.min(jnp.where(front == m, iota, k), axis=0, keepdims=True)
        z = z + jnp.exp(m - m0)
        nxt = jnp.sum(jnp.where(hit, taken, 0), axis=0, keepdims=True) + 1
        follower = jnp.full((1, tn), -jnp.inf, F32)
        for rb in range(1, k):
            follower = jnp.where(nxt == rb, top2[rb], follower)
        t1_hit = jnp.sum(jnp.where(hit, t1, 0.0), axis=0, keepdims=True)
        taken = taken + hit.astype(jnp.int32)
        front = jnp.where(hit, t1_hit + follower, front)
    return taken.astype(F32), z


def _stack_rows(rows_list):
    n = len(rows_list)
    tn = rows_list[0].shape[1]
    iota = lax.broadcasted_iota(jnp.int32, (n, tn), 0)
    out = jnp.zeros((n, tn), rows_list[0].dtype)
    for r, row in enumerate(rows_list):
        out = jnp.where(iota == r, row, out)
    return out


def _peer_select_kernel(q_ref, sk_ref, e1_ref, l1_ref, e2_ref, r2_ref, *, cpt):
    K = PEER_TOPK
    half = PEER_QDIM // 2
    s1 = _dot_nt(sk_ref[0], q_ref[:, :half])
    s2 = _dot_nt(sk_ref[1], q_ref[:, half:])

    def emit(break_ties):
        rank1, top1 = _topk_rows(s1, K, break_ties)
        rank2, top2 = _topk_rows(s2, K, break_ties)
        row_len, z = _merge_topk(top1, top2, K)
        l1 = jnp.zeros(s1.shape, F32)
        for ra in range(K):
            l1 = jnp.where(rank1 == ra, row_len[ra:ra + 1], l1)
        e1 = jnp.where(rank1 < K, jnp.exp(s1 - top1[0]) / z, 0.0)
        e2 = jnp.where(rank2 < K, jnp.exp(s2 - top2[0]), 0.0)
        for j in range(PEER_NKEYS // cpt):
            e1_ref[j] = e1[j * cpt:(j + 1) * cpt]
            l1_ref[j] = l1[j * cpt:(j + 1) * cpt]
        e2_ref[...] = e2.astype(BF16)
        r2_ref[...] = rank2.astype(BF16)
        members = (rank1 < K).astype(F32) + (rank2 < K).astype(F32)
        return jnp.sum(jnp.abs(jnp.sum(members, axis=0, keepdims=True) - 2.0 * K))

    tied = emit(False)

    @pl.when(tied != 0.0)
    def _():
        emit(True)


def peer_select(q, subkeys, *, cpt, tn=256):
    N = q.shape[0]
    H = PEER_HEADS
    nt = PEER_NKEYS // cpt
    s1 = pl.BlockSpec((nt, None, cpt, tn), lambda i, h: (0, h, 0, i))
    s2 = pl.BlockSpec((None, PEER_NKEYS, tn), lambda i, h: (h, 0, i))
    sh1 = jax.ShapeDtypeStruct((nt, H, cpt, N), F32)
    sh2 = jax.ShapeDtypeStruct((H, PEER_NKEYS, N), BF16)
    return pl.pallas_call(
        functools.partial(_peer_select_kernel, cpt=cpt),
        grid=(N // tn, H),
        in_specs=[pl.BlockSpec((tn, PEER_QDIM), lambda i, h: (i, h)),
                  pl.BlockSpec((None, 2, PEER_NKEYS, PEER_QDIM // 2), lambda i, h: (h, 0, 0, 0))],
        out_specs=[s1, s1, s2, s2],
        out_shape=[sh1, sh1, sh2, sh2],
        compiler_params=_cp("parallel", "parallel"),
        name="peer_select",
    )(q, subkeys)


def _peer_dense_kernel(ut_ref, tu_ref, tv_ref, e1_ref, l1_ref, e2_ref, r2_ref, o_ref, a_s, *, cpt, tnc):
    @pl.when(pl.program_id(1) == 0)
    def _():
        o_ref[...] = jnp.zeros_like(o_ref)

    nk = PEER_NKEYS
    cph = cpt // 2
    tm = ut_ref.shape[1]
    for k in range(2):
        rows = slice(k * cph * nk, (k + 1) * cph * nk)
        for hc in range(2):
            tc = slice(hc * (tm // 2), (hc + 1) * (tm // 2))
            ht = jnp.dot(tu_ref[rows, :], ut_ref[:, tc], preferred_element_type=F32)
            for c2 in range(cph):
                cc = k * cph + c2
                w = None
                for h in range(PEER_HEADS):
                    hit = r2_ref[h, :, tc] < l1_ref[h, cc:cc + 1, tc].astype(BF16)
                    wh = jnp.where(hit, e2_ref[h, :, tc], jnp.zeros((), BF16)) * e1_ref[h, cc:cc + 1, tc].astype(BF16)
                    w = wh if w is None else w + wh
                a_s[k, tc, c2 * nk:(c2 + 1) * nk] = (w.astype(F32) * _gelu(ht[c2 * nk:(c2 + 1) * nk])).T.astype(BF16)
    for k in range(2):
        rows = slice(k * cph * nk, (k + 1) * cph * nk)
        for nt in range(o_ref.shape[1] // tnc):
            cs = slice(nt * tnc, (nt + 1) * tnc)
            o_ref[:, cs] += jnp.dot(a_s[k], tv_ref[rows, cs], preferred_element_type=F32)


def peer_dense(ut, tu, tv, layer, e1, l1, e2, r2, *, cpt, tm=512):
    D, N = ut.shape
    E = tu.shape[1]
    te = cpt * PEER_NKEYS
    H = PEER_HEADS
    once = pl.Buffered(1)
    s1 = pl.BlockSpec((None, H, cpt, tm), lambda i, j: (j, 0, 0, i))
    s2 = pl.BlockSpec((H, PEER_NKEYS, tm), lambda i, j: (0, 0, i), pipeline_mode=once)
    tab = pl.BlockSpec((None, te, D), lambda i, j: (layer, j, 0))
    return pl.pallas_call(
        functools.partial(_peer_dense_kernel, cpt=cpt, tnc=512),
        grid=(N // tm, E // te),
        in_specs=[pl.BlockSpec((D, tm), lambda i, j: (0, i), pipeline_mode=once), tab, tab, s1, s1, s2, s2],
        out_specs=pl.BlockSpec((tm, D), lambda i, j: (i, 0)),
        out_shape=jax.ShapeDtypeStruct((N, D), F32),
        scratch_shapes=[pltpu.VMEM((2, tm, te // 2), BF16)],
        compiler_params=_cp("parallel", "arbitrary"),
        name="peer_dense",
    )(ut, tu, tv, e1, l1, e2, r2)


def _hybrid_mixer(u, p, *, B, T):
    N = B * T
    sl = _alibi_slopes(ALIBI_HEADS)
    nsa_slopes = sl[0::2]
    dil_slopes = sl[1::2].reshape(DIL_GROUPS, DIL_HPG)

    lyr = p["layer"]
    nl = p["n_layers"]
    proj_a = matmul_wt(u, p["w_in_rows"], BF16, nl, lyr, 0, NSA_A_W)
    small = small_projection(u, p["w_in_rows"], nl, lyr)
    proj_fox = matmul_wt(u, p["w_in_rows"], BF16, nl, lyr, OFF_FOX, 3 * FOX_W)
    proj_dil = matmul_wt(u, p["w_in_rows"], BF16, nl, lyr, OFF_DIL, 3 * DIL_W)

    nch = T // CMP_STRIDE
    kv_cmp = proj_a[:, OFF_NSA_KV:OFF_NSA_KV + 2 * NSA_KV_W].reshape(B, T, 2, NSA_KV_HEADS, HEAD_DIM)
    chunks = kv_cmp.transpose(2, 0, 3, 1, 4).reshape(2, B * NSA_KV_HEADS, nch, CMP_STRIDE * HEAD_DIM)
    kvc = compress_blocks(chunks, p["cmp_pe"], p["cmp_w1"], p["cmp_w2"])
    smat = jnp.asarray(_pool_matrix(nch, T // SLC_BLK), BF16)
    o_cmp, sel = cmp_attention(proj_a, kvc, smat, nsa_slopes, B=B, T=T)
    qh = NSA_HEADS
    kvh = NSA_KV_HEADS
    assert NSA_GROUP == HEADS_PER_STEP and DIL_HPG == HEADS_PER_STEP
    nsa_kw = dict(B=B, Ts=T, n_sub=1, n_head_blocks=kvh, shared_kv=True,
                  q_col=lambda h, r: h, tq=256, tk=256)
    o_slc = alibi_attention(proj_a, proj_a, proj_a, nsa_slopes, max_dist=None, sel=sel,
                            k_col=lambda h, r: qh + 2 * kvh + h,
                            v_col=lambda h, r: qh + 3 * kvh + h, **nsa_kw)
    o_win = alibi_attention(proj_a, proj_a, proj_a, nsa_slopes, max_dist=WIN - 1,
                            k_col=lambda h, r: qh + 4 * kvh + h,
                            v_col=lambda h, r: qh + 5 * kvh + h, **nsa_kw)
    o_a = nsa_merge(o_cmp, o_slc, o_win, small)

    f_tok = forget_cumsum(small, p["b_forget_row"], B=B, T=T)
    tkf = min(256, T)
    f_rows = f_tok[:, FORGET_LANE0:FORGET_LANE0 + FOX_HEADS].reshape(B, T, FOX_HEADS)
    f_rows = f_rows.transpose(0, 2, 1).reshape(B * FOX_HEADS, T // tkf, tkf)
    o_b = fox_attention(proj_fox, f_tok, f_rows, B=B, T=T, tq=min(256, T), tk=tkf)

    outs, lses = [], []
    for g, (window, dil) in enumerate(DIL_PAIRS):
        Ts = T // dil
        blk = [t * DIL_GROUPS + g for t in range(3)]
        if dil == 1:
            views = [proj_dil] * 3
            cols = [lambda h, r, c=c: c for c in blk]
        else:
            views = [proj_dil[:, c * OUT_C:(c + 1) * OUT_C].reshape(B * Ts, dil * OUT_C) for c in blk]
            cols = [lambda h, r: r] * 3
        o, lse = alibi_attention(
            *views, dil_slopes[g] * np.float32(dil), B=B, Ts=Ts, n_sub=dil, n_head_blocks=1,
            shared_kv=False, max_dist=window // dil, tq=128, tk=128, with_lse=True,
            q_col=cols[0], k_col=cols[1], v_col=cols[2])
        outs.append(o.reshape(N, OUT_C))
        lses.append(lse.reshape(N, OUT_C))
    o_c = dil_merge(outs, lses)

    z = gated_branches(u, p["w_gate"], p["b_gate"], lyr, o_a, o_b, o_c, p["wb_a"], p["wb_b"], p["wb_c"])
    return matmul(z, p["w_out"], F32, layer=lyr)


def _peer_ffn(u, ut, p):
    cpt = 4
    q = matmul(u, p["peer_wq"], BF16, layer=p["layer"])
    e1, l1, e2, r2 = peer_select(q, p["peer_subkeys"], cpt=cpt)
    return peer_dense(ut, p["peer_u"], p["peer_v"], p["layer"], e1, l1, e2, r2, cpt=cpt)


def _layer_params(l, shared, b_forget, cmp_pe, cmp_w1, cmp_w2, w_branch, peer_subkeys):
    b_row = jnp.zeros((1, LANES), F32).at[0, FORGET_LANE0:FORGET_LANE0 + FOX_HEADS].set(b_forget[l])
    half = CMP_LEN // 2
    return dict(
        shared,
        layer=l,
        b_forget_row=b_row,
        cmp_pe=cmp_pe[l].reshape(2, 2, 1, half * HEAD_DIM),
        cmp_w1=cmp_w1[l].astype(BF16).reshape(2, 2, half * HEAD_DIM, HEAD_DIM),
        cmp_w2=cmp_w2[l].astype(BF16),
        wb_a=w_branch[l, :OUT_A].astype(BF16),
        wb_b=w_branch[l, OUT_A:OUT_A + OUT_B].astype(BF16),
        wb_c=w_branch[l, OUT_A + OUT_B:].astype(BF16),
        peer_subkeys=peer_subkeys[l].astype(BF16),
    )


def kernel(x, c, w_ada, b_ada, w_in, b_forget, cmp_pe, cmp_w1, cmp_w2, w_branch, w_gate, b_gate, w_out,
           ln1_g, ln1_b, peer_wq, peer_subkeys, peer_u, peer_v, ln2_g, ln2_b):
    B, T, D = x.shape
    L = w_ada.shape[0]
    N = B * T
    c_pad = jnp.zeros((8, D), BF16).at[:B].set(c.astype(BF16))
    mod = ada_project(c_pad, w_ada, b_ada.reshape(L, 1, 6 * D))
    mods = [mod[l].reshape(8, 1, 6 * D) for l in range(L)]
    xf = x.reshape(N, D)
    u = modulate(xf, mods[0], 1, 0, T)
    shared = dict(w_in_rows=column_rows_view(w_in), n_layers=L, w_out=w_out, peer_wq=peer_wq,
                  w_gate=w_gate.astype(BF16), b_gate=b_gate.reshape(L, 1, 3 * D),
                  peer_u=peer_u.astype(BF16), peer_v=peer_v.astype(BF16))
    for l in range(L):
        p = _layer_params(l, shared, b_forget, cmp_pe, cmp_w1, cmp_w2, w_branch, peer_subkeys)
        y = _hybrid_mixer(u, p, B=B, T=T)
        xf, u, ut = postnorm(xf, y, mods[l], 2, ln1_g[l].reshape(1, D), ln1_b[l].reshape(1, D),
                             mods[l], 4, 3, T, transposed=True)
        y = _peer_ffn(u, ut, p)
        nxt = mods[min(l + 1, L - 1)]
        xf, u = postnorm(xf, y, mods[l], 5, ln2_g[l].reshape(1, D), ln2_b[l].reshape(1, D),
                         nxt, 1, 0, T)
    return xf.reshape(B, T, D)
```
